```python
import math
import jax, jax.numpy as jnp
from jax import lax
import numpy as np

D_MODEL = 1024
BATCH = 4
SEQ = 4096
DEPTH = 2
DEC_BATCH = 128
DEC_SEQ = 4
PAST_LEN = 8192
PAGE_SIZE = 128

D_MIX = D_MODEL
HEAD_DIM = 64
D_SSM = D_MIX // 2
N_SSM_HEADS = D_SSM // HEAD_DIM
N_SSM_GROUPS = 2
D_STATE = 128
CONV_W = 4
CONV_DIM = D_SSM + 2 * N_SSM_GROUPS * D_STATE
SSD_CHUNK = 128
D_ATTN = D_MIX - D_SSM
N_HEADS = D_ATTN // HEAD_DIM
N_KV_HEADS = 2
Q_PER_KV = N_HEADS // N_KV_HEADS
WINDOW = 128
ATTN_BLOCK = 128
D_FF = 2752
D_PLE = 256
D_PROJ = D_SSM + CONV_DIM + N_SSM_HEADS + D_ATTN + 2 * N_KV_HEADS * HEAD_DIM
RMS_EPS = 1e-6

kernel_name = 'hymba_ssd_swa_sink_macaron_step'


def _rmsnorm(x, w):
    xf = x.astype(jnp.float32)
    y = xf * lax.rsqrt(jnp.mean(xf * xf, axis=-1, keepdims=True) + RMS_EPS)
    return (y * w.astype(jnp.float32)).astype(x.dtype)


def _swiglu(x, w1, w3, w2):
    return (jax.nn.silu(x @ w1) * (x @ w3)) @ w2


def _alibi_slopes():
    hh = np.arange(1, N_HEADS + 1, dtype=np.float32)
    return jnp.asarray(np.power(np.float32(2.0), -8.0 * hh / N_HEADS), dtype=jnp.float32)


def _causal_conv(xbc, buf, w, b):
    full = jnp.concatenate([buf.astype(xbc.dtype), xbc], axis=1)
    out = lax.conv_general_dilated(full, w.astype(xbc.dtype)[:, None, :], window_strides=(1,),
                                   padding='VALID', dimension_numbers=('NWC', 'WIO', 'NWC'),
                                   feature_group_count=CONV_DIM)
    return jax.nn.silu(out + b.astype(out.dtype)), full[:, -(CONV_W - 1):]


def _ssd(x, dt, a, bm, cm, h0):
    f32 = jnp.float32
    bsz, L = x.shape[0], x.shape[1]
    T = SSD_CHUNK if L % SSD_CHUNK == 0 else L
    nc = L // T
    rep = N_SSM_HEADS // N_SSM_GROUPS
    xc = x.astype(f32).reshape(bsz, nc, T, N_SSM_HEADS, HEAD_DIM)
    dtc = dt.reshape(bsz, nc, T, N_SSM_HEADS)
    bc = jnp.repeat(bm.astype(f32), rep, axis=2).reshape(bsz, nc, T, N_SSM_HEADS, D_STATE)
    cc = jnp.repeat(cm.astype(f32), rep, axis=2).reshape(bsz, nc, T, N_SSM_HEADS, D_STATE)
    cum = jnp.cumsum(dtc * a, axis=2)
    tt = np.arange(T)
    causal = tt[:, None] >= tt[None, :]
    seg = jnp.transpose(cum, (0, 1, 3, 2))
    lmat = jnp.exp(jnp.where(causal, seg[..., :, None] - seg[..., None, :], -jnp.inf))
    xdt = xc * dtc[..., None]
    scores = jnp.einsum('bcthn,bcshn->bchts', cc, bc) * lmat
    y = jnp.einsum('bchts,bcshp->bcthp', scores, xdt)
    decay_end = jnp.exp(cum[:, :, -1:, :] - cum)
    states = jnp.einsum('bcthn,bcthp->bchpn', bc * decay_end[..., None], xdt)
    chunk_decay = jnp.exp(cum[:, :, -1, :])

    def step(hs, inp):
        st, dec = inp
        return hs * dec[:, :, None, None] + st, hs

    h_last, h_in = lax.scan(step, h0.astype(f32),
                            (jnp.transpose(states, (1, 0, 2, 3, 4)), jnp.transpose(chunk_decay, (1, 0, 2))))
    h_in = jnp.transpose(h_in, (1, 0, 2, 3, 4))
    y = y + jnp.einsum('bcthn,bchpn->bcthp', cc * jnp.exp(cum)[..., None], h_in)
    return y.reshape(bsz, L, N_SSM_HEADS, HEAD_DIM), h_last


def _window_attention(q, k, v, k_past, v_past, pos0, sinks):
    f32 = jnp.float32
    bsz, L = q.shape[0], q.shape[1]
    qb_len = ATTN_BLOCK if L % ATTN_BLOCK == 0 else L
    nb = L // qb_len
    span = qb_len + WINDOW
    k_full = jnp.concatenate([k_past.astype(k.dtype), k], axis=1)
    v_full = jnp.concatenate([v_past.astype(v.dtype), v], axis=1)
    kidx = np.arange(nb)[:, None] * qb_len + np.arange(span)[None, :]
    kb = k_full[:, kidx].astype(f32)
    vb = v_full[:, kidx].astype(f32)
    qb = q.astype(f32).reshape(bsz, nb, qb_len, N_KV_HEADS, Q_PER_KV, HEAD_DIM)
    qpos = pos0 + np.arange(L).reshape(nb, qb_len)
    kpos = pos0 - WINDOW + kidx
    rel = qpos[:, :, None] - kpos[:, None, :]
    valid = (rel >= 0) & (rel < WINDOW) & (kpos[:, None, :] >= 0)
    slopes = _alibi_slopes().reshape(N_KV_HEADS, Q_PER_KV)
    bias = -slopes[None, :, :, None, None] * jnp.asarray(rel, f32)[:, None, None]
    s = jnp.einsum('bnqkgd,bnskd->bnkgqs', qb, kb) * (HEAD_DIM ** -0.5)
    s = jnp.where(valid[:, None, None], s + bias, -jnp.inf)
    sink = sinks.astype(f32).reshape(N_KV_HEADS, Q_PER_KV)[:, :, None, None]
    m = jnp.maximum(jnp.max(s, axis=-1, keepdims=True), sink)
    pr = jnp.exp(s - m)
    denom = jnp.sum(pr, axis=-1, keepdims=True) + jnp.exp(sink - m)
    o = jnp.einsum('bnkgqs,bnskd->bnqkgd', pr / denom, vb)
    return o.reshape(bsz, L, N_HEADS * HEAD_DIM), k_full[:, -WINDOW:], v_full[:, -WINDOW:]


def _layer(h, pe, ssm0, conv0, k_past, v_past, pos0, lp):
    f32 = jnp.float32
    bsz, L, _ = h.shape
    hd = h.dtype
    h = h + (0.5 * _swiglu(_rmsnorm(h, lp['g_ffn1']), lp['w1_a'], lp['w3_a'], lp['w2_a'])).astype(hd)
    u = _rmsnorm(h, lp['g_mix'])
    proj = u @ lp['w_in']
    cuts = [D_SSM, D_SSM + CONV_DIM, D_SSM + CONV_DIM + N_SSM_HEADS,
            D_SSM + CONV_DIM + N_SSM_HEADS + D_ATTN,
            D_SSM + CONV_DIM + N_SSM_HEADS + D_ATTN + N_KV_HEADS * HEAD_DIM]
    z, xbc, dt_raw, q, k, v = jnp.split(proj, cuts, axis=-1)
    xbc, conv_new = _causal_conv(xbc, conv0, lp['conv_w'], lp['conv_b'])
    xs, bm, cm = jnp.split(xbc, [D_SSM, D_SSM + N_SSM_GROUPS * D_STATE], axis=-1)
    xs = xs.reshape(bsz, L, N_SSM_HEADS, HEAD_DIM)
    dt = jax.nn.softplus(dt_raw.astype(f32) + lp['dt_bias'].astype(f32))
    a = -jnp.exp(lp['a_log'].astype(f32))
    y, ssm_new = _ssd(xs, dt, a, bm.reshape(bsz, L, N_SSM_GROUPS, D_STATE),
                      cm.reshape(bsz, L, N_SSM_GROUPS, D_STATE), ssm0)
    y = y + lp['d_skip'].astype(f32)[:, None] * xs.astype(f32)
    y = y.reshape(bsz, L, D_SSM) * jax.nn.silu(z.astype(f32))
    y = _rmsnorm(y.reshape(bsz, L, N_SSM_GROUPS, D_SSM // N_SSM_GROUPS),
                 lp['ssm_norm'].reshape(N_SSM_GROUPS, D_SSM // N_SSM_GROUPS)).reshape(bsz, L, D_SSM)
    qh = _rmsnorm(q.reshape(bsz, L, N_HEADS, HEAD_DIM), lp['q_norm'])
    kh = _rmsnorm(k.reshape(bsz, L, N_KV_HEADS, HEAD_DIM), lp['k_norm'])
    vh = v.reshape(bsz, L, N_KV_HEADS, HEAD_DIM)
    ya, k_new, v_new = _window_attention(qh, kh, vh, k_past, v_past, pos0, lp['sinks'])
    mix = jnp.concatenate([y.astype(hd), ya.astype(hd)], axis=-1) @ lp['w_out']
    h = h + mix.astype(hd)
    h = h + (0.5 * _swiglu(_rmsnorm(h, lp['g_ffn2']), lp['w1_b'], lp['w3_b'], lp['w2_b'])).astype(hd)
    gate = jax.nn.sigmoid(_rmsnorm(h, lp['g_ple']) @ lp['w_ple_gate'])
    h = h + (gate * (pe.astype(hd) @ lp['w_ple_proj'])).astype(hd)
    return h, ssm_new, conv_new, k_new, v_new


def setup_inputs(seed: int = 0) -> dict:
    key = jax.random.key(seed)
    ks = jax.random.split(key, 40)
    f32 = jnp.float32

    def nrm(k, shape, scale):
        return jax.random.normal(k, shape, f32) * scale

    def gain(k, shape):
        return 1.0 + 0.05 * jax.random.normal(k, shape, f32)

    dtv = jnp.exp(jax.random.uniform(ks[20], (DEPTH, N_SSM_HEADS), f32) * (math.log(0.1) - math.log(0.001)) + math.log(0.001))
    return {
        'x_prompt': nrm(ks[0], (BATCH, SEQ, D_MODEL), 1.0),
        'x_sample': nrm(ks[1], (DEC_BATCH, DEC_SEQ, D_MODEL), 1.0),
        'state_ssm': nrm(ks[2], (DEPTH, DEC_BATCH, N_SSM_HEADS, HEAD_DIM, D_STATE), 0.5),
        'state_conv': nrm(ks[3], (DEPTH, DEC_BATCH, CONV_W - 1, CONV_DIM), 1.0),
        'cache_k_win': nrm(ks[4], (DEPTH, DEC_BATCH, WINDOW, N_KV_HEADS, HEAD_DIM), 1.0),
        'cache_v_win': nrm(ks[5], (DEPTH, DEC_BATCH, WINDOW, N_KV_HEADS, HEAD_DIM), 1.0),
        'p_prompt': nrm(ks[6], (DEPTH, BATCH, SEQ, D_PLE), 1.0),
        'p_sample': nrm(ks[7], (DEPTH, DEC_BATCH, DEC_SEQ, D_PLE), 1.0),
        'g_ffn1': gain(ks[8], (DEPTH, D_MODEL)),
        'w1_a': nrm(ks[9], (DEPTH, D_MODEL, D_FF), D_MODEL ** -0.5),
        'w3_a': nrm(ks[10], (DEPTH, D_MODEL, D_FF), D_MODEL ** -0.5),
        'w2_a': nrm(ks[11], (DEPTH, D_FF, D_MODEL), D_FF ** -0.5),
        'g_mix': gain(ks[12], (DEPTH, D_MODEL)),
        'w_in': nrm(ks[13], (DEPTH, D_MODEL, D_PROJ), D_MODEL ** -0.5),
        'conv_w': nrm(ks[14], (DEPTH, CONV_W, CONV_DIM), CONV_W ** -0.5),
        'conv_b': nrm(ks[15], (DEPTH, CONV_DIM), 0.02),
        'dt_bias': dtv + jnp.log(-jnp.expm1(-dtv)),
        'a_log': jnp.log(jax.random.uniform(ks[16], (DEPTH, N_SSM_HEADS), f32, 1.0, 16.0)),
        'd_skip': gain(ks[17], (DEPTH, N_SSM_HEADS)),
        'ssm_norm': gain(ks[18], (DEPTH, D_SSM)),
        'q_norm': gain(ks[19], (DEPTH, HEAD_DIM)),
        'k_norm': gain(ks[21], (DEPTH, HEAD_DIM)),
        'sinks': nrm(ks[22], (DEPTH, N_HEADS), 0.5),
        'w_out': nrm(ks[23], (DEPTH, D_MIX, D_MODEL), D_MIX ** -0.5),
        'g_ffn2': gain(ks[24], (DEPTH, D_MODEL)),
        'w1_b': nrm(ks[25], (DEPTH, D_MODEL, D_FF), D_MODEL ** -0.5),
        'w3_b': nrm(ks[26], (DEPTH, D_MODEL, D_FF), D_MODEL ** -0.5),
        'w2_b': nrm(ks[27], (DEPTH, D_FF, D_MODEL), D_FF ** -0.5),
        'g_ple': gain(ks[28], (DEPTH, D_MODEL)),
        'w_ple_gate': nrm(ks[29], (DEPTH, D_MODEL, D_MODEL), D_MODEL ** -0.5),
        'w_ple_proj': nrm(ks[30], (DEPTH, D_PLE, D_MODEL), D_PLE ** -0.5),
    }


def reference(x_prompt, x_sample, state_ssm, state_conv, cache_k_win, cache_v_win, p_prompt, p_sample,
              g_ffn1, w1_a, w3_a, w2_a, g_mix, w_in, conv_w, conv_b, dt_bias, a_log, d_skip, ssm_norm,
              q_norm, k_norm, sinks, w_out, g_ffn2, w1_b, w3_b, w2_b, g_ple, w_ple_gate, w_ple_proj):
    hp = x_prompt
    hs = x_sample
    bp = x_prompt.shape[0]
    dtp = x_prompt.dtype
    ssm_p, conv_p, k_p, v_p = [], [], [], []
    ssm_s, conv_s, k_s, v_s = [], [], [], []
    for l in range(DEPTH):
        lp = {'g_ffn1': g_ffn1[l], 'w1_a': w1_a[l], 'w3_a': w3_a[l], 'w2_a': w2_a[l],
              'g_mix': g_mix[l], 'w_in': w_in[l], 'conv_w': conv_w[l], 'conv_b': conv_b[l],
              'dt_bias': dt_bias[l], 'a_log': a_log[l], 'd_skip': d_skip[l], 'ssm_norm': ssm_norm[l],
              'q_norm': q_norm[l], 'k_norm': k_norm[l], 'sinks': sinks[l], 'w_out': w_out[l],
              'g_ffn2': g_ffn2[l], 'w1_b': w1_b[l], 'w3_b': w3_b[l], 'w2_b': w2_b[l],
              'g_ple': g_ple[l], 'w_ple_gate': w_ple_gate[l], 'w_ple_proj': w_ple_proj[l]}
        ssm0 = jnp.zeros((bp, N_SSM_HEADS, HEAD_DIM, D_STATE), jnp.float32)
        conv0 = jnp.zeros((bp, CONV_W - 1, CONV_DIM), dtp)
        kv0 = jnp.zeros((bp, WINDOW, N_KV_HEADS, HEAD_DIM), dtp)
        hp, s1, c1, k1, v1 = _layer(hp, p_prompt[l], ssm0, conv0, kv0, kv0, 0, lp)
        ssm_p.append(s1); conv_p.append(c1); k_p.append(k1); v_p.append(v1)
        hs, s2, c2, k2, v2 = _layer(hs, p_sample[l], state_ssm[l], state_conv[l], cache_k_win[l],
                                    cache_v_win[l], PAST_LEN, lp)
        ssm_s.append(s2); conv_s.append(c2); k_s.append(k2); v_s.append(v2)
    return (hp, hs, jnp.stack(ssm_p), jnp.stack(conv_p), jnp.stack(k_p), jnp.stack(v_p),
            jnp.stack(ssm_s), jnp.stack(conv_s), jnp.stack(k_s), jnp.stack(v_s))
```

```python
import functools

import numpy as np
import jax
import jax.numpy as jnp
from jax import lax
from jax.experimental import pallas as pl
from jax.experimental.pallas import tpu as pltpu

F32 = jnp.float32
BF16 = jnp.bfloat16

D_MODEL = 1024
HEAD_DIM = 64
D_SSM = 512
N_SSM_HEADS = 8
N_SSM_GROUPS = 2
D_STATE = 128
CONV_W = 4
CONV_DIM = D_SSM + 2 * N_SSM_GROUPS * D_STATE
SSD_CHUNK = 128
D_ATTN = 512
N_HEADS = 8
N_KV_HEADS = 2
Q_PER_KV = N_HEADS // N_KV_HEADS
WINDOW = 128
D_FF = 2752
D_PLE = 256
RMS_EPS = 1e-6

LANES = 128
SUBLANES = 8
MXU_DIM = 256
FF_CHUNK = MXU_DIM
D_FF_PAD = -(-D_FF // FF_CHUNK) * FF_CHUNK
TOKEN_TILE = 512
SAMPLE_SEQS_PER_STEP = 8
KEY_PAD = 2 * WINDOW
VMEM_LIMIT = 56 * 1024 * 1024

OFF_Z = 0
OFF_XBC = OFF_Z + D_SSM
OFF_DT = OFF_XBC + CONV_DIM
OFF_Q = OFF_DT + D_SSM
OFF_K = OFF_Q + D_ATTN
OFF_V = OFF_K + N_KV_HEADS * HEAD_DIM
D_PROJ_PAD = OFF_V + N_KV_HEADS * HEAD_DIM

ALIBI_SLOPES = tuple(float(s) for s in np.power(
    np.float32(2.0), -8.0 * np.arange(1, N_HEADS + 1, dtype=np.float32) / N_HEADS))

_NT = (((1,), (1,)), ((), ()))


def _mm(a, b):
    return jnp.dot(a, b, preferred_element_type=F32)


def _mm_nt(a, b):
    return lax.dot_general(a, b, _NT, preferred_element_type=F32)


def _sigmoid(x):
    return 1.0 / (1.0 + jnp.exp(-x))


def _silu(x):
    return x * _sigmoid(x)


def _softplus(x):
    return jnp.maximum(x, 0.0) + jnp.log(1.0 + jnp.exp(-jnp.abs(x)))


def _rmsnorm(x, g):
    return x * lax.rsqrt(jnp.mean(x * x, axis=-1, keepdims=True) + RMS_EPS) * g


def _swiglu(xn, w1_ref, w3_ref, w2_ref):
    acc = jnp.zeros((xn.shape[0], D_MODEL), F32)
    for c in range(D_FF_PAD // FF_CHUNK):
        sl = slice(c * FF_CHUNK, (c + 1) * FF_CHUNK)
        a = _mm(xn, w1_ref[:, sl])
        b = _mm(xn, w3_ref[:, sl])
        acc = acc + _mm((_silu(a) * b).astype(BF16), w2_ref[sl, :])
    return acc


def _segment_sumsq(x, seg_ref):
    sq = x * x
    hi = sq.astype(BF16)
    lo = (sq - hi.astype(F32)).astype(BF16)
    seg = seg_ref[0:x.shape[1], 0:x.shape[1]]
    return _mm(hi, seg) + _mm(lo, seg)


def _stage_a_kernel(x_ref, g1_ref, w1_ref, w3_ref, w2_ref, gm_ref, win_ref, qg_ref, kg_ref, seg_ref,
                    h_ref, z_ref, xbc_ref, dt_ref, q_ref, k_ref, v_ref):
    x = x_ref[...]
    h = x + 0.5 * _swiglu(_rmsnorm(x, g1_ref[...]).astype(BF16), w1_ref, w3_ref, w2_ref)
    h_ref[...] = h
    proj = _mm(_rmsnorm(h, gm_ref[...]).astype(BF16), win_ref[...])
    z_ref[...] = proj[:, OFF_Z:OFF_XBC]
    xbc_ref[...] = proj[:, OFF_XBC:OFF_DT]
    dt_ref[...] = proj[:, OFF_DT:OFF_Q]
    q = proj[:, OFF_Q:OFF_K]
    k = proj[:, OFF_K:OFF_V]
    q_ref[...] = q * lax.rsqrt(_segment_sumsq(q, seg_ref) * (1.0 / HEAD_DIM) + RMS_EPS) * qg_ref[...]
    k_ref[...] = k * lax.rsqrt(_segment_sumsq(k, seg_ref) * (1.0 / HEAD_DIM) + RMS_EPS) * kg_ref[...]
    v_ref[...] = proj[:, OFF_V:D_PROJ_PAD]


def _resident(shape):
    return pl.BlockSpec(shape, lambda *_: (0,) * len(shape), pipeline_mode=pl.Buffered(1))


def _stage_a(x, lw):
    n = x.shape[0]
    tm = min(TOKEN_TILE, n)
    row = lambda width: pl.BlockSpec((tm, width), lambda i: (i, 0))
    widths = (D_MODEL, D_SSM, CONV_DIM, D_SSM, D_ATTN, N_KV_HEADS * HEAD_DIM, N_KV_HEADS * HEAD_DIM)
    consts = (lw['g_ffn1'], lw['w1_a'], lw['w3_a'], lw['w2_a'], lw['g_mix'], lw['w_in'],
              lw['q_gain'], lw['k_gain'], lw['seg'])
    return pl.pallas_call(
        _stage_a_kernel,
        grid=(n // tm,),
        in_specs=[row(D_MODEL)] + [_resident(c.shape) for c in consts],
        out_specs=[row(w) for w in widths],
        out_shape=[jax.ShapeDtypeStruct((n, w), F32) for w in widths],
        compiler_params=pltpu.CompilerParams(dimension_semantics=("arbitrary",),
                                             vmem_limit_bytes=VMEM_LIMIT),
        name="stage_a",
    )(x, *consts)


def _stage_c_kernel(h_ref, mix_ref, pe_ref, wout_ref, g2_ref, w1_ref, w3_ref, w2_ref, gp_ref,
                    wgate_ref, wproj_ref, o_ref):
    h = h_ref[...] + _mm(mix_ref[...].astype(BF16), wout_ref[...])
    h = h + 0.5 * _swiglu(_rmsnorm(h, g2_ref[...]).astype(BF16), w1_ref, w3_ref, w2_ref)
    gate = _sigmoid(_mm(_rmsnorm(h, gp_ref[...]).astype(BF16), wgate_ref[...]))
    o_ref[...] = h + gate * _mm(pe_ref[...].astype(BF16), wproj_ref[...])


def _stage_c(h, mix, pe, lw):
    n = h.shape[0]
    tm = min(TOKEN_TILE, n)
    row = lambda width: pl.BlockSpec((tm, width), lambda i: (i, 0))
    consts = (lw['w_out'], lw['g_ffn2'], lw['w1_b'], lw['w3_b'], lw['w2_b'], lw['g_ple'],
              lw['w_ple_gate'], lw['w_ple_proj'])
    return pl.pallas_call(
        _stage_c_kernel,
        grid=(n // tm,),
        in_specs=[row(D_MODEL), row(D_MODEL), row(D_PLE)] + [_resident(c.shape) for c in consts],
        out_specs=row(D_MODEL),
        out_shape=jax.ShapeDtypeStruct((n, D_MODEL), F32),
        compiler_params=pltpu.CompilerParams(dimension_semantics=("arbitrary",),
                                             vmem_limit_bytes=VMEM_LIMIT),
        name="stage_c",
    )(h, mix, pe, *consts)


def _lane_low_half(shape):
    return lax.broadcasted_iota(jnp.int32, shape, len(shape) - 1) < HEAD_DIM


def _split_pair(x):
    rolled = pltpu.roll(x, HEAD_DIM, 1)
    low = _lane_low_half(x.shape)
    return jnp.where(low, x, rolled), jnp.where(low, rolled, x)


def _attention(q, kk, vv, sinks_ref, first_block):
    r = q.shape[0]
    low = _lane_low_half((r, LANES))
    t_i = lax.broadcasted_iota(jnp.int32, (r, KEY_PAD), 0)
    s_i = lax.broadcasted_iota(jnp.int32, (r, KEY_PAD), 1)
    rel = t_i + WINDOW - s_i
    valid = (rel >= 0) & (rel < WINDOW)
    if first_block is not None:
        valid = valid & (jnp.logical_not(first_block) | (s_i >= WINDOW))
    relf = rel.astype(F32)
    k_dup = _split_pair(kk)
    v_dup = _split_pair(vv)
    outs = []
    for g in range(N_KV_HEADS):
        blocks = []
        for jj in range(Q_PER_KV // 2):
            qp = q[:, (g * 2 + jj) * LANES:(g * 2 + jj + 1) * LANES]
            blocks += [jnp.where(low, qp, 0.0), jnp.where(low, 0.0, qp)]
        lhs = jnp.concatenate(blocks, axis=0).astype(BF16)
        s = _mm_nt(lhs, k_dup[g].astype(BF16)) * (HEAD_DIM ** -0.5)
        probs = []
        for hh in range(Q_PER_KV):
            head = g * Q_PER_KV + hh
            sh = jnp.where(valid, s[hh * r:(hh + 1) * r] - ALIBI_SLOPES[head] * relf, -jnp.inf)
            sink = sinks_ref[head:head + 1, 0:1]
            m = jnp.maximum(jnp.max(sh, axis=-1, keepdims=True), sink)
            pr = jnp.exp(sh - m)
            den = jnp.sum(pr, axis=-1, keepdims=True) + jnp.exp(sink - m)
            probs.append(pr / den)
        o = _mm(jnp.concatenate(probs, axis=0).astype(BF16), v_dup[g].astype(BF16))
        for jj in range(Q_PER_KV // 2):
            outs.append(jnp.where(low, o[(2 * jj) * r:(2 * jj + 1) * r], o[(2 * jj + 1) * r:(2 * jj + 2) * r]))
    return jnp.concatenate(outs, axis=1)


def _gated_group_norm(y, z, gain):
    y = y * _silu(z)
    gw = D_SSM // N_SSM_GROUPS
    parts = [_rmsnorm(y[:, g * gw:(g + 1) * gw], gain[:, g * gw:(g + 1) * gw]) for g in range(N_SSM_GROUPS)]
    return jnp.concatenate(parts, axis=1)


def _mixer_prompt_kernel(z_ref, xbc_ref, xprev_ref, dt_ref, q_ref, kc_ref, kp_ref, vc_ref, vp_ref,
                         cw_ref, cb_ref, dtb_ref, alog_ref, dskip_ref, snorm_ref, sinks_ref,
                         mix_ref, state_ref, st_scr, xe_scr):
    t = SSD_CHUNK
    c = pl.program_id(1)
    first = c == 0

    @pl.when(first)
    def _():
        st_scr[...] = jnp.zeros_like(st_scr)

    x = xbc_ref[...]
    xe_scr[0:SUBLANES, :] = jnp.where(first, 0.0, xprev_ref[...])
    xe_scr[SUBLANES:SUBLANES + t, :] = x
    cw = cw_ref[...]
    conv = cb_ref[...] + cw[CONV_W - 1:CONV_W] * x
    for back in range(1, CONV_W):
        conv = conv + cw[CONV_W - 1 - back:CONV_W - back] * xe_scr[SUBLANES - back:SUBLANES - back + t, :]
    xc = _silu(conv)
    xs = xc[:, 0:D_SSM]
    bm = xc[:, D_SSM:D_SSM + N_SSM_GROUPS * D_STATE]
    cm = xc[:, D_SSM + N_SSM_GROUPS * D_STATE:CONV_DIM]

    dt = _softplus(dt_ref[...] + dtb_ref[...])
    dta = dt * (-jnp.exp(alog_ref[...]))
    row = lax.broadcasted_iota(jnp.int32, (t, t), 0)
    col = lax.broadcasted_iota(jnp.int32, (t, t), 1)
    causal = row >= col
    cum = jnp.dot(causal.astype(F32), dta, preferred_element_type=F32, precision=lax.Precision.HIGHEST)
    cum_last = cum[t - 1:t, :]
    xdt = xs * dt
    xdt_b = xdt.astype(BF16)
    xw_b = (xdt * jnp.exp(cum_last - cum)).astype(BF16)
    st = st_scr[...]
    st_b = st.astype(BF16)
    low_b = _lane_low_half((t, LANES))
    hw = D_SSM // N_SSM_GROUPS
    y_in, y_off, st_new = [], [], []
    for g in range(N_SSM_GROUPS):
        bg = bm[:, g * D_STATE:(g + 1) * D_STATE]
        cg_b = cm[:, g * D_STATE:(g + 1) * D_STATE].astype(BF16)
        cb_t = _mm_nt(cg_b, bg.astype(BF16))
        y_off.append(_mm(cg_b, st_b[:, g * hw:(g + 1) * hw]))
        st_new.append(_mm(bg.T.astype(BF16), xw_b[:, g * hw:(g + 1) * hw]))
        for jj in range(hw // LANES):
            j = g * (hw // LANES) + jj
            scores = []
            for ch in _split_pair(cum[:, j * LANES:(j + 1) * LANES]):
                decay = jnp.exp(jnp.where(causal, ch - ch.T, -jnp.inf))
                scores.append((cb_t * decay).astype(BF16))
            xp = xdt_b[:, j * LANES:(j + 1) * LANES]
            zero = jnp.zeros_like(xp)
            rhs = jnp.concatenate([jnp.where(low_b, xp, zero), jnp.where(low_b, zero, xp)], axis=0)
            y_in.append(_mm(jnp.concatenate(scores, axis=1), rhs))
    y = (jnp.concatenate(y_in, axis=1) + jnp.concatenate(y_off, axis=1) * jnp.exp(cum)
         + dskip_ref[...] * xs)
    st_next = st * jnp.exp(cum_last) + jnp.concatenate(st_new, axis=1)
    st_scr[...] = st_next
    mix_ref[:, 0:D_SSM] = _gated_group_norm(y, z_ref[...], snorm_ref[...]).astype(mix_ref.dtype)

    @pl.when(c == pl.num_programs(1) - 1)
    def _():
        for j in range(D_SSM // LANES):
            state_ref[j * LANES:(j + 1) * LANES, :] = st_next[:, j * LANES:(j + 1) * LANES].T

    kk = jnp.concatenate([kp_ref[...], kc_ref[...]], axis=0)
    vv = jnp.concatenate([vp_ref[...], vc_ref[...]], axis=0)
    mix_ref[:, D_SSM:D_SSM + D_ATTN] = _attention(q_ref[...], kk, vv, sinks_ref, first).astype(mix_ref.dtype)


def _mixer_prompt(z, xbc, dtr, q, k, v, lw, batch, seq):
    t = SSD_CHUNK
    nc = seq // t
    cur = lambda width: pl.BlockSpec((t, width), lambda b, c: (b * nc + c, 0))
    prev = lambda width: pl.BlockSpec((t, width), lambda b, c: (jnp.maximum(b * nc + c - 1, 0), 0))
    tail = pl.BlockSpec((SUBLANES, CONV_DIM),
                        lambda b, c: (jnp.maximum((b * nc + c) * (t // SUBLANES) - 1, 0), 0))
    kvw = N_KV_HEADS * HEAD_DIM
    consts = (lw['conv_w'], lw['conv_b'], lw['dt_bias'], lw['a_log'], lw['d_skip'], lw['ssm_norm'], lw['sinks'])
    return pl.pallas_call(
        _mixer_prompt_kernel,
        grid=(batch, nc),
        in_specs=[cur(D_SSM), cur(CONV_DIM), tail, cur(D_SSM), cur(D_ATTN), cur(kvw), prev(kvw), cur(kvw), prev(kvw)]
                 + [pl.BlockSpec(cst.shape, lambda b, c: (0, 0)) for cst in consts],
        out_specs=[pl.BlockSpec((t, D_SSM + D_ATTN), lambda b, c: (b * nc + c, 0)),
                   pl.BlockSpec((None, D_SSM, D_STATE), lambda b, c: (b, 0, 0))],
        out_shape=[jax.ShapeDtypeStruct((batch * seq, D_SSM + D_ATTN), BF16),
                   jax.ShapeDtypeStruct((batch, D_SSM, D_STATE), F32)],
        scratch_shapes=[pltpu.VMEM((D_STATE, D_SSM), F32), pltpu.VMEM((SUBLANES + t, CONV_DIM), F32)],
        compiler_params=pltpu.CompilerParams(dimension_semantics=("arbitrary", "arbitrary"),
                                             vmem_limit_bytes=VMEM_LIMIT),
        name="mixer_prompt",
    )(z, xbc, xbc, dtr, q, k, k, v, v, *consts)


def _mixer_sample_kernel(z_ref, xbc_ref, dt_ref, q_ref, k_ref, v_ref, sst_ref, sconv_ref, ck_ref, cv_ref,
                         cw_ref, cb_ref, dtb_ref, alog_ref, dskip_ref, snorm_ref, sinks_ref,
                         mix_ref, sst_out_ref, ck_out_ref, cv_out_ref,
                         xe_scr, pad_scr, kk_scr, vv_scr, q_scr):
    n_seq, dl = xbc_ref.shape[0], xbc_ref.shape[1]
    tp = pad_scr.shape[0]
    hw = D_SSM // N_SSM_GROUPS
    pad_scr[...] = jnp.zeros_like(pad_scr)
    kk_scr[...] = jnp.zeros_like(kk_scr)
    vv_scr[...] = jnp.zeros_like(vv_scr)
    q_scr[...] = jnp.zeros_like(q_scr)
    cw = cw_ref[...]
    a_rep = -jnp.exp(alog_ref[...])
    rows = lax.broadcasted_iota(jnp.int32, (dl, D_SSM), 0)

    def per_sequence(i, carry):
        x = xbc_ref[i]
        xe_scr[SUBLANES - (CONV_W - 1):SUBLANES, :] = sconv_ref[i]
        xe_scr[SUBLANES:SUBLANES + dl, :] = x
        conv = cb_ref[...] + cw[CONV_W - 1:CONV_W] * x
        for back in range(1, CONV_W):
            conv = conv + cw[CONV_W - 1 - back:CONV_W - back] * xe_scr[SUBLANES - back:SUBLANES - back + dl, :]
        xc = _silu(conv)
        xs = xc[:, 0:D_SSM]
        bm = xc[:, D_SSM:D_SSM + N_SSM_GROUPS * D_STATE]
        cm = xc[:, D_SSM + N_SSM_GROUPS * D_STATE:CONV_DIM]

        dt = _softplus(dt_ref[i] + dtb_ref[...])
        dta = dt * a_rep
        xdt = xs * dt
        bcast = lambda arr, s: jnp.broadcast_to(arr[s:s + 1, :], (dl, arr.shape[1]))
        cum = jnp.zeros_like(dta)
        for s in range(dl):
            cum = cum + jnp.where(rows >= s, bcast(dta, s), 0.0)
        cum_last = bcast(cum, dl - 1)

        y = dskip_ref[...] * xs
        for s in range(dl):
            prod = cm * bcast(bm, s)
            dots = [jnp.broadcast_to(jnp.sum(prod[:, g * D_STATE:(g + 1) * D_STATE], axis=-1, keepdims=True), (dl, hw))
                    for g in range(N_SSM_GROUPS)]
            decay = jnp.exp(jnp.where(rows >= s, cum - bcast(cum, s), -jnp.inf))
            y = y + jnp.concatenate(dots, axis=1) * decay * bcast(xdt, s)

        state = sst_ref[i]
        state_b = state.astype(BF16)
        pad_scr[0:dl, 0:D_SSM] = xdt * jnp.exp(cum_last - cum)
        pad_scr[0:dl, D_SSM:D_SSM + 2 * D_STATE] = bm
        pad_scr[0:dl, D_SSM + 2 * D_STATE:D_SSM + 4 * D_STATE] = cm
        xw_p = pad_scr[:, 0:D_SSM]
        bm_p = pad_scr[:, D_SSM:D_SSM + 2 * D_STATE].astype(BF16)
        cm_p = pad_scr[:, D_SSM + 2 * D_STATE:D_SSM + 4 * D_STATE].astype(BF16)
        y_off = [_mm_nt(cm_p[:, g * D_STATE:(g + 1) * D_STATE], state_b[g * hw:(g + 1) * hw, :])[0:dl]
                 for g in range(N_SSM_GROUPS)]
        y = y + jnp.concatenate(y_off, axis=1) * jnp.exp(cum)
        chunk_decay = jnp.exp(jnp.broadcast_to(cum[dl - 1:dl, :], (SUBLANES, D_SSM)))
        for j in range(D_SSM // LANES):
            g = j // (hw // LANES)
            upd = _mm(xw_p[:, j * LANES:(j + 1) * LANES].T.astype(BF16), bm_p[:, g * D_STATE:(g + 1) * D_STATE])
            for half, dec in enumerate(_split_pair(chunk_decay[:, j * LANES:(j + 1) * LANES])):
                r0 = j * LANES + half * HEAD_DIM
                sst_out_ref[i, r0:r0 + HEAD_DIM, :] = (
                    state[r0:r0 + HEAD_DIM] * jnp.broadcast_to(dec[0:1], (HEAD_DIM, D_STATE))
                    + upd[half * HEAD_DIM:(half + 1) * HEAD_DIM])
        mix_ref[i, :, 0:D_SSM] = _gated_group_norm(y, z_ref[i], snorm_ref[...])

        kk_scr[0:WINDOW, :] = ck_ref[i]
        kk_scr[WINDOW:WINDOW + dl, :] = k_ref[i]
        vv_scr[0:WINDOW, :] = cv_ref[i]
        vv_scr[WINDOW:WINDOW + dl, :] = v_ref[i]
        ck_out_ref[i] = kk_scr[dl:dl + WINDOW, :]
        cv_out_ref[i] = vv_scr[dl:dl + WINDOW, :]
        q_scr[0:dl, :] = q_ref[i]
        att = _attention(q_scr[...], kk_scr[...], vv_scr[...], sinks_ref, None)
        mix_ref[i, :, D_SSM:D_SSM + D_ATTN] = att[0:dl]
        return carry

    lax.fori_loop(0, n_seq, per_sequence, 0)


def _mixer_sample(z, xbc, dtr, q, k, v, state_ssm, state_conv, cache_k, cache_v, lw):
    nb, dl = xbc.shape[0], xbc.shape[1]
    bb = min(SAMPLE_SEQS_PER_STEP, nb)
    blk = lambda *dims: pl.BlockSpec((bb,) + dims, lambda i: (i,) + (0,) * len(dims))
    kvw = N_KV_HEADS * HEAD_DIM
    consts = (lw['conv_w'], lw['conv_b'], lw['dt_bias'], lw['a_log'], lw['d_skip'], lw['ssm_norm'], lw['sinks'])
    return pl.pallas_call(
        _mixer_sample_kernel,
        grid=(nb // bb,),
        in_specs=[blk(dl, D_SSM), blk(dl, CONV_DIM), blk(dl, D_SSM), blk(dl, D_ATTN), blk(dl, kvw), blk(dl, kvw),
                  blk(D_SSM, D_STATE), blk(CONV_W - 1, CONV_DIM), blk(WINDOW, kvw), blk(WINDOW, kvw)]
                 + [pl.BlockSpec(cst.shape, lambda i: (0, 0)) for cst in consts],
        out_specs=[blk(dl, D_SSM + D_ATTN), blk(D_SSM, D_STATE), blk(WINDOW, kvw), blk(WINDOW, kvw)],
        out_shape=[jax.ShapeDtypeStruct((nb, dl, D_SSM + D_ATTN), F32),
                   jax.ShapeDtypeStruct((nb, D_SSM, D_STATE), F32),
                   jax.ShapeDtypeStruct((nb, WINDOW, kvw), F32),
                   jax.ShapeDtypeStruct((nb, WINDOW, kvw), F32)],
        scratch_shapes=[pltpu.VMEM((2 * SUBLANES, CONV_DIM), F32),
                        pltpu.VMEM((SSD_CHUNK, D_SSM + 4 * D_STATE), F32),
                        pltpu.VMEM((KEY_PAD, kvw), F32),
                        pltpu.VMEM((KEY_PAD, kvw), F32),
                        pltpu.VMEM((SUBLANES, D_ATTN), F32)],
        compiler_params=pltpu.CompilerParams(dimension_semantics=("arbitrary",),
                                             vmem_limit_bytes=VMEM_LIMIT),
        name="mixer_sample",
    )(z, xbc, dtr, q, k, v, state_ssm, state_conv, cache_k, cache_v, *consts)


def _layer_weights(l, p):
    pad_ff = D_FF_PAD - D_FF
    bf = lambda w: w.astype(BF16)
    w_in = p['w_in'][l]
    cuts = np.cumsum([D_SSM, CONV_DIM, N_SSM_HEADS, D_ATTN, N_KV_HEADS * HEAD_DIM])
    wz, wxbc, wdt, wq, wk, wv = jnp.split(w_in, cuts, axis=1)
    rep = lambda vec: jnp.repeat(vec, HEAD_DIM)[None, :]
    seg = np.kron(np.eye(D_ATTN // HEAD_DIM, dtype=np.float32), np.ones((HEAD_DIM, HEAD_DIM), np.float32))
    return {
        'g_ffn1': p['g_ffn1'][l][None, :], 'g_mix': p['g_mix'][l][None, :],
        'g_ffn2': p['g_ffn2'][l][None, :], 'g_ple': p['g_ple'][l][None, :],
        'w1_a': bf(jnp.pad(p['w1_a'][l], ((0, 0), (0, pad_ff)))),
        'w3_a': bf(jnp.pad(p['w3_a'][l], ((0, 0), (0, pad_ff)))),
        'w2_a': bf(jnp.pad(p['w2_a'][l], ((0, pad_ff), (0, 0)))),
        'w1_b': bf(jnp.pad(p['w1_b'][l], ((0, 0), (0, pad_ff)))),
        'w3_b': bf(jnp.pad(p['w3_b'][l], ((0, 0), (0, pad_ff)))),
        'w2_b': bf(jnp.pad(p['w2_b'][l], ((0, pad_ff), (0, 0)))),
        'w_in': bf(jnp.concatenate([wz, wxbc, jnp.repeat(wdt, HEAD_DIM, axis=1), wq, wk, wv], axis=1)),
        'w_out': bf(p['w_out'][l]), 'w_ple_gate': bf(p['w_ple_gate'][l]), 'w_ple_proj': bf(p['w_ple_proj'][l]),
        'q_gain': jnp.tile(p['q_norm'][l], N_HEADS)[None, :],
        'k_gain': jnp.tile(p['k_norm'][l], N_KV_HEADS)[None, :],
        'seg': jnp.asarray(seg, BF16),
        'conv_w': p['conv_w'][l], 'conv_b': p['conv_b'][l][None, :],
        'dt_bias': rep(p['dt_bias'][l]), 'a_log': rep(p['a_log'][l]), 'd_skip': rep(p['d_skip'][l]),
        'ssm_norm': p['ssm_norm'][l][None, :],
        'sinks': jnp.broadcast_to(p['sinks'][l][:, None], (N_HEADS, LANES)),
    }


def kernel(x_prompt, x_sample, state_ssm, state_conv, cache_k_win, cache_v_win, p_prompt, p_sample, g_ffn1, w1_a, w3_a, w2_a, g_mix, w_in, conv_w, conv_b, dt_bias, a_log, d_skip, ssm_norm, q_norm, k_norm, sinks, w_out, g_ffn2, w1_b, w3_b, w2_b, g_ple, w_ple_gate, w_ple_proj):
    params = dict(g_ffn1=g_ffn1, w1_a=w1_a, w3_a=w3_a, w2_a=w2_a, g_mix=g_mix, w_in=w_in, conv_w=conv_w,
                  conv_b=conv_b, dt_bias=dt_bias, a_log=a_log, d_skip=d_skip, ssm_norm=ssm_norm, q_norm=q_norm,
                  k_norm=k_norm, sinks=sinks, w_out=w_out, g_ffn2=g_ffn2, w1_b=w1_b, w3_b=w3_b, w2_b=w2_b,
                  g_ple=g_ple, w_ple_gate=w_ple_gate, w_ple_proj=w_ple_proj)
    depth = w_in.shape[0]
    bp, seq, _ = x_prompt.shape
    bs, dl, _ = x_sample.shape
    kvw = N_KV_HEADS * HEAD_DIM
    assert seq % SSD_CHUNK == 0 and seq >= WINDOW and dl >= CONV_W - 1 and dl <= SUBLANES
    assert (bp * seq) % TOKEN_TILE == 0 and (bs * dl) % min(TOKEN_TILE, bs * dl) == 0

    hp = x_prompt.reshape(bp * seq, D_MODEL)
    hs = x_sample.reshape(bs * dl, D_MODEL)
    outs = [[] for _ in range(8)]
    for l in range(depth):
        lw = _layer_weights(l, params)
        hp, zp, xbcp, dtp, qp, kp, vp = _stage_a(hp, lw)
        hs, zs, xbcs, dts, qs, ks, vs = _stage_a(hs, lw)
        mixp, ssm_p = _mixer_prompt(zp, xbcp, dtp, qp, kp, vp, lw, bp, seq)
        seq3 = lambda a: a.reshape(bs, dl, a.shape[-1])
        mixs, ssm_s, k_s, v_s = _mixer_sample(
            seq3(zs), seq3(xbcs), seq3(dts), seq3(qs), seq3(ks), seq3(vs),
            state_ssm[l].reshape(bs, D_SSM, D_STATE), state_conv[l],
            cache_k_win[l].reshape(bs, WINDOW, kvw), cache_v_win[l].reshape(bs, WINDOW, kvw), lw)
        hp = _stage_c(hp, mixp, p_prompt[l].reshape(bp * seq, D_PLE), lw)
        hs = _stage_c(hs, mixs.reshape(bs * dl, D_MODEL), p_sample[l].reshape(bs * dl, D_PLE), lw)
        outs[0].append(ssm_p.reshape(bp, N_SSM_HEADS, HEAD_DIM, D_STATE))
        outs[1].append(xbcp.reshape(bp, seq, CONV_DIM)[:, seq - (CONV_W - 1):])
        outs[2].append(kp.reshape(bp, seq, N_KV_HEADS, HEAD_DIM)[:, seq - WINDOW:])
        outs[3].append(vp.reshape(bp, seq, N_KV_HEADS, HEAD_DIM)[:, seq - WINDOW:])
        outs[4].append(ssm_s.reshape(bs, N_SSM_HEADS, HEAD_DIM, D_STATE))
        outs[5].append(xbcs.reshape(bs, dl, CONV_DIM)[:, dl - (CONV_W - 1):])
        outs[6].append(k_s.reshape(bs, WINDOW, N_KV_HEADS, HEAD_DIM))
        outs[7].append(v_s.reshape(bs, WINDOW, N_KV_HEADS, HEAD_DIM))
    return (hp.reshape(bp, seq, D_MODEL), hs.reshape(bs, dl, D_MODEL)) + tuple(jnp.stack(o) for o in outs)
```

```python
import numpy as np
import jax
import jax.numpy as jnp
from jax import lax
from jax.experimental import pallas as pl
from jax.experimental.pallas import tpu as pltpu

F32 = jnp.float32
BF16 = jnp.bfloat16

D_MODEL = 1024
HEAD_DIM = 64
D_SSM = 512
N_SSM_HEADS = 8
N_SSM_GROUPS = 2
D_STATE = 128
CONV_W = 4
CONV_DIM = D_SSM + 2 * N_SSM_GROUPS * D_STATE
SSD_CHUNK = 128
D_ATTN = 512
N_HEADS = 8
N_KV_HEADS = 2
Q_PER_KV = N_HEADS // N_KV_HEADS
KV_WIDTH = N_KV_HEADS * HEAD_DIM
WINDOW = 128
D_FF = 2752
D_PLE = 256
RMS_EPS = 1e-6

LANES = 128
SUBLANES = 8
MXU_DIM = 256
FF_CHUNK = MXU_DIM
D_FF_PAD = -(-D_FF // FF_CHUNK) * FF_CHUNK
TOKEN_TILE = 512
SAMPLE_SEQS_PER_STEP = 8
KEY_PAD = 2 * WINDOW
VMEM_LIMIT = 56 * 1024 * 1024

OFF_Z = 0
OFF_XBC = OFF_Z + D_SSM
OFF_DT = OFF_XBC + CONV_DIM
OFF_Q = OFF_DT + D_SSM
OFF_K = OFF_Q + D_ATTN
OFF_V = OFF_K + KV_WIDTH
D_PROJ_PAD = OFF_V + KV_WIDTH

ALIBI_SLOPES = tuple(float(s) for s in np.power(
    np.float32(2.0), -8.0 * np.arange(1, N_HEADS + 1, dtype=np.float32) / N_HEADS))

_NT = (((1,), (1,)), ((), ()))


def _mm(a, b):
    return jnp.dot(a, b, preferred_element_type=F32)


def _mm_nt(a, b):
    return lax.dot_general(a, b, _NT, preferred_element_type=F32)


def _sigmoid(x):
    return 1.0 / (1.0 + jnp.exp(-x))


def _silu(x):
    return x * _sigmoid(x)


def _softplus(x):
    return jnp.maximum(x, 0.0) + jnp.log(1.0 + jnp.exp(-jnp.abs(x)))


def _rmsnorm(x, g):
    return x * lax.rsqrt(jnp.mean(x * x, axis=-1, keepdims=True) + RMS_EPS) * g


def _split3(x):
    hi = x.astype(BF16)
    rest = x - hi.astype(F32)
    mid = rest.astype(BF16)
    return hi, mid, (rest - mid.astype(F32)).astype(BF16)


def _swiglu(xn, w1_ref, w3_ref, w2_ref):
    acc = jnp.zeros((xn.shape[0], D_MODEL), F32)
    for c in range(D_FF_PAD // FF_CHUNK):
        sl = slice(c * FF_CHUNK, (c + 1) * FF_CHUNK)
        a = _mm(xn, w1_ref[:, sl])
        b = _mm(xn, w3_ref[:, sl])
        acc = acc + _mm((_silu(a) * b).astype(BF16), w2_ref[sl, :])
    return acc


def _segment_sumsq(x, seg_ref):
    hi, mid, _ = _split3(x * x)
    seg = seg_ref[0:x.shape[1], 0:x.shape[1]]
    return _mm(hi, seg) + _mm(mid, seg)


def _stage_a_kernel(x_ref, g1_ref, w1_ref, w3_ref, w2_ref, gm_ref, win_ref, qg_ref, kg_ref, seg_ref,
                    h_ref, z_ref, xbc_ref, dt_ref, q_ref, k_ref, v_ref):
    x = x_ref[...]
    h = x + 0.5 * _swiglu(_rmsnorm(x, g1_ref[...]).astype(BF16), w1_ref, w3_ref, w2_ref)
    h_ref[...] = h
    proj = _mm(_rmsnorm(h, gm_ref[...]).astype(BF16), win_ref[...])
    z_ref[...] = proj[:, OFF_Z:OFF_XBC]
    xbc_ref[...] = proj[:, OFF_XBC:OFF_DT]
    dt_ref[...] = proj[:, OFF_DT:OFF_Q]
    q = proj[:, OFF_Q:OFF_K]
    k = proj[:, OFF_K:OFF_V]
    q_ref[...] = q * lax.rsqrt(_segment_sumsq(q, seg_ref) * (1.0 / HEAD_DIM) + RMS_EPS) * qg_ref[...]
    k_ref[...] = k * lax.rsqrt(_segment_sumsq(k, seg_ref) * (1.0 / HEAD_DIM) + RMS_EPS) * kg_ref[...]
    v_ref[...] = proj[:, OFF_V:D_PROJ_PAD]


def _layer_block(arr, layer, **kwargs):
    return pl.BlockSpec((None,) + arr.shape[1:], lambda *_: (layer, 0, 0), **kwargs)


def _layer_resident(arr, layer):
    return _layer_block(arr, layer, pipeline_mode=pl.Buffered(1))


def _stage_a(x, pw, layer):
    n = x.shape[0]
    tm = min(TOKEN_TILE, n)
    row = lambda width: pl.BlockSpec((tm, width), lambda i: (i, 0))
    widths = (D_MODEL, D_SSM, CONV_DIM, D_SSM, D_ATTN, KV_WIDTH, KV_WIDTH)
    consts = (pw['g_ffn1'], pw['w1_a'], pw['w3_a'], pw['w2_a'], pw['g_mix'], pw['w_in'],
              pw['q_gain'], pw['k_gain'], pw['seg'])
    return pl.pallas_call(
        _stage_a_kernel,
        grid=(n // tm,),
        in_specs=[row(D_MODEL)] + [_layer_resident(c, layer) for c in consts],
        out_specs=[row(w) for w in widths],
        out_shape=[jax.ShapeDtypeStruct((n, w), F32) for w in widths],
        compiler_params=pltpu.CompilerParams(dimension_semantics=("arbitrary",),
                                             vmem_limit_bytes=VMEM_LIMIT),
        name="stage_a",
    )(x, *consts)


def _stage_c_kernel(h_ref, mix_ref, pe_ref, wout_ref, g2_ref, w1_ref, w3_ref, w2_ref, gp_ref,
                    wgate_ref, wproj_ref, o_ref):
    h = h_ref[...] + _mm(mix_ref[...].astype(BF16), wout_ref[...])
    h = h + 0.5 * _swiglu(_rmsnorm(h, g2_ref[...]).astype(BF16), w1_ref, w3_ref, w2_ref)
    gate = _sigmoid(_mm(_rmsnorm(h, gp_ref[...]).astype(BF16), wgate_ref[...]))
    o_ref[...] = h + gate * _mm(pe_ref[...].astype(BF16), wproj_ref[...])


def _stage_c(h, mix, pe, pw, layer):
    n = h.shape[0]
    tm = min(TOKEN_TILE, n)
    row = lambda width: pl.BlockSpec((tm, width), lambda i: (i, 0))
    consts = (pw['w_out'], pw['g_ffn2'], pw['w1_b'], pw['w3_b'], pw['w2_b'], pw['g_ple'],
              pw['w_ple_gate'], pw['w_ple_proj'])
    return pl.pallas_call(
        _stage_c_kernel,
        grid=(n // tm,),
        in_specs=[row(D_MODEL), row(D_MODEL), row(D_PLE)] + [_layer_resident(c, layer) for c in consts],
        out_specs=row(D_MODEL),
        out_shape=jax.ShapeDtypeStruct((n, D_MODEL), F32),
        compiler_params=pltpu.CompilerParams(dimension_semantics=("arbitrary",),
                                             vmem_limit_bytes=VMEM_LIMIT),
        name="stage_c",
    )(h, mix, pe, *consts)


def _lane_low_half(shape):
    return lax.broadcasted_iota(jnp.int32, shape, len(shape) - 1) < HEAD_DIM


def _split_pair(x):
    rolled = pltpu.roll(x, HEAD_DIM, 1)
    low = _lane_low_half(x.shape)
    return jnp.where(low, x, rolled), jnp.where(low, rolled, x)


def _head_query_rows(q, g):
    low = _lane_low_half((q.shape[0], LANES))
    blocks = []
    for jj in range(Q_PER_KV // 2):
        qp = q[:, (g * 2 + jj) * LANES:(g * 2 + jj + 1) * LANES]
        blocks += [jnp.where(low, qp, 0.0), jnp.where(low, 0.0, qp)]
    return jnp.concatenate(blocks, axis=0).astype(BF16)


def _attention_rows(q, kk, vv, sinks_ref):
    r = q.shape[0]
    low = _lane_low_half((r, LANES))
    t_i = lax.broadcasted_iota(jnp.int32, (r, KEY_PAD), 0)
    s_i = lax.broadcasted_iota(jnp.int32, (r, KEY_PAD), 1)
    rel = t_i + WINDOW - s_i
    valid = (rel >= 0) & (rel < WINDOW)
    relf = rel.astype(F32)
    k_dup = _split_pair(kk)
    v_dup = _split_pair(vv)
    outs = []
    for g in range(N_KV_HEADS):
        s = _mm_nt(_head_query_rows(q, g), k_dup[g].astype(BF16)) * (HEAD_DIM ** -0.5)
        probs = []
        for hh in range(Q_PER_KV):
            head = g * Q_PER_KV + hh
            sh = jnp.where(valid, s[hh * r:(hh + 1) * r] - ALIBI_SLOPES[head] * relf, -jnp.inf)
            sink = sinks_ref[head:head + 1, 0:1]
            m = jnp.maximum(jnp.max(sh, axis=-1, keepdims=True), sink)
            pr = jnp.exp(sh - m)
            den = jnp.sum(pr, axis=-1, keepdims=True) + jnp.exp(sink - m)
            probs.append(pr / den)
        o = _mm(jnp.concatenate(probs, axis=0).astype(BF16), v_dup[g].astype(BF16))
        for jj in range(Q_PER_KV // 2):
            outs.append(jnp.where(low, o[(2 * jj) * r:(2 * jj + 1) * r], o[(2 * jj + 1) * r:(2 * jj + 2) * r]))
    return jnp.concatenate(outs, axis=1)


def _fill_attention_bias(bias_scr):
    s_i = lax.broadcasted_iota(jnp.int32, (KEY_PAD, WINDOW), 0)
    t_i = lax.broadcasted_iota(jnp.int32, (KEY_PAD, WINDOW), 1)
    rel = t_i + WINDOW - s_i
    valid = (rel >= 0) & (rel < WINDOW)
    relf = rel.astype(F32)
    for head in range(N_HEADS):
        bias_scr[head * KEY_PAD:(head + 1) * KEY_PAD, :] = jnp.where(valid, -ALIBI_SLOPES[head] * relf, -jnp.inf)


def _attention_cols(q, kk, vv, bias_scr, sinks_ref, first_block):
    r = q.shape[0]
    k_dup = _split_pair(kk)
    vt = vv.T
    ones = jnp.ones((HEAD_DIM, KEY_PAD), F32)
    mask_prev = jnp.where(first_block, -jnp.inf, 0.0)
    qs = q * (HEAD_DIM ** -0.5)
    outs = []
    for g in range(N_KV_HEADS):
        st = _mm_nt(k_dup[g].astype(BF16), _head_query_rows(qs, g))
        probs, sink_terms = [], []
        for hh in range(Q_PER_KV):
            head = g * Q_PER_KV + hh
            sh = st[:, hh * r:(hh + 1) * r] + bias_scr[head * KEY_PAD:(head + 1) * KEY_PAD, :]
            prev = sh[0:WINDOW] + mask_prev
            cur = sh[WINDOW:KEY_PAD]
            sink = sinks_ref[head:head + 1, :]
            m = jnp.maximum(jnp.maximum(jnp.max(prev, axis=0, keepdims=True),
                                        jnp.max(cur, axis=0, keepdims=True)), sink)
            probs.append(jnp.concatenate([jnp.exp(prev - m), jnp.exp(cur - m)], axis=0))
            sink_terms.append(jnp.exp(sink - m))
        pr = jnp.concatenate(probs, axis=1).astype(BF16)
        v_ext = jnp.concatenate([vt[g * HEAD_DIM:(g + 1) * HEAD_DIM], ones], axis=0).astype(BF16)
        ot = _mm(v_ext, pr)
        den = ot[HEAD_DIM:HEAD_DIM + 1, :] + jnp.concatenate(sink_terms, axis=1)
        on = ot[0:HEAD_DIM, :] * (1.0 / den)
        for jj in range(Q_PER_KV // 2):
            pair = jnp.concatenate([on[:, (2 * jj) * r:(2 * jj + 1) * r],
                                    on[:, (2 * jj + 1) * r:(2 * jj + 2) * r]], axis=0)
            outs.append(pair.T)
    return jnp.concatenate(outs, axis=1)


def _gated_group_norm(y, z, gain):
    y = y * _silu(z)
    gw = D_SSM // N_SSM_GROUPS
    parts = [_rmsnorm(y[:, g * gw:(g + 1) * gw], gain[:, g * gw:(g + 1) * gw]) for g in range(N_SSM_GROUPS)]
    return jnp.concatenate(parts, axis=1)


def _mixer_prompt_kernel(z_ref, xbc_ref, xprev_ref, dt_ref, q_ref, kc_ref, kp_ref, vc_ref, vp_ref,
                         cw_ref, cb_ref, dtb_ref, alog_ref, dskip_ref, snorm_ref, sinks_ref,
                         mix_ref, state_ref, st_scr, xe_scr, bias_scr):
    t = SSD_CHUNK
    c = pl.program_id(1)
    first = c == 0

    @pl.when(first & (pl.program_id(0) == 0))
    def _():
        _fill_attention_bias(bias_scr)

    @pl.when(first)
    def _():
        st_scr[...] = jnp.zeros_like(st_scr)

    x = xbc_ref[...]
    xe_scr[0:SUBLANES, :] = jnp.where(first, 0.0, xprev_ref[...])
    xe_scr[SUBLANES:SUBLANES + t, :] = x
    cw = cw_ref[...]
    conv = cb_ref[...] + cw[CONV_W - 1:CONV_W] * x
    for back in range(1, CONV_W):
        conv = conv + cw[CONV_W - 1 - back:CONV_W - back] * xe_scr[SUBLANES - back:SUBLANES - back + t, :]
    xc = _silu(conv)
    xs = xc[:, 0:D_SSM]
    bm = xc[:, D_SSM:D_SSM + N_SSM_GROUPS * D_STATE]
    cm = xc[:, D_SSM + N_SSM_GROUPS * D_STATE:CONV_DIM]

    dt = _softplus(dt_ref[...] + dtb_ref[...])
    dta = dt * (-jnp.exp(alog_ref[...]))
    row = lax.broadcasted_iota(jnp.int32, (t, t), 0)
    col = lax.broadcasted_iota(jnp.int32, (t, t), 1)
    causal = row >= col
    cum3 = _mm(causal.astype(BF16), jnp.concatenate(_split3(dta), axis=1))
    cum = cum3[:, 0:D_SSM] + cum3[:, D_SSM:2 * D_SSM] + cum3[:, 2 * D_SSM:3 * D_SSM]
    cum_last = cum[t - 1:t, :]
    xdt = xs * dt
    xdt_b = xdt.astype(BF16)
    xw_b = (xdt * jnp.exp(cum_last - cum)).astype(BF16)
    st = st_scr[...]
    st_b = st.astype(BF16)
    low_b = _lane_low_half((t, LANES))
    hw = D_SSM // N_SSM_GROUPS
    y_in, y_off, st_new = [], [], []
    for g in range(N_SSM_GROUPS):
        bg = bm[:, g * D_STATE:(g + 1) * D_STATE]
        cg_b = cm[:, g * D_STATE:(g + 1) * D_STATE].astype(BF16)
        cb_t = _mm_nt(cg_b, bg.astype(BF16))
        y_off.append(_mm(cg_b, st_b[:, g * hw:(g + 1) * hw]))
        st_new.append(_mm(bg.T.astype(BF16), xw_b[:, g * hw:(g + 1) * hw]))
        for jj in range(hw // LANES):
            j = g * (hw // LANES) + jj
            scores = []
            for ch in _split_pair(cum[:, j * LANES:(j + 1) * LANES]):
                decay = jnp.exp(jnp.where(causal, ch - ch.T, -jnp.inf))
                scores.append((cb_t * decay).astype(BF16))
            xp = xdt_b[:, j * LANES:(j + 1) * LANES]
            zero = jnp.zeros_like(xp)
            rhs = jnp.concatenate([jnp.where(low_b, xp, zero), jnp.where(low_b, zero, xp)], axis=0)
            y_in.append(_mm(jnp.concatenate(scores, axis=1), rhs))
    y = (jnp.concatenate(y_in, axis=1) + jnp.concatenate(y_off, axis=1) * jnp.exp(cum)
         + dskip_ref[...] * xs)
    st_next = st * jnp.exp(cum_last) + jnp.concatenate(st_new, axis=1)
    st_scr[...] = st_next
    mix_ref[:, 0:D_SSM] = _gated_group_norm(y, z_ref[...], snorm_ref[...]).astype(mix_ref.dtype)

    @pl.when(c == pl.num_programs(1) - 1)
    def _():
        for j in range(D_SSM // LANES):
            state_ref[j * LANES:(j + 1) * LANES, :] = st_next[:, j * LANES:(j + 1) * LANES].T

    kk = jnp.concatenate([kp_ref[...], kc_ref[...]], axis=0)
    vv = jnp.concatenate([vp_ref[...], vc_ref[...]], axis=0)
    mix_ref[:, D_SSM:D_SSM + D_ATTN] = _attention_cols(
        q_ref[...], kk, vv, bias_scr, sinks_ref, first).astype(mix_ref.dtype)


def _mixer_prompt(z, xbc, dtr, q, k, v, pw, layer, batch, seq):
    t = SSD_CHUNK
    nc = seq // t
    cur = lambda width: pl.BlockSpec((t, width), lambda b, c: (b * nc + c, 0))
    prev = lambda width: pl.BlockSpec((t, width), lambda b, c: (jnp.maximum(b * nc + c - 1, 0), 0))
    tail = pl.BlockSpec((SUBLANES, CONV_DIM),
                        lambda b, c: (jnp.maximum((b * nc + c) * (t // SUBLANES) - 1, 0), 0))
    consts = (pw['conv_w'], pw['conv_b'], pw['dt_bias'], pw['a_log'], pw['d_skip'], pw['ssm_norm'], pw['sinks'])
    return pl.pallas_call(
        _mixer_prompt_kernel,
        grid=(batch, nc),
        in_specs=[cur(D_SSM), cur(CONV_DIM), tail, cur(D_SSM), cur(D_ATTN),
                  cur(KV_WIDTH), prev(KV_WIDTH), cur(KV_WIDTH), prev(KV_WIDTH)]
                 + [_layer_block(cst, layer) for cst in consts],
        out_specs=[pl.BlockSpec((t, D_SSM + D_ATTN), lambda b, c: (b * nc + c, 0)),
                   pl.BlockSpec((None, D_SSM, D_STATE), lambda b, c: (b, 0, 0))],
        out_shape=[jax.ShapeDtypeStruct((batch * seq, D_SSM + D_ATTN), BF16),
                   jax.ShapeDtypeStruct((batch, D_SSM, D_STATE), F32)],
        scratch_shapes=[pltpu.VMEM((D_STATE, D_SSM), F32),
                        pltpu.VMEM((SUBLANES + t, CONV_DIM), F32),
                        pltpu.VMEM((N_HEADS * KEY_PAD, WINDOW), F32)],
        compiler_params=pltpu.CompilerParams(dimension_semantics=("arbitrary", "arbitrary"),
                                             vmem_limit_bytes=VMEM_LIMIT),
        name="mixer_prompt",
    )(z, xbc, xbc, dtr, q, k, k, v, v, *consts)


N_SAMPLE_IN = 17


def _mixer_sample_kernel(*refs):
    (z_ref, xbc_ref, dt_ref, q_ref, k_ref, v_ref, sst_ref, sconv_ref, ck_ref, cv_ref,
     cw_ref, cb_ref, dtb_ref, alog_ref, dskip_ref, snorm_ref, sinks_ref) = refs[:N_SAMPLE_IN]
    (mix_ref, sst_out_ref, ck_out_ref, cv_out_ref, xe_scr, pad_scr, kk_scr, vv_scr, q_scr) = refs[-9:]
    n_seq, dl = xbc_ref.shape[0], xbc_ref.shape[1]
    hw = D_SSM // N_SSM_GROUPS
    pad_scr[...] = jnp.zeros_like(pad_scr)
    kk_scr[...] = jnp.zeros_like(kk_scr)
    vv_scr[...] = jnp.zeros_like(vv_scr)
    q_scr[...] = jnp.zeros_like(q_scr)
    cw = cw_ref[...]
    a_rep = -jnp.exp(alog_ref[...])
    rows = lax.broadcasted_iota(jnp.int32, (dl, D_SSM), 0)

    def per_sequence(i, carry):
        x = xbc_ref[i]
        xe_scr[SUBLANES - (CONV_W - 1):SUBLANES, :] = sconv_ref[i]
        xe_scr[SUBLANES:SUBLANES + dl, :] = x
        conv = cb_ref[...] + cw[CONV_W - 1:CONV_W] * x
        for back in range(1, CONV_W):
            conv = conv + cw[CONV_W - 1 - back:CONV_W - back] * xe_scr[SUBLANES - back:SUBLANES - back + dl, :]
        xc = _silu(conv)
        xs = xc[:, 0:D_SSM]
        bm = xc[:, D_SSM:D_SSM + N_SSM_GROUPS * D_STATE]
        cm = xc[:, D_SSM + N_SSM_GROUPS * D_STATE:CONV_DIM]

        dt = _softplus(dt_ref[i] + dtb_ref[...])
        dta = dt * a_rep
        xdt = xs * dt
        bcast = lambda arr, s: jnp.broadcast_to(arr[s:s + 1, :], (dl, arr.shape[1]))
        cum = jnp.zeros_like(dta)
        for s in range(dl):
            cum = cum + jnp.where(rows >= s, bcast(dta, s), 0.0)
        cum_last = bcast(cum, dl - 1)

        y = dskip_ref[...] * xs
        for s in range(dl):
            prod = cm * bcast(bm, s)
            dots = [jnp.broadcast_to(jnp.sum(prod[:, g * D_STATE:(g + 1) * D_STATE], axis=-1, keepdims=True), (dl, hw))
                    for g in range(N_SSM_GROUPS)]
            decay = jnp.exp(jnp.where(rows >= s, cum - bcast(cum, s), -jnp.inf))
            y = y + jnp.concatenate(dots, axis=1) * decay * bcast(xdt, s)

        state = sst_ref[i]
        state_b = state.astype(BF16)
        pad_scr[0:dl, 0:D_SSM] = xdt * jnp.exp(cum_last - cum)
        pad_scr[0:dl, D_SSM:D_SSM + 2 * D_STATE] = bm
        pad_scr[0:dl, D_SSM + 2 * D_STATE:D_SSM + 4 * D_STATE] = cm
        xw_p = pad_scr[:, 0:D_SSM]
        bm_p = pad_scr[:, D_SSM:D_SSM + 2 * D_STATE].astype(BF16)
        cm_p = pad_scr[:, D_SSM + 2 * D_STATE:D_SSM + 4 * D_STATE].astype(BF16)
        y_off = [_mm_nt(cm_p[:, g * D_STATE:(g + 1) * D_STATE], state_b[g * hw:(g + 1) * hw, :])[0:dl]
                 for g in range(N_SSM_GROUPS)]
        y = y + jnp.concatenate(y_off, axis=1) * jnp.exp(cum)
        chunk_decay = jnp.exp(jnp.broadcast_to(cum[dl - 1:dl, :], (SUBLANES, D_SSM)))
        for j in range(D_SSM // LANES):
            g = j // (hw // LANES)
            upd = _mm(xw_p[:, j * LANES:(j + 1) * LANES].T.astype(BF16), bm_p[:, g * D_STATE:(g + 1) * D_STATE])
            for half, dec in enumerate(_split_pair(chunk_decay[:, j * LANES:(j + 1) * LANES])):
                r0 = j * LANES + half * HEAD_DIM
                sst_out_ref[i, r0:r0 + HEAD_DIM, :] = (
                    state[r0:r0 + HEAD_DIM] * jnp.broadcast_to(dec[0:1], (HEAD_DIM, D_STATE))
                    + upd[half * HEAD_DIM:(half + 1) * HEAD_DIM])
        mix_ref[i, :, 0:D_SSM] = _gated_group_norm(y, z_ref[i], snorm_ref[...])

        kk_scr[0:WINDOW, :] = ck_ref[i]
        kk_scr[WINDOW:WINDOW + dl, :] = k_ref[i]
        vv_scr[0:WINDOW, :] = cv_ref[i]
        vv_scr[WINDOW:WINDOW + dl, :] = v_ref[i]
        ck_out_ref[i] = kk_scr[dl:dl + WINDOW, :]
        cv_out_ref[i] = vv_scr[dl:dl + WINDOW, :]
        q_scr[0:dl, :] = q_ref[i]
        att = _attention_rows(q_scr[...], kk_scr[...], vv_scr[...], sinks_ref)
        mix_ref[i, :, D_SSM:D_SSM + D_ATTN] = att[0:dl]
        return carry

    lax.fori_loop(0, n_seq, per_sequence, 0)


def _mixer_sample(z, xbc, dtr, q, k, v, state_ssm, state_conv, cache_k, cache_v, pw, layer, carried):
    depth = state_ssm.shape[0]
    nb, dl = xbc.shape[0], xbc.shape[1]
    bb = min(SAMPLE_SEQS_PER_STEP, nb)
    blk = lambda *dims: pl.BlockSpec((bb,) + dims, lambda i: (i,) + (0,) * len(dims))
    lblk = lambda *dims: pl.BlockSpec((None, bb) + dims, lambda i: (layer, i) + (0,) * len(dims))
    consts = (pw['conv_w'], pw['conv_b'], pw['dt_bias'], pw['a_log'], pw['d_skip'], pw['ssm_norm'], pw['sinks'])
    operands = (z, xbc, dtr, q, k, v, state_ssm, state_conv, cache_k, cache_v) + consts
    assert len(operands) == N_SAMPLE_IN
    return pl.pallas_call(
        _mixer_sample_kernel,
        grid=(nb // bb,),
        in_specs=[blk(dl, D_SSM), blk(dl, CONV_DIM), blk(dl, D_SSM), blk(dl, D_ATTN), blk(dl, KV_WIDTH),
                  blk(dl, KV_WIDTH), lblk(D_SSM, D_STATE), lblk(CONV_W - 1, CONV_DIM), lblk(WINDOW, KV_WIDTH),
                  lblk(WINDOW, KV_WIDTH)]
                 + [_layer_block(cst, layer) for cst in consts]
                 + [pl.BlockSpec(memory_space=pl.ANY)] * len(carried),
        out_specs=[blk(dl, D_SSM + D_ATTN), lblk(D_SSM, D_STATE), lblk(WINDOW, KV_WIDTH), lblk(WINDOW, KV_WIDTH)],
        out_shape=[jax.ShapeDtypeStruct((nb, dl, D_SSM + D_ATTN), F32),
                   jax.ShapeDtypeStruct((depth, nb, D_SSM, D_STATE), F32),
                   jax.ShapeDtypeStruct((depth, nb, WINDOW, KV_WIDTH), F32),
                   jax.ShapeDtypeStruct((depth, nb, WINDOW, KV_WIDTH), F32)],
        input_output_aliases={N_SAMPLE_IN + n: 1 + n for n in range(len(carried))},
        scratch_shapes=[pltpu.VMEM((2 * SUBLANES, CONV_DIM), F32),
                        pltpu.VMEM((SSD_CHUNK, D_SSM + 4 * D_STATE), F32),
                        pltpu.VMEM((KEY_PAD, KV_WIDTH), F32),
                        pltpu.VMEM((KEY_PAD, KV_WIDTH), F32),
                        pltpu.VMEM((SUBLANES, D_ATTN), F32)],
        compiler_params=pltpu.CompilerParams(dimension_semantics=("arbitrary",),
                                             vmem_limit_bytes=VMEM_LIMIT),
        name="mixer_sample",
    )(*operands, *carried)


def _prepare_weights(p):
    pad_ff = D_FF_PAD - D_FF
    bf = lambda w: w.astype(BF16)
    vec = lambda v: v[:, None, :]
    rep = lambda v: jnp.repeat(v, HEAD_DIM, axis=-1)[:, None, :]
    cuts = np.cumsum([D_SSM, CONV_DIM, N_SSM_HEADS, D_ATTN, KV_WIDTH])
    wz, wxbc, wdt, wq, wk, wv = jnp.split(bf(p['w_in']), cuts, axis=2)
    depth = p['w_in'].shape[0]
    seg = np.kron(np.eye(D_ATTN // HEAD_DIM, dtype=np.float32), np.ones((HEAD_DIM, HEAD_DIM), np.float32))
    return {
        'g_ffn1': vec(p['g_ffn1']), 'g_mix': vec(p['g_mix']), 'g_ffn2': vec(p['g_ffn2']), 'g_ple': vec(p['g_ple']),
        'w1_a': jnp.pad(bf(p['w1_a']), ((0, 0), (0, 0), (0, pad_ff))),
        'w3_a': jnp.pad(bf(p['w3_a']), ((0, 0), (0, 0), (0, pad_ff))),
        'w2_a': jnp.pad(bf(p['w2_a']), ((0, 0), (0, pad_ff), (0, 0))),
        'w1_b': jnp.pad(bf(p['w1_b']), ((0, 0), (0, 0), (0, pad_ff))),
        'w3_b': jnp.pad(bf(p['w3_b']), ((0, 0), (0, 0), (0, pad_ff))),
        'w2_b': jnp.pad(bf(p['w2_b']), ((0, 0), (0, pad_ff), (0, 0))),
        'w_in': jnp.concatenate([wz, wxbc, jnp.repeat(wdt, HEAD_DIM, axis=2), wq, wk, wv], axis=2),
        'w_out': bf(p['w_out']), 'w_ple_gate': bf(p['w_ple_gate']), 'w_ple_proj': bf(p['w_ple_proj']),
        'q_gain': vec(jnp.tile(p['q_norm'], (1, N_HEADS))),
        'k_gain': vec(jnp.tile(p['k_norm'], (1, N_KV_HEADS))),
        'seg': jnp.broadcast_to(jnp.asarray(seg, BF16), (depth,) + seg.shape),
        'conv_w': p['conv_w'], 'conv_b': vec(p['conv_b']),
        'dt_bias': rep(p['dt_bias']), 'a_log': rep(p['a_log']), 'd_skip': rep(p['d_skip']),
        'ssm_norm': vec(p['ssm_norm']),
        'sinks': jnp.broadcast_to(p['sinks'][:, :, None], (depth, N_HEADS, LANES)),
    }


def kernel(x_prompt, x_sample, state_ssm, state_conv, cache_k_win, cache_v_win, p_prompt, p_sample, g_ffn1, w1_a, w3_a, w2_a, g_mix, w_in, conv_w, conv_b, dt_bias, a_log, d_skip, ssm_norm, q_norm, k_norm, sinks, w_out, g_ffn2, w1_b, w3_b, w2_b, g_ple, w_ple_gate, w_ple_proj):
    params = dict(g_ffn1=g_ffn1, w1_a=w1_a, w3_a=w3_a, w2_a=w2_a, g_mix=g_mix, w_in=w_in, conv_w=conv_w,
                  conv_b=conv_b, dt_bias=dt_bias, a_log=a_log, d_skip=d_skip, ssm_norm=ssm_norm, q_norm=q_norm,
                  k_norm=k_norm, sinks=sinks, w_out=w_out, g_ffn2=g_ffn2, w1_b=w1_b, w3_b=w3_b, w2_b=w2_b,
                  g_ple=g_ple, w_ple_gate=w_ple_gate, w_ple_proj=w_ple_proj)
    depth = w_in.shape[0]
    bp, seq, _ = x_prompt.shape
    bs, dl, _ = x_sample.shape
    assert seq % SSD_CHUNK == 0 and seq >= WINDOW and dl >= CONV_W - 1 and dl <= SUBLANES
    assert (bp * seq) % TOKEN_TILE == 0 and (bs * dl) % min(TOKEN_TILE, bs * dl) == 0

    pw = _prepare_weights(params)
    sst_in = state_ssm.reshape(depth, bs, D_SSM, D_STATE)
    ck_in = cache_k_win.reshape(depth, bs, WINDOW, KV_WIDTH)
    cv_in = cache_v_win.reshape(depth, bs, WINDOW, KV_WIDTH)
    hp = x_prompt.reshape(bp * seq, D_MODEL)
    hs = x_sample.reshape(bs * dl, D_MODEL)
    ssm_p, conv_p, k_p, v_p, conv_s = [], [], [], [], []
    carried = ()
    for l in range(depth):
        hp, zp, xbcp, dtp, qp, kp, vp = _stage_a(hp, pw, l)
        hs, zs, xbcs, dts, qs, ks, vs = _stage_a(hs, pw, l)
        mixp, sp = _mixer_prompt(zp, xbcp, dtp, qp, kp, vp, pw, l, bp, seq)
        seq3 = lambda a: a.reshape(bs, dl, a.shape[-1])
        mixs, *carried = _mixer_sample(seq3(zs), seq3(xbcs), seq3(dts), seq3(qs), seq3(ks), seq3(vs),
                                       sst_in, state_conv, ck_in, cv_in, pw, l, tuple(carried))
        hp = _stage_c(hp, mixp, p_prompt[l].reshape(bp * seq, D_PLE), pw, l)
        hs = _stage_c(hs, mixs.reshape(bs * dl, D_MODEL), p_sample[l].reshape(bs * dl, D_PLE), pw, l)
        ssm_p.append(sp.reshape(bp, N_SSM_HEADS, HEAD_DIM, D_STATE))
        conv_p.append(xbcp.reshape(bp, seq, CONV_DIM)[:, seq - (CONV_W - 1):])
        k_p.append(kp.reshape(bp, seq, N_KV_HEADS, HEAD_DIM)[:, seq - WINDOW:])
        v_p.append(vp.reshape(bp, seq, N_KV_HEADS, HEAD_DIM)[:, seq - WINDOW:])
        conv_s.append(xbcs.reshape(bs, dl, CONV_DIM)[:, dl - (CONV_W - 1):])
    ssm_s, k_s, v_s = carried
    return (hp.reshape(bp, seq, D_MODEL), hs.reshape(bs, dl, D_MODEL),
            jnp.stack(ssm_p), jnp.stack(conv_p), jnp.stack(k_p), jnp.stack(v_p),
            ssm_s.reshape(depth, bs, N_SSM_HEADS, HEAD_DIM, D_STATE), jnp.stack(conv_s),
            k_s.reshape(depth, bs, WINDOW, N_KV_HEADS, HEAD_DIM),
            v_s.reshape(depth, bs, WINDOW, N_KV_HEADS, HEAD_DIM))
```

```python
import numpy as np
import jax
import jax.numpy as jnp
from jax import lax
from jax.experimental import pallas as pl
from jax.experimental.pallas import tpu as pltpu

F32 = jnp.float32
BF16 = jnp.bfloat16

D_MODEL = 1024
HEAD_DIM = 64
D_SSM = 512
N_SSM_HEADS = 8
N_SSM_GROUPS = 2
D_STATE = 128
CONV_W = 4
CONV_DIM = D_SSM + 2 * N_SSM_GROUPS * D_STATE
SSD_CHUNK = 128
D_ATTN = 512
N_HEADS = 8
N_KV_HEADS = 2
Q_PER_KV = N_HEADS // N_KV_HEADS
KV_WIDTH = N_KV_HEADS * HEAD_DIM
WINDOW = 128
D_FF = 2752
D_PLE = 256
RMS_EPS = 1e-6

LANES = 128
SUBLANES = 8
MXU_DIM = 256
FF_CHUNK = MXU_DIM
D_FF_PAD = -(-D_FF // FF_CHUNK) * FF_CHUNK
TOKEN_TILE = 512
SAMPLE_SEQS_PER_STEP = 16
SAMPLE_UNROLL = 2
SEQ_ROWS = SUBLANES
KEY_PAD = 2 * WINDOW
VMEM_LIMIT = 56 * 1024 * 1024

OFF_Z = 0
OFF_XBC = OFF_Z + D_SSM
OFF_DT = OFF_XBC + CONV_DIM
OFF_Q = OFF_DT + D_SSM
OFF_K = OFF_Q + D_ATTN
OFF_V = OFF_K + KV_WIDTH
D_PROJ_PAD = OFF_V + KV_WIDTH

ALIBI_SLOPES = tuple(float(s) for s in np.power(
    np.float32(2.0), -8.0 * np.arange(1, N_HEADS + 1, dtype=np.float32) / N_HEADS))

_NT = (((1,), (1,)), ((), ()))


def _mm(a, b):
    return jnp.dot(a, b, preferred_element_type=F32)


def _mm_nt(a, b):
    return lax.dot_general(a, b, _NT, preferred_element_type=F32)


def _sigmoid(x):
    return 1.0 / (1.0 + jnp.exp(-x))


def _silu(x):
    return x * _sigmoid(x)


def _softplus(x):
    return jnp.maximum(x, 0.0) + jnp.log(1.0 + jnp.exp(-jnp.abs(x)))


def _rmsnorm(x, g):
    return x * lax.rsqrt(jnp.mean(x * x, axis=-1, keepdims=True) + RMS_EPS) * g


def _split3(x):
    hi = x.astype(BF16)
    rest = x - hi.astype(F32)
    mid = rest.astype(BF16)
    return hi, mid, (rest - mid.astype(F32)).astype(BF16)


def _swiglu(xn, w1_ref, w3_ref, w2_ref):
    acc = jnp.zeros((xn.shape[0], D_MODEL), F32)
    for c in range(D_FF_PAD // FF_CHUNK):
        sl = slice(c * FF_CHUNK, (c + 1) * FF_CHUNK)
        a = _mm(xn, w1_ref[:, sl])
        b = _mm(xn, w3_ref[:, sl])
        acc = acc + _mm((_silu(a) * b).astype(BF16), w2_ref[sl, :])
    return acc


def _segment_sumsq(x, seg_ref):
    hi, mid, _ = _split3(x * x)
    seg = seg_ref[0:x.shape[1], 0:x.shape[1]]
    return _mm(hi, seg) + _mm(mid, seg)


def _stage_a_kernel(x_ref, g1_ref, w1_ref, w3_ref, w2_ref, gm_ref, win_ref, qg_ref, kg_ref, seg_ref,
                    h_ref, z_ref, xbc_ref, dt_ref, q_ref, k_ref, v_ref):
    x = x_ref[...]
    h = x + 0.5 * _swiglu(_rmsnorm(x, g1_ref[...]).astype(BF16), w1_ref, w3_ref, w2_ref)
    h_ref[...] = h
    proj = _mm(_rmsnorm(h, gm_ref[...]).astype(BF16), win_ref[...])
    z_ref[...] = proj[:, OFF_Z:OFF_XBC]
    xbc_ref[...] = proj[:, OFF_XBC:OFF_DT]
    dt_ref[...] = proj[:, OFF_DT:OFF_Q]
    q = proj[:, OFF_Q:OFF_K]
    k = proj[:, OFF_K:OFF_V]
    q_ref[...] = q * lax.rsqrt(_segment_sumsq(q, seg_ref) * (1.0 / HEAD_DIM) + RMS_EPS) * qg_ref[...]
    k_ref[...] = k * lax.rsqrt(_segment_sumsq(k, seg_ref) * (1.0 / HEAD_DIM) + RMS_EPS) * kg_ref[...]
    v_ref[...] = proj[:, OFF_V:D_PROJ_PAD]


def _layer_block(arr, layer, **kwargs):
    return pl.BlockSpec((None,) + arr.shape[1:], lambda *_: (layer, 0, 0), **kwargs)


def _layer_resident(arr, layer):
    return _layer_block(arr, layer, pipeline_mode=pl.Buffered(1))


def _stage_a(x, pw, layer):
    n = x.shape[0]
    tm = min(TOKEN_TILE, n)
    row = lambda width: pl.BlockSpec((tm, width), lambda i: (i, 0))
    widths = (D_MODEL, D_SSM, CONV_DIM, D_SSM, D_ATTN, KV_WIDTH, KV_WIDTH)
    consts = (pw['g_ffn1'], pw['w1_a'], pw['w3_a'], pw['w2_a'], pw['g_mix'], pw['w_in'],
              pw['q_gain'], pw['k_gain'], pw['seg'])
    return pl.pallas_call(
        _stage_a_kernel,
        grid=(n // tm,),
        in_specs=[row(D_MODEL)] + [_layer_resident(c, layer) for c in consts],
        out_specs=[row(w) for w in widths],
        out_shape=[jax.ShapeDtypeStruct((n, w), F32) for w in widths],
        compiler_params=pltpu.CompilerParams(dimension_semantics=("arbitrary",),
                                             vmem_limit_bytes=VMEM_LIMIT),
        name="stage_a",
    )(x, *consts)


def _stage_c_kernel(h_ref, mix_ref, pe_ref, wout_ref, g2_ref, w1_ref, w3_ref, w2_ref, gp_ref,
                    wgate_ref, wproj_ref, o_ref):
    h = h_ref[...] + _mm(mix_ref[...].astype(BF16), wout_ref[...])
    h = h + 0.5 * _swiglu(_rmsnorm(h, g2_ref[...]).astype(BF16), w1_ref, w3_ref, w2_ref)
    gate = _sigmoid(_mm(_rmsnorm(h, gp_ref[...]).astype(BF16), wgate_ref[...]))
    o_ref[...] = h + gate * _mm(pe_ref[...].astype(BF16), wproj_ref[...])


def _stage_c(h, mix, pe, pw, layer):
    n = h.shape[0]
    tm = min(TOKEN_TILE, n)
    row = lambda width: pl.BlockSpec((tm, width), lambda i: (i, 0))
    consts = (pw['w_out'], pw['g_ffn2'], pw['w1_b'], pw['w3_b'], pw['w2_b'], pw['g_ple'],
              pw['w_ple_gate'], pw['w_ple_proj'])
    return pl.pallas_call(
        _stage_c_kernel,
        grid=(n // tm,),
        in_specs=[row(D_MODEL), row(D_MODEL), pl.BlockSpec((None, tm, D_PLE), lambda i: (layer, i, 0))]
                 + [_layer_resident(c, layer) for c in consts],
        out_specs=row(D_MODEL),
        out_shape=jax.ShapeDtypeStruct((n, D_MODEL), F32),
        compiler_params=pltpu.CompilerParams(dimension_semantics=("arbitrary",),
                                             vmem_limit_bytes=VMEM_LIMIT),
        name="stage_c",
    )(h, mix, pe, *consts)


def _lane_low_half(shape):
    return lax.broadcasted_iota(jnp.int32, shape, len(shape) - 1) < HEAD_DIM


def _split_pair(x):
    rolled = pltpu.roll(x, HEAD_DIM, 1)
    low = _lane_low_half(x.shape)
    return jnp.where(low, x, rolled), jnp.where(low, rolled, x)


def _head_query_rows(q, g):
    low = _lane_low_half((q.shape[0], LANES))
    blocks = []
    for jj in range(Q_PER_KV // 2):
        qp = q[:, (g * 2 + jj) * LANES:(g * 2 + jj + 1) * LANES]
        blocks += [jnp.where(low, qp, 0.0), jnp.where(low, 0.0, qp)]
    return jnp.concatenate(blocks, axis=0).astype(BF16)


def _attention_rows(q, kk, vv, sinks_ref):
    r = q.shape[0]
    low = _lane_low_half((r, LANES))
    low_k = _lane_low_half((KEY_PAD, LANES))
    t_i = lax.broadcasted_iota(jnp.int32, (r, KEY_PAD), 0)
    s_i = lax.broadcasted_iota(jnp.int32, (r, KEY_PAD), 1)
    rel = t_i + WINDOW - s_i
    valid = (rel >= 0) & (rel < WINDOW)
    relf = rel.astype(F32)
    k_dup = _split_pair(kk)
    v_roll = pltpu.roll(vv, HEAD_DIM, 1)
    v_even = (jnp.where(low_k, vv, 1.0), jnp.where(low_k, v_roll, 1.0))
    v_odd = (jnp.where(low_k, 1.0, v_roll), jnp.where(low_k, 1.0, vv))
    outs = []
    for g in range(N_KV_HEADS):
        s = _mm_nt(_head_query_rows(q, g), k_dup[g].astype(BF16))
        probs, sink_terms = [], []
        for hh in range(Q_PER_KV):
            head = g * Q_PER_KV + hh
            sh = jnp.where(valid, s[hh * r:(hh + 1) * r] - ALIBI_SLOPES[head] * relf, -jnp.inf)
            sink = sinks_ref[head:head + 1, 0:1]
            m = jnp.maximum(jnp.max(sh, axis=-1, keepdims=True), sink)
            probs.append(jnp.exp(sh - m))
            sink_terms.append(jnp.broadcast_to(jnp.exp(sink - m), (r, LANES)))
        o_even = _mm(jnp.concatenate(probs[0::2], axis=0).astype(BF16), v_even[g].astype(BF16))
        o_odd = _mm(jnp.concatenate(probs[1::2], axis=0).astype(BF16), v_odd[g].astype(BF16))
        for jj in range(Q_PER_KV // 2):
            oe, oo = o_even[jj * r:(jj + 1) * r], o_odd[jj * r:(jj + 1) * r]
            den = (pltpu.roll(jnp.where(low, oo, oe), HEAD_DIM, 1)
                   + jnp.where(low, sink_terms[2 * jj], sink_terms[2 * jj + 1]))
            outs.append(jnp.where(low, oe, oo) / den)
    return jnp.concatenate(outs, axis=1)


def _fill_attention_bias(bias_scr):
    s_i = lax.broadcasted_iota(jnp.int32, (KEY_PAD, WINDOW), 0)
    t_i = lax.broadcasted_iota(jnp.int32, (KEY_PAD, WINDOW), 1)
    rel = t_i + WINDOW - s_i
    valid = (rel >= 0) & (rel < WINDOW)
    relf = rel.astype(F32)
    for head in range(N_HEADS):
        bias_scr[head * KEY_PAD:(head + 1) * KEY_PAD, :] = jnp.where(valid, -ALIBI_SLOPES[head] * relf, -jnp.inf)


def _attention_cols(q, kk, vv, bias_scr, sinks_ref, first_block):
    r = q.shape[0]
    k_dup = _split_pair(kk)
    vt = vv.T
    ones = jnp.ones((HEAD_DIM, KEY_PAD), F32)
    mask_prev = jnp.where(first_block, -jnp.inf, 0.0)
    qs = q * (HEAD_DIM ** -0.5)
    outs = []
    for g in range(N_KV_HEADS):
        st = _mm_nt(k_dup[g].astype(BF16), _head_query_rows(qs, g))
        probs, sink_terms = [], []
        for hh in range(Q_PER_KV):
            head = g * Q_PER_KV + hh
            sh = st[:, hh * r:(hh + 1) * r] + bias_scr[head * KEY_PAD:(head + 1) * KEY_PAD, :]
            prev = sh[0:WINDOW] + mask_prev
            cur = sh[WINDOW:KEY_PAD]
            sink = sinks_ref[head:head + 1, :]
            m = jnp.maximum(jnp.maximum(jnp.max(prev, axis=0, keepdims=True),
                                        jnp.max(cur, axis=0, keepdims=True)), sink)
            probs.append(jnp.concatenate([jnp.exp(prev - m), jnp.exp(cur - m)], axis=0))
            sink_terms.append(jnp.exp(sink - m))
        pr = jnp.concatenate(probs, axis=1).astype(BF16)
        v_ext = jnp.concatenate([vt[g * HEAD_DIM:(g + 1) * HEAD_DIM], ones], axis=0).astype(BF16)
        ot = _mm(v_ext, pr)
        den = ot[HEAD_DIM:HEAD_DIM + 1, :] + jnp.concatenate(sink_terms, axis=1)
        on = ot[0:HEAD_DIM, :] * (1.0 / den)
        for jj in range(Q_PER_KV // 2):
            pair = jnp.concatenate([on[:, (2 * jj) * r:(2 * jj + 1) * r],
                                    on[:, (2 * jj + 1) * r:(2 * jj + 2) * r]], axis=0)
            outs.append(pair.T)
    return jnp.concatenate(outs, axis=1)


def _gated_group_norm(y, z, gain):
    y = y * _silu(z)
    gw = D_SSM // N_SSM_GROUPS
    parts = [_rmsnorm(y[:, g * gw:(g + 1) * gw], gain[:, g * gw:(g + 1) * gw]) for g in range(N_SSM_GROUPS)]
    return jnp.concatenate(parts, axis=1)


def _mixer_prompt_kernel(z_ref, xbc_ref, xprev_ref, dt_ref, q_ref, kc_ref, kp_ref, vc_ref, vp_ref,
                         cw_ref, cb_ref, dtb_ref, alog_ref, dskip_ref, snorm_ref, sinks_ref,
                         mix_ref, state_ref, st_scr, xe_scr, bias_scr):
    t = SSD_CHUNK
    c = pl.program_id(1)
    first = c == 0

    @pl.when(first & (pl.program_id(0) == 0))
    def _():
        _fill_attention_bias(bias_scr)

    @pl.when(first)
    def _():
        st_scr[...] = jnp.zeros_like(st_scr)

    x = xbc_ref[...]
    xe_scr[0:SUBLANES, :] = jnp.where(first, 0.0, xprev_ref[...])
    xe_scr[SUBLANES:SUBLANES + t, :] = x
    cw = cw_ref[...]
    conv = cb_ref[...] + cw[CONV_W - 1:CONV_W] * x
    for back in range(1, CONV_W):
        conv = conv + cw[CONV_W - 1 - back:CONV_W - back] * xe_scr[SUBLANES - back:SUBLANES - back + t, :]
    xc = _silu(conv)
    xs = xc[:, 0:D_SSM]
    bm = xc[:, D_SSM:D_SSM + N_SSM_GROUPS * D_STATE]
    cm = xc[:, D_SSM + N_SSM_GROUPS * D_STATE:CONV_DIM]

    dt = _softplus(dt_ref[...] + dtb_ref[...])
    dta = dt * (-jnp.exp(alog_ref[...]))
    row = lax.broadcasted_iota(jnp.int32, (t, t), 0)
    col = lax.broadcasted_iota(jnp.int32, (t, t), 1)
    causal = row >= col
    cum3 = _mm(causal.astype(BF16), jnp.concatenate(_split3(dta), axis=1))
    cum = cum3[:, 0:D_SSM] + cum3[:, D_SSM:2 * D_SSM] + cum3[:, 2 * D_SSM:3 * D_SSM]
    cum_last = cum[t - 1:t, :]
    xdt = xs * dt
    xdt_b = xdt.astype(BF16)
    xw_b = (xdt * jnp.exp(cum_last - cum)).astype(BF16)
    st = st_scr[...]
    st_b = st.astype(BF16)
    low_b = _lane_low_half((t, LANES))
    hw = D_SSM // N_SSM_GROUPS
    y_in, y_off, st_new = [], [], []
    for g in range(N_SSM_GROUPS):
        bg = bm[:, g * D_STATE:(g + 1) * D_STATE]
        cg_b = cm[:, g * D_STATE:(g + 1) * D_STATE].astype(BF16)
        cb_t = _mm_nt(cg_b, bg.astype(BF16))
        y_off.append(_mm(cg_b, st_b[:, g * hw:(g + 1) * hw]))
        st_new.append(_mm(bg.T.astype(BF16), xw_b[:, g * hw:(g + 1) * hw]))
        for jj in range(hw // LANES):
            j = g * (hw // LANES) + jj
            scores = []
            for ch in _split_pair(cum[:, j * LANES:(j + 1) * LANES]):
                decay = jnp.exp(jnp.where(causal, ch - ch.T, -jnp.inf))
                scores.append((cb_t * decay).astype(BF16))
            xp = xdt_b[:, j * LANES:(j + 1) * LANES]
            zero = jnp.zeros_like(xp)
            rhs = jnp.concatenate([jnp.where(low_b, xp, zero), jnp.where(low_b, zero, xp)], axis=0)
            y_in.append(_mm(jnp.concatenate(scores, axis=1), rhs))
    y = (jnp.concatenate(y_in, axis=1) + jnp.concatenate(y_off, axis=1) * jnp.exp(cum)
         + dskip_ref[...] * xs)
    st_next = st * jnp.exp(cum_last) + jnp.concatenate(st_new, axis=1)
    st_scr[...] = st_next
    mix_ref[:, 0:D_SSM] = _gated_group_norm(y, z_ref[...], snorm_ref[...]).astype(mix_ref.dtype)

    @pl.when(c == pl.num_programs(1) - 1)
    def _():
        for j in range(D_SSM // LANES):
            state_ref[j * LANES:(j + 1) * LANES, :] = st_next[:, j * LANES:(j + 1) * LANES].T

    kk = jnp.concatenate([kp_ref[...], kc_ref[...]], axis=0)
    vv = jnp.concatenate([vp_ref[...], vc_ref[...]], axis=0)
    mix_ref[:, D_SSM:D_SSM + D_ATTN] = _attention_cols(
        q_ref[...], kk, vv, bias_scr, sinks_ref, first).astype(mix_ref.dtype)


def _mixer_prompt(z, xbc, dtr, q, k, v, pw, layer, batch, seq):
    t = SSD_CHUNK
    nc = seq // t
    cur = lambda width: pl.BlockSpec((t, width), lambda b, c: (b * nc + c, 0))
    prev = lambda width: pl.BlockSpec((t, width), lambda b, c: (jnp.maximum(b * nc + c - 1, 0), 0))
    tail = pl.BlockSpec((SUBLANES, CONV_DIM),
                        lambda b, c: (jnp.maximum((b * nc + c) * (t // SUBLANES) - 1, 0), 0))
    consts = (pw['conv_w'], pw['conv_b'], pw['dt_bias'], pw['a_log'], pw['d_skip'], pw['ssm_norm'], pw['sinks'])
    return pl.pallas_call(
        _mixer_prompt_kernel,
        grid=(batch, nc),
        in_specs=[cur(D_SSM), cur(CONV_DIM), tail, cur(D_SSM), cur(D_ATTN),
                  cur(KV_WIDTH), prev(KV_WIDTH), cur(KV_WIDTH), prev(KV_WIDTH)]
                 + [_layer_block(cst, layer) for cst in consts],
        out_specs=[pl.BlockSpec((t, D_SSM + D_ATTN), lambda b, c: (b * nc + c, 0)),
                   pl.BlockSpec((None, D_SSM, D_STATE), lambda b, c: (b, 0, 0))],
        out_shape=[jax.ShapeDtypeStruct((batch * seq, D_SSM + D_ATTN), BF16),
                   jax.ShapeDtypeStruct((batch, D_SSM, D_STATE), F32)],
        scratch_shapes=[pltpu.VMEM((D_STATE, D_SSM), F32),
                        pltpu.VMEM((SUBLANES + t, CONV_DIM), F32),
                        pltpu.VMEM((N_HEADS * KEY_PAD, WINDOW), F32)],
        compiler_params=pltpu.CompilerParams(dimension_semantics=("arbitrary", "arbitrary"),
                                             vmem_limit_bytes=VMEM_LIMIT),
        name="mixer_prompt",
    )(z, xbc, xbc, dtr, q, k, k, v, v, *consts)


N_SAMPLE_IN = 17


def _put_lane_blocks(scr, col0, rows, val):
    for j in range(val.shape[1] // LANES):
        scr[col0 // LANES + j, rows, :] = val[:, j * LANES:(j + 1) * LANES]


def _get_lane_blocks(scr, col0, width, rows):
    return jnp.concatenate([scr[col0 // LANES + j, rows, :] for j in range(width // LANES)], axis=1)


def _mixer_sample_kernel(*refs):
    (z_ref, xbc_ref, dt_ref, q_ref, k_ref, v_ref, sst_ref, sconv_ref, ck_ref, cv_ref,
     cw_ref, cb_ref, dtb_ref, alog_ref, dskip_ref, snorm_ref, sinks_ref) = refs[:N_SAMPLE_IN]
    (mix_ref, sst_out_ref, ck_out_ref, cv_out_ref,
     ypart_scr, ecum_scr, cdec_scr, xw_scr, xt_scr, bmat_scr, seq_scr, res_scr, kk_scr, vv_scr) = refs[-14:]
    dl, bb = xbc_ref.shape[0], xbc_ref.shape[1]
    hw = D_SSM // N_SSM_GROUPS
    nbc = N_SSM_GROUPS * D_STATE
    c_q, c_c, c_k, c_v = 0, D_ATTN, D_ATTN + nbc, D_ATTN + nbc + KV_WIDTH
    xw_scr[...] = jnp.zeros_like(xw_scr)
    bmat_scr[...] = jnp.zeros_like(bmat_scr)
    seq_scr[...] = jnp.zeros_like(seq_scr)
    kk_scr[...] = jnp.zeros_like(kk_scr)
    vv_scr[...] = jnp.zeros_like(vv_scr)

    cw = cw_ref[...]
    a_rep = -jnp.exp(alog_ref[...])
    xs, bm, cm, xdt, cum = [], [], [], [], []
    for t in range(dl):
        conv = cb_ref[...] + cw[CONV_W - 1:CONV_W] * xbc_ref[t]
        for back in range(1, CONV_W):
            src = xbc_ref[t - back] if t >= back else sconv_ref[CONV_W - 1 + t - back]
            conv = conv + cw[CONV_W - 1 - back:CONV_W - back] * src
        xc = _silu(conv)
        xs.append(xc[:, 0:D_SSM])
        bm.append(xc[:, D_SSM:D_SSM + nbc])
        cm.append(xc[:, D_SSM + nbc:CONV_DIM])
        dt = _softplus(dt_ref[t] + dtb_ref[...])
        cum.append(dt * a_rep if t == 0 else cum[-1] + dt * a_rep)
        xdt.append(xs[t] * dt)
    cdec_scr[...] = jnp.exp(cum[dl - 1])
    for t in range(dl):
        y = dskip_ref[...] * xs[t]
        for s in range(t + 1):
            prod = cm[t] * bm[s]
            dots = jnp.concatenate(
                [jnp.broadcast_to(jnp.sum(prod[:, g * D_STATE:(g + 1) * D_STATE], axis=-1, keepdims=True), (bb, hw))
                 for g in range(N_SSM_GROUPS)], axis=1)
            y = y + (dots * xdt[s] if s == t else dots * jnp.exp(cum[t] - cum[s]) * xdt[s])
        rows_t = slice(t * bb, (t + 1) * bb)
        ypart_scr[rows_t, :] = y
        ecum_scr[rows_t, :] = jnp.exp(cum[t])
        xw_scr[rows_t, :] = xdt[t] * jnp.exp(cum[dl - 1] - cum[t])
        bmat_scr[rows_t, :] = bm[t]
        own_rows = pl.ds(t, bb, stride=SEQ_ROWS)
        _put_lane_blocks(seq_scr, c_q, own_rows, q_ref[t] * (HEAD_DIM ** -0.5))
        _put_lane_blocks(seq_scr, c_c, own_rows, cm[t])
        _put_lane_blocks(seq_scr, c_k, own_rows, k_ref[t])
        _put_lane_blocks(seq_scr, c_v, own_rows, v_ref[t])
    xw = xw_scr[...]
    for j in range(D_SSM // LANES):
        xt_scr[j * LANES:(j + 1) * LANES, :] = xw[:, j * LANES:(j + 1) * LANES].T.astype(BF16)

    block_row = lax.broadcasted_iota(jnp.int32, (SSD_CHUNK, D_STATE), 0)

    def one_sequence(b, slot):
        r0 = pl.multiple_of(b * SEQ_ROWS, SEQ_ROWS)
        my_rows = pl.ds(r0, SEQ_ROWS)
        state = sst_ref[b]
        state_b = state.astype(BF16)
        c_rows = jnp.concatenate([_get_lane_blocks(seq_scr, c_c, nbc, my_rows),
                                  jnp.zeros((SEQ_ROWS, nbc), F32)], axis=0).astype(BF16)
        y_off = [_mm_nt(c_rows[:, g * D_STATE:(g + 1) * D_STATE], state_b[g * hw:(g + 1) * hw, :])[0:SEQ_ROWS]
                 for g in range(N_SSM_GROUPS)]
        _put_lane_blocks(res_scr, 0, my_rows, jnp.concatenate(y_off, axis=1))
        is_mine = (block_row & (bb - 1)) == b
        decay = jnp.broadcast_to(cdec_scr[pl.ds(b, 1), :], (SUBLANES, D_SSM))
        for g in range(N_SSM_GROUPS):
            rhs = jnp.where(is_mine, bmat_scr[:, g * D_STATE:(g + 1) * D_STATE], 0.0).astype(BF16)
            upd = _mm(xt_scr[g * hw:(g + 1) * hw, :], rhs)
            for jj in range(hw // LANES):
                j = g * (hw // LANES) + jj
                for half, dec in enumerate(_split_pair(decay[:, j * LANES:(j + 1) * LANES])):
                    h0 = j * LANES + half * HEAD_DIM
                    u0 = jj * LANES + half * HEAD_DIM
                    sst_out_ref[b, h0:h0 + HEAD_DIM, :] = (
                        state[h0:h0 + HEAD_DIM] * jnp.broadcast_to(dec[0:1], (HEAD_DIM, D_STATE))
                        + upd[u0:u0 + HEAD_DIM])
        kk_scr[slot, 0:WINDOW, :] = ck_ref[b]
        kk_scr[slot, WINDOW:WINDOW + SEQ_ROWS, :] = _get_lane_blocks(seq_scr, c_k, KV_WIDTH, my_rows)
        vv_scr[slot, 0:WINDOW, :] = cv_ref[b]
        vv_scr[slot, WINDOW:WINDOW + SEQ_ROWS, :] = _get_lane_blocks(seq_scr, c_v, KV_WIDTH, my_rows)
        ck_out_ref[b] = kk_scr[slot, dl:dl + WINDOW, :]
        cv_out_ref[b] = vv_scr[slot, dl:dl + WINDOW, :]
        att = _attention_rows(_get_lane_blocks(seq_scr, c_q, D_ATTN, my_rows), kk_scr[slot], vv_scr[slot], sinks_ref)
        _put_lane_blocks(res_scr, D_SSM, my_rows, att)

    def per_group(i, carry):
        for slot in range(SAMPLE_UNROLL):
            one_sequence(i * SAMPLE_UNROLL + slot, slot)
        return carry

    lax.fori_loop(0, bb // SAMPLE_UNROLL, per_group, 0)

    for t in range(dl):
        own_rows = pl.ds(t, bb, stride=SEQ_ROWS)
        rows_t = slice(t * bb, (t + 1) * bb)
        y = ypart_scr[rows_t, :] + _get_lane_blocks(res_scr, 0, D_SSM, own_rows) * ecum_scr[rows_t, :]
        mix_ref[t, :, 0:D_SSM] = _gated_group_norm(y, z_ref[t], snorm_ref[...])
        mix_ref[t, :, D_SSM:D_SSM + D_ATTN] = _get_lane_blocks(res_scr, D_SSM, D_ATTN, own_rows)


def _mixer_sample(z, xbc, dtr, q, k, v, state_ssm, state_conv, cache_k, cache_v, pw, layer, carried):
    depth = state_ssm.shape[0]
    dl, nb = xbc.shape[0], xbc.shape[1]
    bb = min(SAMPLE_SEQS_PER_STEP, nb)
    assert bb & (bb - 1) == 0 and dl * bb <= SSD_CHUNK and dl <= SEQ_ROWS and bb % SAMPLE_UNROLL == 0
    tok = lambda width: pl.BlockSpec((dl, bb, width), lambda i: (0, i, 0))
    lblk = lambda *dims: pl.BlockSpec((None, bb) + dims, lambda i: (layer, i) + (0,) * len(dims))
    consts = (pw['conv_w'], pw['conv_b'], pw['dt_bias'], pw['a_log'], pw['d_skip'], pw['ssm_norm'], pw['sinks'])
    operands = (z, xbc, dtr, q, k, v, state_ssm, state_conv, cache_k, cache_v) + consts
    assert len(operands) == N_SAMPLE_IN
    seq_cols = D_ATTN + N_SSM_GROUPS * D_STATE + 2 * KV_WIDTH
    return pl.pallas_call(
        _mixer_sample_kernel,
        grid=(nb // bb,),
        in_specs=[tok(D_SSM), tok(CONV_DIM), tok(D_SSM), tok(D_ATTN), tok(KV_WIDTH), tok(KV_WIDTH),
                  lblk(D_SSM, D_STATE),
                  pl.BlockSpec((None, CONV_W - 1, bb, CONV_DIM), lambda i: (layer, 0, i, 0)),
                  lblk(WINDOW, KV_WIDTH), lblk(WINDOW, KV_WIDTH)]
                 + [_layer_block(cst, layer) for cst in consts]
                 + [pl.BlockSpec(memory_space=pl.ANY)] * len(carried),
        out_specs=[tok(D_SSM + D_ATTN), lblk(D_SSM, D_STATE), lblk(WINDOW, KV_WIDTH), lblk(WINDOW, KV_WIDTH)],
        out_shape=[jax.ShapeDtypeStruct((dl, nb, D_SSM + D_ATTN), F32),
                   jax.ShapeDtypeStruct((depth, nb, D_SSM, D_STATE), F32),
                   jax.ShapeDtypeStruct((depth, nb, WINDOW, KV_WIDTH), F32),
                   jax.ShapeDtypeStruct((depth, nb, WINDOW, KV_WIDTH), F32)],
        input_output_aliases={N_SAMPLE_IN + n: 1 + n for n in range(len(carried))},
        scratch_shapes=[pltpu.VMEM((dl * bb, D_SSM), F32),
                        pltpu.VMEM((dl * bb, D_SSM), F32),
                        pltpu.VMEM((bb, D_SSM), F32),
                        pltpu.VMEM((SSD_CHUNK, D_SSM), F32),
                        pltpu.VMEM((D_SSM, SSD_CHUNK), BF16),
                        pltpu.VMEM((SSD_CHUNK, N_SSM_GROUPS * D_STATE), F32),
                        pltpu.VMEM((seq_cols // LANES, bb * SEQ_ROWS, LANES), F32),
                        pltpu.VMEM(((D_SSM + D_ATTN) // LANES, bb * SEQ_ROWS, LANES), F32),
                        pltpu.VMEM((SAMPLE_UNROLL, KEY_PAD, KV_WIDTH), F32),
                        pltpu.VMEM((SAMPLE_UNROLL, KEY_PAD, KV_WIDTH), F32)],
        compiler_params=pltpu.CompilerParams(dimension_semantics=("arbitrary",),
                                             vmem_limit_bytes=VMEM_LIMIT),
        name="mixer_sample",
    )(*operands, *carried)


def _prepare_weights(p):
    bf = lambda w: w.astype(BF16)
    vec = lambda v: v[:, None, :]

    def pad_ff(w, axis):
        shape = list(w.shape)
        shape[axis] = D_FF_PAD - D_FF
        return jnp.concatenate([bf(w), jnp.zeros(shape, BF16)], axis=axis)

    rep = lambda v: jnp.repeat(v, HEAD_DIM, axis=-1)[:, None, :]
    cuts = np.cumsum([D_SSM, CONV_DIM, N_SSM_HEADS, D_ATTN, KV_WIDTH])
    wz, wxbc, wdt, wq, wk, wv = jnp.split(bf(p['w_in']), cuts, axis=2)
    depth = p['w_in'].shape[0]
    seg = np.kron(np.eye(D_ATTN // HEAD_DIM, dtype=np.float32), np.ones((HEAD_DIM, HEAD_DIM), np.float32))
    return {
        'g_ffn1': vec(p['g_ffn1']), 'g_mix': vec(p['g_mix']), 'g_ffn2': vec(p['g_ffn2']), 'g_ple': vec(p['g_ple']),
        'w1_a': pad_ff(p['w1_a'], 2), 'w3_a': pad_ff(p['w3_a'], 2), 'w2_a': pad_ff(p['w2_a'], 1),
        'w1_b': pad_ff(p['w1_b'], 2), 'w3_b': pad_ff(p['w3_b'], 2), 'w2_b': pad_ff(p['w2_b'], 1),
        'w_in': jnp.concatenate([wz, wxbc, jnp.repeat(wdt, HEAD_DIM, axis=2), wq, wk, wv], axis=2),
        'w_out': bf(p['w_out']), 'w_ple_gate': bf(p['w_ple_gate']), 'w_ple_proj': bf(p['w_ple_proj']),
        'q_gain': vec(jnp.tile(p['q_norm'], (1, N_HEADS))),
        'k_gain': vec(jnp.tile(p['k_norm'], (1, N_KV_HEADS))),
        'seg': jnp.broadcast_to(jnp.asarray(seg, BF16), (depth,) + seg.shape),
        'conv_w': p['conv_w'], 'conv_b': vec(p['conv_b']),
        'dt_bias': rep(p['dt_bias']), 'a_log': rep(p['a_log']), 'd_skip': rep(p['d_skip']),
        'ssm_norm': vec(p['ssm_norm']),
        'sinks': jnp.broadcast_to(p['sinks'][:, :, None], (depth, N_HEADS, LANES)),
    }


def kernel(x_prompt, x_sample, state_ssm, state_conv, cache_k_win, cache_v_win, p_prompt, p_sample, g_ffn1, w1_a, w3_a, w2_a, g_mix, w_in, conv_w, conv_b, dt_bias, a_log, d_skip, ssm_norm, q_norm, k_norm, sinks, w_out, g_ffn2, w1_b, w3_b, w2_b, g_ple, w_ple_gate, w_ple_proj):
    params = dict(g_ffn1=g_ffn1, w1_a=w1_a, w3_a=w3_a, w2_a=w2_a, g_mix=g_mix, w_in=w_in, conv_w=conv_w,
                  conv_b=conv_b, dt_bias=dt_bias, a_log=a_log, d_skip=d_skip, ssm_norm=ssm_norm, q_norm=q_norm,
                  k_norm=k_norm, sinks=sinks, w_out=w_out, g_ffn2=g_ffn2, w1_b=w1_b, w3_b=w3_b, w2_b=w2_b,
                  g_ple=g_ple, w_ple_gate=w_ple_gate, w_ple_proj=w_ple_proj)
    depth = w_in.shape[0]
    bp, seq, _ = x_prompt.shape
    bs, dl, _ = x_sample.shape
    assert seq % SSD_CHUNK == 0 and seq >= WINDOW and dl >= CONV_W - 1 and dl <= SUBLANES
    assert (bp * seq) % TOKEN_TILE == 0 and (bs * dl) % min(TOKEN_TILE, bs * dl) == 0

    pw = _prepare_weights(params)
    sst_in = state_ssm.reshape(depth, bs, D_SSM, D_STATE)
    ck_in = cache_k_win.reshape(depth, bs, WINDOW, KV_WIDTH)
    cv_in = cache_v_win.reshape(depth, bs, WINDOW, KV_WIDTH)
    sconv_in = jnp.swapaxes(state_conv, 1, 2)
    pe_p = p_prompt.reshape(depth, bp * seq, D_PLE)
    pe_s = jnp.swapaxes(p_sample, 1, 2).reshape(depth, dl * bs, D_PLE)
    hp = x_prompt.reshape(bp * seq, D_MODEL)
    hs = jnp.swapaxes(x_sample, 0, 1).reshape(dl * bs, D_MODEL)
    ssm_p, conv_p, k_p, v_p, conv_s = [], [], [], [], []
    carried = ()
    for l in range(depth):
        hp, zp, xbcp, dtp, qp, kp, vp = _stage_a(hp, pw, l)
        hs, zs, xbcs, dts, qs, ks, vs = _stage_a(hs, pw, l)
        mixp, sp = _mixer_prompt(zp, xbcp, dtp, qp, kp, vp, pw, l, bp, seq)
        tm3 = lambda a: a.reshape(dl, bs, a.shape[-1])
        mixs, *carried = _mixer_sample(tm3(zs), tm3(xbcs), tm3(dts), tm3(qs), tm3(ks), tm3(vs),
                                       sst_in, sconv_in, ck_in, cv_in, pw, l, tuple(carried))
        hp = _stage_c(hp, mixp, pe_p, pw, l)
        hs = _stage_c(hs, mixs.reshape(dl * bs, D_MODEL), pe_s, pw, l)
        ssm_p.append(sp.reshape(bp, N_SSM_HEADS, HEAD_DIM, D_STATE))
        conv_p.append(xbcp.reshape(bp, seq, CONV_DIM)[:, seq - (CONV_W - 1):])
        k_p.append(kp.reshape(bp, seq, N_KV_HEADS, HEAD_DIM)[:, seq - WINDOW:])
        v_p.append(vp.reshape(bp, seq, N_KV_HEADS, HEAD_DIM)[:, seq - WINDOW:])
        conv_s.append(jnp.swapaxes(tm3(xbcs)[dl - (CONV_W - 1):], 0, 1))
    ssm_s, k_s, v_s = carried
    return (hp.reshape(bp, seq, D_MODEL), jnp.swapaxes(hs.reshape(dl, bs, D_MODEL), 0, 1),
            jnp.stack(ssm_p), jnp.stack(conv_p), jnp.stack(k_p), jnp.stack(v_p),
            ssm_s.reshape(depth, bs, N_SSM_HEADS, HEAD_DIM, D_STATE), jnp.stack(conv_s),
            k_s.reshape(depth, bs, WINDOW, N_KV_HEADS, HEAD_DIM),
            v_s.reshape(depth, bs, WINDOW, N_KV_HEADS, HEAD_DIM))
```

```python
import functools

import numpy as np
import jax
import jax.numpy as jnp
from jax import lax
from jax.experimental import pallas as pl
from jax.experimental.pallas import tpu as pltpu

F32 = jnp.float32
BF16 = jnp.bfloat16

D_MODEL = 1024
HEAD_DIM = 64
D_SSM = 512
N_SSM_HEADS = 8
N_SSM_GROUPS = 2
D_STATE = 128
CONV_W = 4
CONV_DIM = D_SSM + 2 * N_SSM_GROUPS * D_STATE
SSD_CHUNK = 128
D_ATTN = 512
N_HEADS = 8
N_KV_HEADS = 2
Q_PER_KV = N_HEADS // N_KV_HEADS
KV_WIDTH = N_KV_HEADS * HEAD_DIM
WINDOW = 128
D_FF = 2752
D_PLE = 256
RMS_EPS = 1e-6

LANES = 128
SUBLANES = 8
MXU_DIM = 256
FF_CHUNK = MXU_DIM
FF_FULL_CHUNKS = D_FF // FF_CHUNK
FF_TAIL = D_FF - FF_FULL_CHUNKS * FF_CHUNK
TOKEN_TILE = 512
MIXER_TILE = 256
SAMPLE_SEQS_PER_STEP = 16
SAMPLE_UNROLL = 2
SEQ_ROWS = SUBLANES
KEY_PAD = 2 * WINDOW
VMEM_LIMIT = 56 * 1024 * 1024

OFF_Z = 0
OFF_XBC = OFF_Z + D_SSM
OFF_DT = OFF_XBC + CONV_DIM
OFF_Q = OFF_DT + D_SSM
OFF_K = OFF_Q + D_ATTN
OFF_V = OFF_K + KV_WIDTH
D_PROJ_PAD = OFF_V + KV_WIDTH

ALIBI_SLOPES = tuple(float(s) for s in np.power(
    np.float32(2.0), -8.0 * np.arange(1, N_HEADS + 1, dtype=np.float32) / N_HEADS))

_NT = (((1,), (1,)), ((), ()))


def _mm(a, b):
    return jnp.dot(a, b, preferred_element_type=F32)


def _mm_nt(a, b):
    return lax.dot_general(a, b, _NT, preferred_element_type=F32)


def _sigmoid(x):
    return 1.0 / (1.0 + jnp.exp(-x))


def _silu(x):
    return x * _sigmoid(x)


def _softplus(x):
    return jnp.maximum(x, 0.0) + jnp.log(1.0 + jnp.exp(-jnp.abs(x)))


def _rmsnorm(x, g):
    return x * lax.rsqrt(jnp.mean(x * x, axis=-1, keepdims=True) + RMS_EPS) * g


def _split3(x):
    hi = x.astype(BF16)
    rest = x - hi.astype(F32)
    mid = rest.astype(BF16)
    return hi, mid, (rest - mid.astype(F32)).astype(BF16)


def _stage_ff_tail(w1_ref, w3_ref, w2_ref, tail_refs):
    w1t_ref, w3t_ref, w2t_ref = tail_refs
    lo = FF_FULL_CHUNKS * FF_CHUNK
    for dst, src in ((w1t_ref, w1_ref), (w3t_ref, w3_ref)):
        dst[...] = jnp.zeros_like(dst)
        dst[:, 0:FF_TAIL] = src[:, lo:D_FF]
    w2t_ref[...] = jnp.zeros_like(w2t_ref)
    w2t_ref[0:FF_TAIL, :] = w2_ref[lo:D_FF, :]


def _ff_tail_scratch():
    return [pltpu.VMEM((D_MODEL, FF_CHUNK), BF16), pltpu.VMEM((D_MODEL, FF_CHUNK), BF16),
            pltpu.VMEM((FF_CHUNK, D_MODEL), BF16)]


def _swiglu(xn, w1_ref, w3_ref, w2_ref, tail_refs):
    def chunk(acc, w1c, w3c, w2c):
        a = _mm(xn, w1c)
        b = _mm(xn, w3c)
        return acc + _mm((_silu(a) * b).astype(BF16), w2c)

    acc = jnp.zeros((xn.shape[0], D_MODEL), F32)
    for c in range(FF_FULL_CHUNKS):
        sl = slice(c * FF_CHUNK, (c + 1) * FF_CHUNK)
        acc = chunk(acc, w1_ref[:, sl], w3_ref[:, sl], w2_ref[sl, :])
    if FF_TAIL:
        acc = chunk(acc, *(r[...] for r in tail_refs))
    return acc


def _segment_sumsq(x, seg_ref):
    hi, mid, _ = _split3(x * x)
    seg = seg_ref[0:x.shape[1], 0:x.shape[1]]
    return _mm(hi, seg) + _mm(mid, seg)


def _stage_a_kernel(x_ref, g1_ref, w1_ref, w3_ref, w2_ref, gm_ref, win_ref, qg_ref, kg_ref, seg_ref,
                    h_ref, z_ref, xbc_ref, dt_ref, q_ref, k_ref, v_ref, *tail_refs):
    @pl.when(pl.program_id(0) == 0)
    def _():
        _stage_ff_tail(w1_ref, w3_ref, w2_ref, tail_refs)

    x = x_ref[...]
    h = x + 0.5 * _swiglu(_rmsnorm(x, g1_ref[...]).astype(BF16), w1_ref, w3_ref, w2_ref, tail_refs)
    h_ref[...] = h
    proj = _mm(_rmsnorm(h, gm_ref[...]).astype(BF16), win_ref[...])
    z_ref[...] = proj[:, OFF_Z:OFF_XBC]
    xbc_ref[...] = proj[:, OFF_XBC:OFF_DT]
    dt_ref[...] = proj[:, OFF_DT:OFF_Q]
    q = proj[:, OFF_Q:OFF_K]
    k = proj[:, OFF_K:OFF_V]
    q_ref[...] = q * lax.rsqrt(_segment_sumsq(q, seg_ref) * (1.0 / HEAD_DIM) + RMS_EPS) * qg_ref[...]
    k_ref[...] = k * lax.rsqrt(_segment_sumsq(k, seg_ref) * (1.0 / HEAD_DIM) + RMS_EPS) * kg_ref[...]
    v_ref[...] = proj[:, OFF_V:D_PROJ_PAD]


def _layer_block(arr, layer, **kwargs):
    return pl.BlockSpec((None,) + arr.shape[1:], lambda *_: (layer, 0, 0), **kwargs)


def _layer_resident(arr, layer):
    return _layer_block(arr, layer, pipeline_mode=pl.Buffered(1))


def _stage_a(x, pw, layer):
    n = x.shape[0]
    tm = min(TOKEN_TILE, n)
    row = lambda width: pl.BlockSpec((tm, width), lambda i: (i, 0))
    widths = (D_MODEL, D_SSM, CONV_DIM, D_SSM, D_ATTN, KV_WIDTH, KV_WIDTH)
    consts = (pw['g_ffn1'], pw['w1_a'], pw['w3_a'], pw['w2_a'], pw['g_mix'], pw['w_in'],
              pw['q_gain'], pw['k_gain'], pw['seg'])
    return pl.pallas_call(
        _stage_a_kernel,
        grid=(n // tm,),
        in_specs=[row(D_MODEL)] + [_layer_resident(c, layer) for c in consts],
        out_specs=[row(w) for w in widths],
        out_shape=[jax.ShapeDtypeStruct((n, w), F32) for w in widths],
        scratch_shapes=_ff_tail_scratch(),
        compiler_params=pltpu.CompilerParams(dimension_semantics=("arbitrary",),
                                             vmem_limit_bytes=VMEM_LIMIT),
        name="stage_a",
    )(x, *consts)


def _stage_c_kernel(h_ref, mix_ref, pe_ref, wout_ref, g2_ref, w1_ref, w3_ref, w2_ref, gp_ref,
                    wgate_ref, wproj_ref, o_ref, *tail_refs):
    @pl.when(pl.program_id(0) == 0)
    def _():
        _stage_ff_tail(w1_ref, w3_ref, w2_ref, tail_refs)

    h = h_ref[...] + _mm(mix_ref[...].astype(BF16), wout_ref[...])
    h = h + 0.5 * _swiglu(_rmsnorm(h, g2_ref[...]).astype(BF16), w1_ref, w3_ref, w2_ref, tail_refs)
    gate = _sigmoid(_mm(_rmsnorm(h, gp_ref[...]).astype(BF16), wgate_ref[...]))
    o_ref[...] = h + gate * _mm(pe_ref[...].astype(BF16), wproj_ref[...])


def _stage_c(h, mix, pe, pw, layer):
    n = h.shape[0]
    tm = min(TOKEN_TILE, n)
    row = lambda width: pl.BlockSpec((tm, width), lambda i: (i, 0))
    consts = (pw['w_out'], pw['g_ffn2'], pw['w1_b'], pw['w3_b'], pw['w2_b'], pw['g_ple'],
              pw['w_ple_gate'], pw['w_ple_proj'])
    return pl.pallas_call(
        _stage_c_kernel,
        grid=(n // tm,),
        in_specs=[row(D_MODEL), row(D_MODEL), pl.BlockSpec((None, tm, D_PLE), lambda i: (layer, i, 0))]
                 + [_layer_resident(c, layer) for c in consts],
        out_specs=row(D_MODEL),
        out_shape=jax.ShapeDtypeStruct((n, D_MODEL), F32),
        scratch_shapes=_ff_tail_scratch(),
        compiler_params=pltpu.CompilerParams(dimension_semantics=("arbitrary",),
                                             vmem_limit_bytes=VMEM_LIMIT),
        name="stage_c",
    )(h, mix, pe, *consts)


def _lane_low_half(shape):
    return lax.broadcasted_iota(jnp.int32, shape, len(shape) - 1) < HEAD_DIM


def _split_pair(x):
    rolled = pltpu.roll(x, HEAD_DIM, 1)
    low = _lane_low_half(x.shape)
    return jnp.where(low, x, rolled), jnp.where(low, rolled, x)


def _head_query_rows(q, g):
    low = _lane_low_half((q.shape[0], LANES))
    blocks = []
    for jj in range(Q_PER_KV // 2):
        qp = q[:, (g * 2 + jj) * LANES:(g * 2 + jj + 1) * LANES]
        blocks += [jnp.where(low, qp, 0.0), jnp.where(low, 0.0, qp)]
    return jnp.concatenate(blocks, axis=0).astype(BF16)


def _attention_rows(q, kk, vv, sinks_ref):
    r = q.shape[0]
    low = _lane_low_half((r, LANES))
    low_k = _lane_low_half((KEY_PAD, LANES))
    t_i = lax.broadcasted_iota(jnp.int32, (r, KEY_PAD), 0)
    s_i = lax.broadcasted_iota(jnp.int32, (r, KEY_PAD), 1)
    rel = t_i + WINDOW - s_i
    valid = (rel >= 0) & (rel < WINDOW)
    relf = rel.astype(F32)
    k_dup = _split_pair(kk)
    v_roll = pltpu.roll(vv, HEAD_DIM, 1)
    v_even = (jnp.where(low_k, vv, 1.0), jnp.where(low_k, v_roll, 1.0))
    v_odd = (jnp.where(low_k, 1.0, v_roll), jnp.where(low_k, 1.0, vv))
    outs = []
    for g in range(N_KV_HEADS):
        s = _mm_nt(_head_query_rows(q, g), k_dup[g].astype(BF16))
        probs, sink_terms = [], []
        for hh in range(Q_PER_KV):
            head = g * Q_PER_KV + hh
            sh = jnp.where(valid, s[hh * r:(hh + 1) * r] - ALIBI_SLOPES[head] * relf, -jnp.inf)
            sink = sinks_ref[head:head + 1, 0:1]
            m = jnp.maximum(jnp.max(sh, axis=-1, keepdims=True), sink)
            probs.append(jnp.exp(sh - m))
            sink_terms.append(jnp.broadcast_to(jnp.exp(sink - m), (r, LANES)))
        o_even = _mm(jnp.concatenate(probs[0::2], axis=0).astype(BF16), v_even[g].astype(BF16))
        o_odd = _mm(jnp.concatenate(probs[1::2], axis=0).astype(BF16), v_odd[g].astype(BF16))
        for jj in range(Q_PER_KV // 2):
            oe, oo = o_even[jj * r:(jj + 1) * r], o_odd[jj * r:(jj + 1) * r]
            den = (pltpu.roll(jnp.where(low, oo, oe), HEAD_DIM, 1)
                   + jnp.where(low, sink_terms[2 * jj], sink_terms[2 * jj + 1]))
            outs.append(jnp.where(low, oe, oo) / den)
    return jnp.concatenate(outs, axis=1)


def _fill_attention_bias(bias_scr):
    s_i = lax.broadcasted_iota(jnp.int32, (KEY_PAD, WINDOW), 0)
    t_i = lax.broadcasted_iota(jnp.int32, (KEY_PAD, WINDOW), 1)
    rel = t_i + WINDOW - s_i
    valid = (rel >= 0) & (rel < WINDOW)
    relf = rel.astype(F32)
    for head in range(N_HEADS):
        bias_scr[head * KEY_PAD:(head + 1) * KEY_PAD, :] = jnp.where(valid, -ALIBI_SLOPES[head] * relf, -jnp.inf)


def _attention_cols(q, kk, vv, bias_scr, sinks_ref, first_block):
    r = q.shape[0]
    k_dup = _split_pair(kk)
    vt = vv.T
    ones = jnp.ones((HEAD_DIM, KEY_PAD), F32)
    mask_prev = None if first_block is None else jnp.where(first_block, -jnp.inf, 0.0)
    qs = q * (HEAD_DIM ** -0.5)
    outs = []
    for g in range(N_KV_HEADS):
        st = _mm_nt(k_dup[g].astype(BF16), _head_query_rows(qs, g))
        probs, sink_terms = [], []
        for hh in range(Q_PER_KV):
            head = g * Q_PER_KV + hh
            sh = st[:, hh * r:(hh + 1) * r] + bias_scr[head * KEY_PAD:(head + 1) * KEY_PAD, :]
            prev = sh[0:WINDOW] if mask_prev is None else sh[0:WINDOW] + mask_prev
            cur = sh[WINDOW:KEY_PAD]
            sink = sinks_ref[head:head + 1, :]
            m = jnp.maximum(jnp.maximum(jnp.max(prev, axis=0, keepdims=True),
                                        jnp.max(cur, axis=0, keepdims=True)), sink)
            probs.append(jnp.concatenate([jnp.exp(prev - m), jnp.exp(cur - m)], axis=0))
            sink_terms.append(jnp.exp(sink - m))
        pr = jnp.concatenate(probs, axis=1).astype(BF16)
        v_ext = jnp.concatenate([vt[g * HEAD_DIM:(g + 1) * HEAD_DIM], ones], axis=0).astype(BF16)
        ot = _mm(v_ext, pr)
        den = ot[HEAD_DIM:HEAD_DIM + 1, :] + jnp.concatenate(sink_terms, axis=1)
        on = ot[0:HEAD_DIM, :] * (1.0 / den)
        for jj in range(Q_PER_KV // 2):
            pair = jnp.concatenate([on[:, (2 * jj) * r:(2 * jj + 1) * r],
                                    on[:, (2 * jj + 1) * r:(2 * jj + 2) * r]], axis=0)
            outs.append(pair.T)
    return jnp.concatenate(outs, axis=1)


def _gated_group_norm(y, z, gain):
    y = y * _silu(z)
    gw = D_SSM // N_SSM_GROUPS
    parts = [_rmsnorm(y[:, g * gw:(g + 1) * gw], gain[:, g * gw:(g + 1) * gw]) for g in range(N_SSM_GROUPS)]
    return jnp.concatenate(parts, axis=1)


def _prompt_tile_mixer(first, z_ref, xbc_ref, xprev_ref, dt_ref, q_ref, k_ref, kprev_ref, v_ref, vprev_ref,
                       cw_ref, cb_ref, dtb_ref, alog_ref, dskip_ref, snorm_ref, sinks_ref,
                       st_scr, xe_scr, bias_scr, store_mix):
    t = SSD_CHUNK
    rows = xbc_ref.shape[0]
    hw = D_SSM // N_SSM_GROUPS
    nbc = N_SSM_GROUPS * D_STATE

    x = xbc_ref[...]
    xe_scr[0:SUBLANES, :] = jnp.where(first, 0.0, xprev_ref[...])
    xe_scr[SUBLANES:SUBLANES + rows, :] = x
    cw = cw_ref[...]
    conv = cb_ref[...] + cw[CONV_W - 1:CONV_W] * x
    for back in range(1, CONV_W):
        conv = conv + cw[CONV_W - 1 - back:CONV_W - back] * xe_scr[SUBLANES - back:SUBLANES - back + rows, :]
    xc = _silu(conv)

    dt_all = _softplus(dt_ref[...] + dtb_ref[...])
    a_rep = -jnp.exp(alog_ref[...])
    causal = lax.broadcasted_iota(jnp.int32, (t, t), 0) >= lax.broadcasted_iota(jnp.int32, (t, t), 1)
    ltri = causal.astype(BF16)
    low_b = _lane_low_half((t, LANES))
    st = jnp.where(first, 0.0, st_scr[...])
    for c in range(rows // t):
        sl = slice(c * t, (c + 1) * t)
        xs = xc[sl, 0:D_SSM]
        bm = xc[sl, D_SSM:D_SSM + nbc]
        cm = xc[sl, D_SSM + nbc:CONV_DIM]
        dt = dt_all[sl]
        cum3 = _mm(ltri, jnp.concatenate(_split3(dt * a_rep), axis=1))
        cum = cum3[:, 0:D_SSM] + cum3[:, D_SSM:2 * D_SSM] + cum3[:, 2 * D_SSM:3 * D_SSM]
        cum_last = cum[t - 1:t, :]
        xdt = xs * dt
        xdt_b = xdt.astype(BF16)
        xw_b = (xdt * jnp.exp(cum_last - cum)).astype(BF16)
        st_b = st.astype(BF16)
        y_in, y_off, st_new = [], [], []
        for g in range(N_SSM_GROUPS):
            bg = bm[:, g * D_STATE:(g + 1) * D_STATE]
            cg_b = cm[:, g * D_STATE:(g + 1) * D_STATE].astype(BF16)
            cb_t = _mm_nt(cg_b, bg.astype(BF16))
            y_off.append(_mm(cg_b, st_b[:, g * hw:(g + 1) * hw]))
            st_new.append(_mm(bg.T.astype(BF16), xw_b[:, g * hw:(g + 1) * hw]))
            for jj in range(hw // LANES):
                j = g * (hw // LANES) + jj
                scores = []
                for ch in _split_pair(cum[:, j * LANES:(j + 1) * LANES]):
                    decay = jnp.exp(jnp.where(causal, ch - ch.T, -jnp.inf))
                    scores.append((cb_t * decay).astype(BF16))
                xp = xdt_b[:, j * LANES:(j + 1) * LANES]
                zero = jnp.zeros_like(xp)
                rhs = jnp.concatenate([jnp.where(low_b, xp, zero), jnp.where(low_b, zero, xp)], axis=0)
                y_in.append(_mm(jnp.concatenate(scores, axis=1), rhs))
        y = (jnp.concatenate(y_in, axis=1) + jnp.concatenate(y_off, axis=1) * jnp.exp(cum)
             + dskip_ref[...] * xs)
        st = st * jnp.exp(cum_last) + jnp.concatenate(st_new, axis=1)
        ssd_out = _gated_group_norm(y, z_ref[sl, :], snorm_ref[...])

        k_before = kprev_ref[...] if c == 0 else k_ref[(c - 1) * t:c * t, :]
        v_before = vprev_ref[...] if c == 0 else v_ref[(c - 1) * t:c * t, :]
        kk = jnp.concatenate([k_before, k_ref[sl, :]], axis=0)
        vv = jnp.concatenate([v_before, v_ref[sl, :]], axis=0)
        attn_out = _attention_cols(q_ref[sl, :], kk, vv, bias_scr, sinks_ref, first if c == 0 else None)
        store_mix(sl, ssd_out, attn_out)
    st_scr[...] = st
    return st


def _mixer_prompt_kernel(*refs, tiles_per_seq):
    mix_ref, state_ref, st_scr, xe_scr, bias_scr = refs[-5:]
    j = pl.program_id(0)

    @pl.when(j == 0)
    def _():
        _fill_attention_bias(bias_scr)

    def store_mix(rows, ssd_out, attn_out):
        mix_ref[rows, 0:D_SSM] = ssd_out.astype(mix_ref.dtype)
        mix_ref[rows, D_SSM:D_SSM + D_ATTN] = attn_out.astype(mix_ref.dtype)

    st = _prompt_tile_mixer(lax.rem(j, tiles_per_seq) == 0, *refs[:-5], st_scr, xe_scr, bias_scr, store_mix)

    @pl.when(lax.rem(j, tiles_per_seq) == tiles_per_seq - 1)
    def _():
        for jb in range(D_SSM // LANES):
            state_ref[jb * LANES:(jb + 1) * LANES, :] = st[:, jb * LANES:(jb + 1) * LANES].T


def _mixer_in_specs(tile, shift):
    t = SSD_CHUNK
    at = lambda i: jnp.maximum(i - shift, 0)
    cur = lambda width: pl.BlockSpec((tile, width), lambda i: (at(i), 0))
    before = lambda rows, width: pl.BlockSpec(
        (rows, width), lambda i: (jnp.maximum(at(i) * (tile // rows) - 1, 0), 0))
    return [cur(D_SSM), cur(CONV_DIM), before(SUBLANES, CONV_DIM), cur(D_SSM), cur(D_ATTN),
            cur(KV_WIDTH), before(t, KV_WIDTH), cur(KV_WIDTH), before(t, KV_WIDTH)]


def _mixer_consts(pw):
    return (pw['conv_w'], pw['conv_b'], pw['dt_bias'], pw['a_log'], pw['d_skip'], pw['ssm_norm'], pw['sinks'])


def _mixer_scratch(tile):
    return [pltpu.VMEM((D_STATE, D_SSM), F32),
            pltpu.VMEM((SUBLANES + tile, CONV_DIM), F32),
            pltpu.VMEM((N_HEADS * KEY_PAD, WINDOW), F32)]


def _mixer_prompt(z, xbc, dtr, q, k, v, pw, layer, batch, seq):
    tile = MIXER_TILE
    tiles_per_seq = seq // tile
    consts = _mixer_consts(pw)
    return pl.pallas_call(
        functools.partial(_mixer_prompt_kernel, tiles_per_seq=tiles_per_seq),
        grid=(batch * tiles_per_seq,),
        in_specs=_mixer_in_specs(tile, 0) + [_layer_block(cst, layer) for cst in consts],
        out_specs=[pl.BlockSpec((tile, D_SSM + D_ATTN), lambda i: (i, 0)),
                   pl.BlockSpec((None, D_SSM, D_STATE), lambda i: (i // tiles_per_seq, 0, 0))],
        out_shape=[jax.ShapeDtypeStruct((batch * seq, D_SSM + D_ATTN), BF16),
                   jax.ShapeDtypeStruct((batch, D_SSM, D_STATE), F32)],
        scratch_shapes=_mixer_scratch(tile),
        compiler_params=pltpu.CompilerParams(dimension_semantics=("arbitrary",),
                                             vmem_limit_bytes=VMEM_LIMIT),
        name="mixer_prompt",
    )(z, xbc, xbc, dtr, q, k, k, v, v, *consts)


N_SAMPLE_IN = 17


def _put_lane_blocks(scr, col0, rows, val):
    for j in range(val.shape[1] // LANES):
        scr[col0 // LANES + j, rows, :] = val[:, j * LANES:(j + 1) * LANES]


def _get_lane_blocks(scr, col0, width, rows):
    return jnp.concatenate([scr[col0 // LANES + j, rows, :] for j in range(width // LANES)], axis=1)


def _mixer_sample_kernel(*refs):
    (z_ref, xbc_ref, dt_ref, q_ref, k_ref, v_ref, sst_ref, sconv_ref, ck_ref, cv_ref,
     cw_ref, cb_ref, dtb_ref, alog_ref, dskip_ref, snorm_ref, sinks_ref) = refs[:N_SAMPLE_IN]
    (mix_ref, sst_out_ref, ck_out_ref, cv_out_ref,
     ypart_scr, ecum_scr, cdec_scr, xw_scr, xt_scr, bmat_scr, seq_scr, res_scr, kk_scr, vv_scr) = refs[-14:]
    dl, bb = xbc_ref.shape[0], xbc_ref.shape[1]
    hw = D_SSM // N_SSM_GROUPS
    nbc = N_SSM_GROUPS * D_STATE
    c_q, c_c, c_k, c_v = 0, D_ATTN, D_ATTN + nbc, D_ATTN + nbc + KV_WIDTH
    xw_scr[...] = jnp.zeros_like(xw_scr)
    bmat_scr[...] = jnp.zeros_like(bmat_scr)
    seq_scr[...] = jnp.zeros_like(seq_scr)
    kk_scr[...] = jnp.zeros_like(kk_scr)
    vv_scr[...] = jnp.zeros_like(vv_scr)

    cw = cw_ref[...]
    a_rep = -jnp.exp(alog_ref[...])
    xs, bm, cm, xdt, cum = [], [], [], [], []
    for t in range(dl):
        conv = cb_ref[...] + cw[CONV_W - 1:CONV_W] * xbc_ref[t]
        for back in range(1, CONV_W):
            src = xbc_ref[t - back] if t >= back else sconv_ref[CONV_W - 1 + t - back]
            conv = conv + cw[CONV_W - 1 - back:CONV_W - back] * src
        xc = _silu(conv)
        xs.append(xc[:, 0:D_SSM])
        bm.append(xc[:, D_SSM:D_SSM + nbc])
        cm.append(xc[:, D_SSM + nbc:CONV_DIM])
        dt = _softplus(dt_ref[t] + dtb_ref[...])
        cum.append(dt * a_rep if t == 0 else cum[-1] + dt * a_rep)
        xdt.append(xs[t] * dt)
    cdec_scr[...] = jnp.exp(cum[dl - 1])
    for t in range(dl):
        y = dskip_ref[...] * xs[t]
        for s in range(t + 1):
            prod = cm[t] * bm[s]
            dots = jnp.concatenate(
                [jnp.broadcast_to(jnp.sum(prod[:, g * D_STATE:(g + 1) * D_STATE], axis=-1, keepdims=True), (bb, hw))
                 for g in range(N_SSM_GROUPS)], axis=1)
            y = y + (dots * xdt[s] if s == t else dots * jnp.exp(cum[t] - cum[s]) * xdt[s])
        rows_t = slice(t * bb, (t + 1) * bb)
        ypart_scr[rows_t, :] = y
        ecum_scr[rows_t, :] = jnp.exp(cum[t])
        xw_scr[rows_t, :] = xdt[t] * jnp.exp(cum[dl - 1] - cum[t])
        bmat_scr[rows_t, :] = bm[t]
        own_rows = pl.ds(t, bb, stride=SEQ_ROWS)
        _put_lane_blocks(seq_scr, c_q, own_rows, q_ref[t] * (HEAD_DIM ** -0.5))
        _put_lane_blocks(seq_scr, c_c, own_rows, cm[t])
        _put_lane_blocks(seq_scr, c_k, own_rows, k_ref[t])
        _put_lane_blocks(seq_scr, c_v, own_rows, v_ref[t])
    xw = xw_scr[...]
    for j in range(D_SSM // LANES):
        xt_scr[j * LANES:(j + 1) * LANES, :] = xw[:, j * LANES:(j + 1) * LANES].T.astype(BF16)

    block_row = lax.broadcasted_iota(jnp.int32, (SSD_CHUNK, D_STATE), 0)

    def one_sequence(b, slot):
        r0 = pl.multiple_of(b * SEQ_ROWS, SEQ_ROWS)
        my_rows = pl.ds(r0, SEQ_ROWS)
        state = sst_ref[b]
        state_b = state.astype(BF16)
        c_rows = jnp.concatenate([_get_lane_blocks(seq_scr, c_c, nbc, my_rows),
                                  jnp.zeros((SEQ_ROWS, nbc), F32)], axis=0).astype(BF16)
        y_off = [_mm_nt(c_rows[:, g * D_STATE:(g + 1) * D_STATE], state_b[g * hw:(g + 1) * hw, :])[0:SEQ_ROWS]
                 for g in range(N_SSM_GROUPS)]
        _put_lane_blocks(res_scr, 0, my_rows, jnp.concatenate(y_off, axis=1))
        is_mine = (block_row & (bb - 1)) == b
        decay = jnp.broadcast_to(cdec_scr[pl.ds(b, 1), :], (SUBLANES, D_SSM))
        for g in range(N_SSM_GROUPS):
            rhs = jnp.where(is_mine, bmat_scr[:, g * D_STATE:(g + 1) * D_STATE], 0.0).astype(BF16)
            upd = _mm(xt_scr[g * hw:(g + 1) * hw, :], rhs)
            for jj in range(hw // LANES):
                j = g * (hw // LANES) + jj
                for half, dec in enumerate(_split_pair(decay[:, j * LANES:(j + 1) * LANES])):
                    h0 = j * LANES + half * HEAD_DIM
                    u0 = jj * LANES + half * HEAD_DIM
                    sst_out_ref[b, h0:h0 + HEAD_DIM, :] = (
                        state[h0:h0 + HEAD_DIM] * jnp.broadcast_to(dec[0:1], (HEAD_DIM, D_STATE))
                        + upd[u0:u0 + HEAD_DIM])
        kk_scr[slot, 0:WINDOW, :] = ck_ref[b]
        kk_scr[slot, WINDOW:WINDOW + SEQ_ROWS, :] = _get_lane_blocks(seq_scr, c_k, KV_WIDTH, my_rows)
        vv_scr[slot, 0:WINDOW, :] = cv_ref[b]
        vv_scr[slot, WINDOW:WINDOW + SEQ_ROWS, :] = _get_lane_blocks(seq_scr, c_v, KV_WIDTH, my_rows)
        ck_out_ref[b] = kk_scr[slot, dl:dl + WINDOW, :]
        cv_out_ref[b] = vv_scr[slot, dl:dl + WINDOW, :]
        att = _attention_rows(_get_lane_blocks(seq_scr, c_q, D_ATTN, my_rows), kk_scr[slot], vv_scr[slot], sinks_ref)
        _put_lane_blocks(res_scr, D_SSM, my_rows, att)

    def per_group(i, carry):
        for slot in range(SAMPLE_UNROLL):
            one_sequence(i * SAMPLE_UNROLL + slot, slot)
        return carry

    lax.fori_loop(0, bb // SAMPLE_UNROLL, per_group, 0)

    for t in range(dl):
        own_rows = pl.ds(t, bb, stride=SEQ_ROWS)
        rows_t = slice(t * bb, (t + 1) * bb)
        y = ypart_scr[rows_t, :] + _get_lane_blocks(res_scr, 0, D_SSM, own_rows) * ecum_scr[rows_t, :]
        mix_ref[t, :, 0:D_SSM] = _gated_group_norm(y, z_ref[t], snorm_ref[...])
        mix_ref[t, :, D_SSM:D_SSM + D_ATTN] = _get_lane_blocks(res_scr, D_SSM, D_ATTN, own_rows)


def _mixer_sample(z, xbc, dtr, q, k, v, state_ssm, state_conv, cache_k, cache_v, pw, layer, carried):
    depth = state_ssm.shape[0]
    dl, nb = xbc.shape[0], xbc.shape[1]
    bb = min(SAMPLE_SEQS_PER_STEP, nb)
    assert bb & (bb - 1) == 0 and dl * bb <= SSD_CHUNK and dl <= SEQ_ROWS and bb % SAMPLE_UNROLL == 0
    tok = lambda width: pl.BlockSpec((dl, bb, width), lambda i: (0, i, 0))
    lblk = lambda *dims: pl.BlockSpec((None, bb) + dims, lambda i: (layer, i) + (0,) * len(dims))
    consts = (pw['conv_w'], pw['conv_b'], pw['dt_bias'], pw['a_log'], pw['d_skip'], pw['ssm_norm'], pw['sinks'])
    operands = (z, xbc, dtr, q, k, v, state_ssm, state_conv, cache_k, cache_v) + consts
    assert len(operands) == N_SAMPLE_IN
    seq_cols = D_ATTN + N_SSM_GROUPS * D_STATE + 2 * KV_WIDTH
    return pl.pallas_call(
        _mixer_sample_kernel,
        grid=(nb // bb,),
        in_specs=[tok(D_SSM), tok(CONV_DIM), tok(D_SSM), tok(D_ATTN), tok(KV_WIDTH), tok(KV_WIDTH),
                  lblk(D_SSM, D_STATE),
                  pl.BlockSpec((None, CONV_W - 1, bb, CONV_DIM), lambda i: (layer, 0, i, 0)),
                  lblk(WINDOW, KV_WIDTH), lblk(WINDOW, KV_WIDTH)]
                 + [_layer_block(cst, layer) for cst in consts]
                 + [pl.BlockSpec(memory_space=pl.ANY)] * len(carried),
        out_specs=[tok(D_SSM + D_ATTN), lblk(D_SSM, D_STATE), lblk(WINDOW, KV_WIDTH), lblk(WINDOW, KV_WIDTH)],
        out_shape=[jax.ShapeDtypeStruct((dl, nb, D_SSM + D_ATTN), F32),
                   jax.ShapeDtypeStruct((depth, nb, D_SSM, D_STATE), F32),
                   jax.ShapeDtypeStruct((depth, nb, WINDOW, KV_WIDTH), F32),
                   jax.ShapeDtypeStruct((depth, nb, WINDOW, KV_WIDTH), F32)],
        input_output_aliases={N_SAMPLE_IN + n: 1 + n for n in range(len(carried))},
        scratch_shapes=[pltpu.VMEM((dl * bb, D_SSM), F32),
                        pltpu.VMEM((dl * bb, D_SSM), F32),
                        pltpu.VMEM((bb, D_SSM), F32),
                        pltpu.VMEM((SSD_CHUNK, D_SSM), F32),
                        pltpu.VMEM((D_SSM, SSD_CHUNK), BF16),
                        pltpu.VMEM((SSD_CHUNK, N_SSM_GROUPS * D_STATE), F32),
                        pltpu.VMEM((seq_cols // LANES, bb * SEQ_ROWS, LANES), F32),
                        pltpu.VMEM(((D_SSM + D_ATTN) // LANES, bb * SEQ_ROWS, LANES), F32),
                        pltpu.VMEM((SAMPLE_UNROLL, KEY_PAD, KV_WIDTH), F32),
                        pltpu.VMEM((SAMPLE_UNROLL, KEY_PAD, KV_WIDTH), F32)],
        compiler_params=pltpu.CompilerParams(dimension_semantics=("arbitrary",),
                                             vmem_limit_bytes=VMEM_LIMIT),
        name="mixer_sample",
    )(*operands, *carried)


def _prepare_weights(p):
    bf = lambda w: w.astype(BF16)
    vec = lambda v: v[:, None, :]

    rep = lambda v: jnp.repeat(v, HEAD_DIM, axis=-1)[:, None, :]
    cuts = np.cumsum([D_SSM, CONV_DIM, N_SSM_HEADS, D_ATTN, KV_WIDTH])
    wz, wxbc, wdt, wq, wk, wv = jnp.split(bf(p['w_in']), cuts, axis=2)
    depth = p['w_in'].shape[0]
    seg = np.kron(np.eye(D_ATTN // HEAD_DIM, dtype=np.float32), np.ones((HEAD_DIM, HEAD_DIM), np.float32))
    return {
        'g_ffn1': vec(p['g_ffn1']), 'g_mix': vec(p['g_mix']), 'g_ffn2': vec(p['g_ffn2']), 'g_ple': vec(p['g_ple']),
        'w1_a': bf(p['w1_a']), 'w3_a': bf(p['w3_a']), 'w2_a': bf(p['w2_a']),
        'w1_b': bf(p['w1_b']), 'w3_b': bf(p['w3_b']), 'w2_b': bf(p['w2_b']),
        'w_in': jnp.concatenate([wz, wxbc, jnp.repeat(wdt, HEAD_DIM, axis=2), wq, wk, wv], axis=2),
        'w_out': bf(p['w_out']), 'w_ple_gate': bf(p['w_ple_gate']), 'w_ple_proj': bf(p['w_ple_proj']),
        'q_gain': vec(jnp.tile(p['q_norm'], (1, N_HEADS))),
        'k_gain': vec(jnp.tile(p['k_norm'], (1, N_KV_HEADS))),
        'seg': jnp.broadcast_to(jnp.asarray(seg, BF16), (depth,) + seg.shape),
        'conv_w': p['conv_w'], 'conv_b': vec(p['conv_b']),
        'dt_bias': rep(p['dt_bias']), 'a_log': rep(p['a_log']), 'd_skip': rep(p['d_skip']),
        'ssm_norm': vec(p['ssm_norm']),
        'sinks': jnp.broadcast_to(p['sinks'][:, :, None], (depth, N_HEADS, LANES)),
    }


def kernel(x_prompt, x_sample, state_ssm, state_conv, cache_k_win, cache_v_win, p_prompt, p_sample, g_ffn1, w1_a, w3_a, w2_a, g_mix, w_in, conv_w, conv_b, dt_bias, a_log, d_skip, ssm_norm, q_norm, k_norm, sinks, w_out, g_ffn2, w1_b, w3_b, w2_b, g_ple, w_ple_gate, w_ple_proj):
    params = dict(g_ffn1=g_ffn1, w1_a=w1_a, w3_a=w3_a, w2_a=w2_a, g_mix=g_mix, w_in=w_in, conv_w=conv_w,
                  conv_b=conv_b, dt_bias=dt_bias, a_log=a_log, d_skip=d_skip, ssm_norm=ssm_norm, q_norm=q_norm,
                  k_norm=k_norm, sinks=sinks, w_out=w_out, g_ffn2=g_ffn2, w1_b=w1_b, w3_b=w3_b, w2_b=w2_b,
                  g_ple=g_ple, w_ple_gate=w_ple_gate, w_ple_proj=w_ple_proj)
    depth = w_in.shape[0]
    bp, seq, _ = x_prompt.shape
    bs, dl, _ = x_sample.shape
    assert seq % SSD_CHUNK == 0 and seq >= WINDOW and dl >= CONV_W - 1 and dl <= SUBLANES
    assert (bp * seq) % TOKEN_TILE == 0 and (bs * dl) % min(TOKEN_TILE, bs * dl) == 0

    pw = _prepare_weights(params)
    sst_in = state_ssm.reshape(depth, bs, D_SSM, D_STATE)
    ck_in = cache_k_win.reshape(depth, bs, WINDOW, KV_WIDTH)
    cv_in = cache_v_win.reshape(depth, bs, WINDOW, KV_WIDTH)
    sconv_in = jnp.swapaxes(state_conv, 1, 2)
    pe_p = p_prompt.reshape(depth, bp * seq, D_PLE)
    pe_s = jnp.swapaxes(p_sample, 1, 2).reshape(depth, dl * bs, D_PLE)
    hp = x_prompt.reshape(bp * seq, D_MODEL)
    hs = jnp.swapaxes(x_sample, 0, 1).reshape(dl * bs, D_MODEL)
    ssm_p, conv_p, k_p, v_p, conv_s = [], [], [], [], []
    carried = ()
    for l in range(depth):
        hp, zp, xbcp, dtp, qp, kp, vp = _stage_a(hp, pw, l)
        hs, zs, xbcs, dts, qs, ks, vs = _stage_a(hs, pw, l)
        mixp, sp = _mixer_prompt(zp, xbcp, dtp, qp, kp, vp, pw, l, bp, seq)
        tm3 = lambda a: a.reshape(dl, bs, a.shape[-1])
        mixs, *carried = _mixer_sample(tm3(zs), tm3(xbcs), tm3(dts), tm3(qs), tm3(ks), tm3(vs),
                                       sst_in, sconv_in, ck_in, cv_in, pw, l, tuple(carried))
        hp = _stage_c(hp, mixp, pe_p, pw, l)
        hs = _stage_c(hs, mixs.reshape(dl * bs, D_MODEL), pe_s, pw, l)
        ssm_p.append(sp.reshape(bp, N_SSM_HEADS, HEAD_DIM, D_STATE))
        conv_p.append(xbcp.reshape(bp, seq, CONV_DIM)[:, seq - (CONV_W - 1):])
        last_window = lambda a: a.reshape(bp, seq, KV_WIDTH)[:, seq - WINDOW:].reshape(
            bp, WINDOW, N_KV_HEADS, HEAD_DIM)
        k_p.append(last_window(kp))
        v_p.append(last_window(vp))
        conv_s.append(jnp.swapaxes(tm3(xbcs)[dl - (CONV_W - 1):], 0, 1))
    ssm_s, k_s, v_s = carried
    return (hp.reshape(bp, seq, D_MODEL), jnp.swapaxes(hs.reshape(dl, bs, D_MODEL), 0, 1),
            jnp.stack(ssm_p), jnp.stack(conv_p), jnp.stack(k_p), jnp.stack(v_p),
            ssm_s.reshape(depth, bs, N_SSM_HEADS, HEAD_DIM, D_STATE), jnp.stack(conv_s),
            k_s.reshape(depth, bs, WINDOW, N_KV_HEADS, HEAD_DIM),
            v_s.reshape(depth, bs, WINDOW, N_KV_HEADS, HEAD_DIM))
```

```python
import functools

import numpy as np
import jax
import jax.numpy as jnp
from jax import lax
from jax.experimental import pallas as pl
from jax.experimental.pallas import tpu as pltpu

F32 = jnp.float32
BF16 = jnp.bfloat16

D_MODEL = 1024
HEAD_DIM = 64
D_SSM = 512
N_SSM_HEADS = 8
N_SSM_GROUPS = 2
D_STATE = 128
CONV_W = 4
CONV_DIM = D_SSM + 2 * N_SSM_GROUPS * D_STATE
SSD_CHUNK = 128
D_ATTN = 512
N_HEADS = 8
N_KV_HEADS = 2
Q_PER_KV = N_HEADS // N_KV_HEADS
KV_WIDTH = N_KV_HEADS * HEAD_DIM
WINDOW = 128
D_FF = 2752
D_PLE = 256
RMS_EPS = 1e-6

LANES = 128
SUBLANES = 8
MXU_DIM = 256
FF_CHUNK = MXU_DIM
FF_FULL_CHUNKS = D_FF // FF_CHUNK
FF_TAIL = D_FF - FF_FULL_CHUNKS * FF_CHUNK
TOKEN_TILE = 512
MIXER_TILE = 512
SAMPLE_SEQS_PER_STEP = 16
SAMPLE_UNROLL = 4
SEQ_ROWS = SUBLANES
KEY_PAD = 2 * WINDOW
VMEM_LIMIT = 56 * 1024 * 1024

OFF_Z = 0
OFF_XBC = OFF_Z + D_SSM
OFF_Q = OFF_XBC + CONV_DIM
OFF_K = OFF_Q + D_ATTN
OFF_V = OFF_K + KV_WIDTH
OFF_DT = OFF_V + KV_WIDTH
D_PROJ_PAD = OFF_DT + LANES

ALIBI_SLOPES = tuple(float(s) for s in np.power(
    np.float32(2.0), -8.0 * np.arange(1, N_HEADS + 1, dtype=np.float32) / N_HEADS))

_NT = (((1,), (1,)), ((), ()))


def _mm(a, b):
    return jnp.dot(a, b, preferred_element_type=F32)


def _mm_nt(a, b):
    return lax.dot_general(a, b, _NT, preferred_element_type=F32)


def _sigmoid(x):
    return 1.0 / (1.0 + jnp.exp(-x))


def _silu(x):
    return x * _sigmoid(x)


def _softplus(x):
    return jnp.maximum(x, 0.0) + jnp.log(1.0 + jnp.exp(-jnp.abs(x)))


def _rmsnorm(x, g):
    return x * lax.rsqrt(jnp.mean(x * x, axis=-1, keepdims=True) + RMS_EPS) * g


def _split3(x):
    hi = x.astype(BF16)
    rest = x - hi.astype(F32)
    mid = rest.astype(BF16)
    return hi, mid, (rest - mid.astype(F32)).astype(BF16)


def _stage_ff_tail(w1_ref, w3_ref, w2_ref, tail_refs):
    w1t_ref, w3t_ref, w2t_ref = tail_refs
    lo = FF_FULL_CHUNKS * FF_CHUNK
    for dst, src in ((w1t_ref, w1_ref), (w3t_ref, w3_ref)):
        dst[...] = jnp.zeros_like(dst)
        dst[:, 0:FF_TAIL] = src[:, lo:D_FF]
    w2t_ref[...] = jnp.zeros_like(w2t_ref)
    w2t_ref[0:FF_TAIL, :] = w2_ref[lo:D_FF, :]


def _ff_tail_scratch():
    return [pltpu.VMEM((D_MODEL, FF_CHUNK), BF16), pltpu.VMEM((D_MODEL, FF_CHUNK), BF16),
            pltpu.VMEM((FF_CHUNK, D_MODEL), BF16)]


def _swiglu(xn, w1_ref, w3_ref, w2_ref, tail_refs):
    def chunk(acc, w1c, w3c, w2c):
        a = _mm(xn, w1c)
        b = _mm(xn, w3c)
        return acc + _mm((_silu(a) * b).astype(BF16), w2c)

    acc = jnp.zeros((xn.shape[0], D_MODEL), F32)
    for c in range(FF_FULL_CHUNKS):
        sl = slice(c * FF_CHUNK, (c + 1) * FF_CHUNK)
        acc = chunk(acc, w1_ref[:, sl], w3_ref[:, sl], w2_ref[sl, :])
    if FF_TAIL:
        acc = chunk(acc, *(r[...] for r in tail_refs))
    return acc


def _segment_sumsq(x, seg_ref):
    hi, mid, _ = _split3(x * x)
    seg = seg_ref[0:x.shape[1], 0:x.shape[1]]
    return _mm(hi, seg) + _mm(mid, seg)


def _stage_a_kernel(x_ref, g1_ref, w1_ref, w3_ref, w2_ref, gm_ref, win_ref, qg_ref, kg_ref, seg_ref,
                    h_ref, z_ref, xbc_ref, dt_ref, q_ref, k_ref, v_ref, *tail_refs):
    @pl.when(pl.program_id(0) == 0)
    def _():
        _stage_ff_tail(w1_ref, w3_ref, w2_ref, tail_refs)

    x = x_ref[...]
    h = x + 0.5 * _swiglu(_rmsnorm(x, g1_ref[...]).astype(BF16), w1_ref, w3_ref, w2_ref, tail_refs)
    h_ref[...] = h
    proj = _mm(_rmsnorm(h, gm_ref[...]).astype(BF16), win_ref[...])
    z_ref[...] = proj[:, OFF_Z:OFF_XBC]
    xbc_ref[...] = proj[:, OFF_XBC:OFF_Q]
    dt_ref[...] = proj[:, OFF_DT:D_PROJ_PAD]
    q = proj[:, OFF_Q:OFF_K]
    k = proj[:, OFF_K:OFF_V]
    q_ref[...] = q * lax.rsqrt(_segment_sumsq(q, seg_ref) * (1.0 / HEAD_DIM) + RMS_EPS) * qg_ref[...]
    k_ref[...] = k * lax.rsqrt(_segment_sumsq(k, seg_ref) * (1.0 / HEAD_DIM) + RMS_EPS) * kg_ref[...]
    v_ref[...] = proj[:, OFF_V:OFF_DT]


def _layer_block(arr, layer, **kwargs):
    return pl.BlockSpec((None,) + arr.shape[1:], lambda *_: (layer, 0, 0), **kwargs)


def _layer_resident(arr, layer):
    return _layer_block(arr, layer, pipeline_mode=pl.Buffered(1))


def _stage_a(x, pw, layer):
    n = x.shape[0]
    tm = min(TOKEN_TILE, n)
    row = lambda width: pl.BlockSpec((tm, width), lambda i: (i, 0))
    widths = (D_MODEL, D_SSM, CONV_DIM, LANES, D_ATTN, KV_WIDTH, KV_WIDTH)
    consts = (pw['g_ffn1'], pw['w1_a'], pw['w3_a'], pw['w2_a'], pw['g_mix'], pw['w_in'],
              pw['q_gain'], pw['k_gain'], pw['seg'])
    return pl.pallas_call(
        _stage_a_kernel,
        grid=(n // tm,),
        in_specs=[row(D_MODEL)] + [_layer_resident(c, layer) for c in consts],
        out_specs=[row(w) for w in widths],
        out_shape=[jax.ShapeDtypeStruct((n, w), F32) for w in widths],
        scratch_shapes=_ff_tail_scratch(),
        compiler_params=pltpu.CompilerParams(dimension_semantics=("arbitrary",),
                                             vmem_limit_bytes=VMEM_LIMIT),
        name="stage_a",
    )(x, *consts)


def _stage_c_kernel(h_ref, mix_ref, pe_ref, wout_ref, g2_ref, w1_ref, w3_ref, w2_ref, gp_ref,
                    wgate_ref, wproj_ref, o_ref, *tail_refs):
    @pl.when(pl.program_id(0) == 0)
    def _():
        _stage_ff_tail(w1_ref, w3_ref, w2_ref, tail_refs)

    o_ref[...] = _stage_c_compute(h_ref[...], mix_ref[...].astype(BF16), pe_ref, wout_ref, g2_ref, w1_ref,
                                  w3_ref, w2_ref, gp_ref, wgate_ref, wproj_ref, tail_refs)


def _stage_c_compute(h, mix, pe_ref, wout_ref, g2_ref, w1_ref, w3_ref, w2_ref, gp_ref, wgate_ref, wproj_ref,
                     tail_refs):
    h = h + _mm(mix, wout_ref[...])
    h = h + 0.5 * _swiglu(_rmsnorm(h, g2_ref[...]).astype(BF16), w1_ref, w3_ref, w2_ref, tail_refs)
    gate = _sigmoid(_mm(_rmsnorm(h, gp_ref[...]).astype(BF16), wgate_ref[...]))
    return h + gate * _mm(pe_ref[...].astype(BF16), wproj_ref[...])


def _stage_c(h, mix, pe, pw, layer):
    n = h.shape[0]
    tm = min(TOKEN_TILE, n)
    row = lambda width: pl.BlockSpec((tm, width), lambda i: (i, 0))
    consts = (pw['w_out'], pw['g_ffn2'], pw['w1_b'], pw['w3_b'], pw['w2_b'], pw['g_ple'],
              pw['w_ple_gate'], pw['w_ple_proj'])
    return pl.pallas_call(
        _stage_c_kernel,
        grid=(n // tm,),
        in_specs=[row(D_MODEL), row(D_MODEL), pl.BlockSpec((None, tm, D_PLE), lambda i: (layer, i, 0))]
                 + [_layer_resident(c, layer) for c in consts],
        out_specs=row(D_MODEL),
        out_shape=jax.ShapeDtypeStruct((n, D_MODEL), F32),
        scratch_shapes=_ff_tail_scratch(),
        compiler_params=pltpu.CompilerParams(dimension_semantics=("arbitrary",),
                                             vmem_limit_bytes=VMEM_LIMIT),
        name="stage_c",
    )(h, mix, pe, *consts)


def _lane_low_half(shape):
    return lax.broadcasted_iota(jnp.int32, shape, len(shape) - 1) < HEAD_DIM


def _split_pair(x):
    rolled = pltpu.roll(x, HEAD_DIM, 1)
    low = _lane_low_half(x.shape)
    return jnp.where(low, x, rolled), jnp.where(low, rolled, x)


def _head_query_rows(q, g):
    low = _lane_low_half((q.shape[0], LANES))
    blocks = []
    for jj in range(Q_PER_KV // 2):
        qp = q[:, (g * 2 + jj) * LANES:(g * 2 + jj + 1) * LANES]
        blocks += [jnp.where(low, qp, 0.0), jnp.where(low, 0.0, qp)]
    return jnp.concatenate(blocks, axis=0).astype(BF16)


def _attention_rows(q, kk, vv, sinks_ref):
    r = q.shape[0]
    low = _lane_low_half((r, LANES))
    low_k = _lane_low_half((KEY_PAD, LANES))
    t_i = lax.broadcasted_iota(jnp.int32, (r, KEY_PAD), 0)
    s_i = lax.broadcasted_iota(jnp.int32, (r, KEY_PAD), 1)
    rel = t_i + WINDOW - s_i
    valid = (rel >= 0) & (rel < WINDOW)
    relf = rel.astype(F32)
    k_dup = _split_pair(kk)
    v_roll = pltpu.roll(vv, HEAD_DIM, 1)
    v_even = (jnp.where(low_k, vv, 1.0), jnp.where(low_k, v_roll, 1.0))
    v_odd = (jnp.where(low_k, 1.0, v_roll), jnp.where(low_k, 1.0, vv))
    outs = []
    for g in range(N_KV_HEADS):
        s = _mm_nt(_head_query_rows(q, g), k_dup[g].astype(BF16))
        probs, sink_terms = [], []
        for hh in range(Q_PER_KV):
            head = g * Q_PER_KV + hh
            sh = jnp.where(valid, s[hh * r:(hh + 1) * r] - ALIBI_SLOPES[head] * relf, -jnp.inf)
            sink = sinks_ref[head:head + 1, 0:1]
            m = jnp.maximum(jnp.max(sh, axis=-1, keepdims=True), sink)
            probs.append(jnp.exp(sh - m))
            sink_terms.append(jnp.broadcast_to(jnp.exp(sink - m), (r, LANES)))
        o_even = _mm(jnp.concatenate(probs[0::2], axis=0).astype(BF16), v_even[g].astype(BF16))
        o_odd = _mm(jnp.concatenate(probs[1::2], axis=0).astype(BF16), v_odd[g].astype(BF16))
        for jj in range(Q_PER_KV // 2):
            oe, oo = o_even[jj * r:(jj + 1) * r], o_odd[jj * r:(jj + 1) * r]
            den = (pltpu.roll(jnp.where(low, oo, oe), HEAD_DIM, 1)
                   + jnp.where(low, sink_terms[2 * jj], sink_terms[2 * jj + 1]))
            outs.append(jnp.where(low, oe, oo) / den)
    return jnp.concatenate(outs, axis=1)


def _fill_attention_bias(bias_scr):
    s_i = lax.broadcasted_iota(jnp.int32, (KEY_PAD, WINDOW), 0)
    t_i = lax.broadcasted_iota(jnp.int32, (KEY_PAD, WINDOW), 1)
    rel = t_i + WINDOW - s_i
    valid = (rel >= 0) & (rel < WINDOW)
    relf = rel.astype(F32)
    for head in range(N_HEADS):
        bias_scr[head * KEY_PAD:(head + 1) * KEY_PAD, :] = jnp.where(valid, -ALIBI_SLOPES[head] * relf, -jnp.inf)


def _attention_cols(q, kk, vv, bias_scr, sinks_ref, first_block):
    r = q.shape[0]
    k_dup = _split_pair(kk)
    vt = vv.T
    ones = jnp.ones((HEAD_DIM, KEY_PAD), F32)
    mask_prev = None if first_block is None else jnp.where(first_block, -jnp.inf, 0.0)
    qs = q * (HEAD_DIM ** -0.5)
    outs = []
    for g in range(N_KV_HEADS):
        st = _mm_nt(k_dup[g].astype(BF16), _head_query_rows(qs, g))
        probs, sink_terms = [], []
        for hh in range(Q_PER_KV):
            head = g * Q_PER_KV + hh
            sh = st[:, hh * r:(hh + 1) * r] + bias_scr[head * KEY_PAD:(head + 1) * KEY_PAD, :]
            prev = sh[0:WINDOW] if mask_prev is None else sh[0:WINDOW] + mask_prev
            cur = sh[WINDOW:KEY_PAD]
            sink = sinks_ref[head:head + 1, :]
            m = jnp.maximum(jnp.maximum(jnp.max(prev, axis=0, keepdims=True),
                                        jnp.max(cur, axis=0, keepdims=True)), sink)
            probs.append(jnp.concatenate([jnp.exp(prev - m), jnp.exp(cur - m)], axis=0))
            sink_terms.append(jnp.exp(sink - m))
        pr = jnp.concatenate(probs, axis=1).astype(BF16)
        v_ext = jnp.concatenate([vt[g * HEAD_DIM:(g + 1) * HEAD_DIM], ones], axis=0).astype(BF16)
        ot = _mm(v_ext, pr)
        den = ot[HEAD_DIM:HEAD_DIM + 1, :] + jnp.concatenate(sink_terms, axis=1)
        on = ot[0:HEAD_DIM, :] * (1.0 / den)
        for jj in range(Q_PER_KV // 2):
            pair = jnp.concatenate([on[:, (2 * jj) * r:(2 * jj + 1) * r],
                                    on[:, (2 * jj + 1) * r:(2 * jj + 2) * r]], axis=0)
            outs.append(pair.T)
    return jnp.concatenate(outs, axis=1)


def _gated_group_norm(y, z, gain):
    y = y * _silu(z)
    gw = D_SSM // N_SSM_GROUPS
    parts = [_rmsnorm(y[:, g * gw:(g + 1) * gw], gain[:, g * gw:(g + 1) * gw]) for g in range(N_SSM_GROUPS)]
    return jnp.concatenate(parts, axis=1)


def _prompt_tile_mixer(first, z_ref, xbc_ref, xprev_ref, dt_ref, q_ref, k_ref, kprev_ref, v_ref, vprev_ref,
                       cw_ref, cb_ref, dtb_ref, alog_ref, dskip_ref, snorm_ref, sinks_ref, expand_ref,
                       st_scr, xe_scr, bias_scr, store_mix):
    t = SSD_CHUNK
    rows = xbc_ref.shape[0]
    hw = D_SSM // N_SSM_GROUPS
    nbc = N_SSM_GROUPS * D_STATE

    x = xbc_ref[...]
    xe_scr[0:SUBLANES, :] = jnp.where(first, 0.0, xprev_ref[...])
    xe_scr[SUBLANES:SUBLANES + rows, :] = x
    cw = cw_ref[...]
    conv = cb_ref[...] + cw[CONV_W - 1:CONV_W] * x
    for back in range(1, CONV_W):
        conv = conv + cw[CONV_W - 1 - back:CONV_W - back] * xe_scr[SUBLANES - back:SUBLANES - back + rows, :]
    xc = _silu(conv)

    dt_raw = _mm(jnp.concatenate(_split3(dt_ref[...] + dtb_ref[...]), axis=1), expand_ref[...])
    dt_all = _softplus(dt_raw)
    a_rep = -jnp.exp(alog_ref[...])
    causal = lax.broadcasted_iota(jnp.int32, (t, t), 0) >= lax.broadcasted_iota(jnp.int32, (t, t), 1)
    ltri = causal.astype(BF16)
    low_b = _lane_low_half((t, LANES))
    st = jnp.where(first, 0.0, st_scr[...])
    for c in range(rows // t):
        sl = slice(c * t, (c + 1) * t)
        xs = xc[sl, 0:D_SSM]
        bm = xc[sl, D_SSM:D_SSM + nbc]
        cm = xc[sl, D_SSM + nbc:CONV_DIM]
        dt = dt_all[sl]
        cum3 = _mm(ltri, jnp.concatenate(_split3(dt * a_rep), axis=1))
        cum = cum3[:, 0:D_SSM] + cum3[:, D_SSM:2 * D_SSM] + cum3[:, 2 * D_SSM:3 * D_SSM]
        cum_last = cum[t - 1:t, :]
        ecum = jnp.exp(cum)
        xdt = xs * dt
        xdt_b = xdt.astype(BF16)
        xw_b = (xdt * jnp.exp(cum_last - cum)).astype(BF16)
        st_b = st.astype(BF16)
        y_in, y_off, st_new = [], [], []
        for g in range(N_SSM_GROUPS):
            bg = bm[:, g * D_STATE:(g + 1) * D_STATE]
            cg_b = cm[:, g * D_STATE:(g + 1) * D_STATE].astype(BF16)
            cb_t = _mm_nt(cg_b, bg.astype(BF16))
            y_off.append(_mm(cg_b, st_b[:, g * hw:(g + 1) * hw]))
            st_new.append(_mm(bg.T.astype(BF16), xw_b[:, g * hw:(g + 1) * hw]))
            for jj in range(hw // LANES):
                j = g * (hw // LANES) + jj
                scores = []
                for ch in _split_pair(cum[:, j * LANES:(j + 1) * LANES]):
                    decay = jnp.exp(jnp.where(causal, ch - ch.T, -jnp.inf))
                    scores.append((cb_t * decay).astype(BF16))
                xp = xdt_b[:, j * LANES:(j + 1) * LANES]
                zero = jnp.zeros_like(xp)
                rhs = jnp.concatenate([jnp.where(low_b, xp, zero), jnp.where(low_b, zero, xp)], axis=0)
                y_in.append(_mm(jnp.concatenate(scores, axis=1), rhs))
        y = jnp.concatenate(y_in, axis=1) + jnp.concatenate(y_off, axis=1) * ecum + dskip_ref[...] * xs
        st = st * ecum[t - 1:t, :] + jnp.concatenate(st_new, axis=1)
        ssd_out = _gated_group_norm(y, z_ref[sl, :], snorm_ref[...])

        k_before = kprev_ref[...] if c == 0 else k_ref[(c - 1) * t:c * t, :]
        v_before = vprev_ref[...] if c == 0 else v_ref[(c - 1) * t:c * t, :]
        kk = jnp.concatenate([k_before, k_ref[sl, :]], axis=0)
        vv = jnp.concatenate([v_before, v_ref[sl, :]], axis=0)
        attn_out = _attention_cols(q_ref[sl, :], kk, vv, bias_scr, sinks_ref, first if c == 0 else None)
        store_mix(sl, ssd_out, attn_out)
    st_scr[...] = st
    return st


def _mixer_prompt_kernel(*refs, tiles_per_seq):
    mix_ref, state_ref, st_scr, xe_scr, bias_scr = refs[-5:]
    j = pl.program_id(0)

    @pl.when(j == 0)
    def _():
        _fill_attention_bias(bias_scr)

    def store_mix(rows, ssd_out, attn_out):
        mix_ref[rows, 0:D_SSM] = ssd_out.astype(mix_ref.dtype)
        mix_ref[rows, D_SSM:D_SSM + D_ATTN] = attn_out.astype(mix_ref.dtype)

    st = _prompt_tile_mixer(lax.rem(j, tiles_per_seq) == 0, *refs[:-5], st_scr, xe_scr, bias_scr, store_mix)

    @pl.when(lax.rem(j, tiles_per_seq) == tiles_per_seq - 1)
    def _():
        for jb in range(D_SSM // LANES):
            state_ref[jb * LANES:(jb + 1) * LANES, :] = st[:, jb * LANES:(jb + 1) * LANES].T


def _mixer_consts(pw):
    return (pw['conv_w'], pw['conv_b'], pw['dt_bias'], pw['a_log_rep'], pw['d_skip'], pw['ssm_norm'], pw['sinks'],
            pw['head_expand'])


def _mixer_prompt(z, xbc, dtr, q, k, v, pw, layer, batch, seq):
    t = SSD_CHUNK
    tile = MIXER_TILE
    tiles_per_seq = seq // tile
    cur = lambda width: pl.BlockSpec((tile, width), lambda i: (i, 0))
    before = lambda rows, width: pl.BlockSpec((rows, width), lambda i: (jnp.maximum(i * (tile // rows) - 1, 0), 0))
    consts = _mixer_consts(pw)
    return pl.pallas_call(
        functools.partial(_mixer_prompt_kernel, tiles_per_seq=tiles_per_seq),
        grid=(batch * tiles_per_seq,),
        in_specs=[cur(D_SSM), cur(CONV_DIM), before(SUBLANES, CONV_DIM), cur(LANES), cur(D_ATTN),
                  cur(KV_WIDTH), before(t, KV_WIDTH), cur(KV_WIDTH), before(t, KV_WIDTH)]
                 + [_layer_block(cst, layer) for cst in consts],
        out_specs=[pl.BlockSpec((tile, D_SSM + D_ATTN), lambda i: (i, 0)),
                   pl.BlockSpec((None, D_SSM, D_STATE), lambda i: (i // tiles_per_seq, 0, 0))],
        out_shape=[jax.ShapeDtypeStruct((batch * seq, D_SSM + D_ATTN), BF16),
                   jax.ShapeDtypeStruct((batch, D_SSM, D_STATE), F32)],
        scratch_shapes=[pltpu.VMEM((D_STATE, D_SSM), F32),
                        pltpu.VMEM((SUBLANES + tile, CONV_DIM), F32),
                        pltpu.VMEM((N_HEADS * KEY_PAD, WINDOW), F32)],
        compiler_params=pltpu.CompilerParams(dimension_semantics=("arbitrary",),
                                             vmem_limit_bytes=VMEM_LIMIT),
        name="mixer_prompt",
    )(z, xbc, xbc, dtr, q, k, k, v, v, *consts)


N_SAMPLE_IN = 18


def _put_lane_blocks(scr, col0, rows, val):
    for j in range(val.shape[1] // LANES):
        scr[col0 // LANES + j, rows, :] = val[:, j * LANES:(j + 1) * LANES]


def _get_lane_blocks(scr, col0, width, rows):
    return jnp.concatenate([scr[col0 // LANES + j, rows, :] for j in range(width // LANES)], axis=1)


def _mixer_sample_kernel(*refs):
    (z_ref, xbc_ref, dt_ref, q_ref, k_ref, v_ref, sst_ref, sconv_ref, ck_ref, cv_ref,
     cw_ref, cb_ref, dtb_ref, alog_ref, dskip_ref, snorm_ref, sinks_ref, expand_ref) = refs[:N_SAMPLE_IN]
    (mix_ref, sst_out_ref, ck_out_ref, cv_out_ref,
     ypart_scr, ecum_scr, cdec_scr, xw_scr, xt_scr, bmat_scr, seq_scr, res_scr, kk_scr, vv_scr) = refs[-14:]
    dl, bb = xbc_ref.shape[0], xbc_ref.shape[1]
    hw = D_SSM // N_SSM_GROUPS
    nbc = N_SSM_GROUPS * D_STATE
    c_q, c_c, c_k, c_v = 0, D_ATTN, D_ATTN + nbc, D_ATTN + nbc + KV_WIDTH
    xw_scr[...] = jnp.zeros_like(xw_scr)
    bmat_scr[...] = jnp.zeros_like(bmat_scr)
    seq_scr[...] = jnp.zeros_like(seq_scr)
    kk_scr[...] = jnp.zeros_like(kk_scr)
    vv_scr[...] = jnp.zeros_like(vv_scr)

    cw = cw_ref[...]
    a_rep = -jnp.exp(alog_ref[...])
    xs, bm, cm, xdt, cum = [], [], [], [], []
    for t in range(dl):
        conv = cb_ref[...] + cw[CONV_W - 1:CONV_W] * xbc_ref[t]
        for back in range(1, CONV_W):
            src = xbc_ref[t - back] if t >= back else sconv_ref[CONV_W - 1 + t - back]
            conv = conv + cw[CONV_W - 1 - back:CONV_W - back] * src
        xc = _silu(conv)
        xs.append(xc[:, 0:D_SSM])
        bm.append(xc[:, D_SSM:D_SSM + nbc])
        cm.append(xc[:, D_SSM + nbc:CONV_DIM])
        dt_c = _softplus(dt_ref[t] + dtb_ref[...])
        dt = _mm(jnp.concatenate(_split3(dt_c), axis=1), expand_ref[...])
        cum.append(dt * a_rep if t == 0 else cum[-1] + dt * a_rep)
        xdt.append(xs[t] * dt)
    cdec_scr[...] = jnp.exp(cum[dl - 1])
    for t in range(dl):
        y = dskip_ref[...] * xs[t]
        for s in range(t + 1):
            prod = cm[t] * bm[s]
            dots = jnp.concatenate(
                [jnp.broadcast_to(jnp.sum(prod[:, g * D_STATE:(g + 1) * D_STATE], axis=-1, keepdims=True), (bb, hw))
                 for g in range(N_SSM_GROUPS)], axis=1)
            y = y + (dots * xdt[s] if s == t else dots * jnp.exp(cum[t] - cum[s]) * xdt[s])
        rows_t = slice(t * bb, (t + 1) * bb)
        ypart_scr[rows_t, :] = y
        ecum_scr[rows_t, :] = jnp.exp(cum[t])
        xw_scr[rows_t, :] = xdt[t] * jnp.exp(cum[dl - 1] - cum[t])
        bmat_scr[rows_t, :] = bm[t]
        own_rows = pl.ds(t, bb, stride=SEQ_ROWS)
        _put_lane_blocks(seq_scr, c_q, own_rows, q_ref[t] * (HEAD_DIM ** -0.5))
        _put_lane_blocks(seq_scr, c_c, own_rows, cm[t])
        _put_lane_blocks(seq_scr, c_k, own_rows, k_ref[t])
        _put_lane_blocks(seq_scr, c_v, own_rows, v_ref[t])
    xw = xw_scr[...]
    for j in range(D_SSM // LANES):
        xt_scr[j * LANES:(j + 1) * LANES, :] = xw[:, j * LANES:(j + 1) * LANES].T.astype(BF16)

    block_row = lax.broadcasted_iota(jnp.int32, (SSD_CHUNK, D_STATE), 0)

    def one_sequence(b, slot):
        r0 = pl.multiple_of(b * SEQ_ROWS, SEQ_ROWS)
        my_rows = pl.ds(r0, SEQ_ROWS)
        state = sst_ref[b]
        state_b = state.astype(BF16)
        c_rows = jnp.concatenate([_get_lane_blocks(seq_scr, c_c, nbc, my_rows),
                                  jnp.zeros((SEQ_ROWS, nbc), F32)], axis=0).astype(BF16)
        y_off = [_mm_nt(c_rows[:, g * D_STATE:(g + 1) * D_STATE], state_b[g * hw:(g + 1) * hw, :])[0:SEQ_ROWS]
                 for g in range(N_SSM_GROUPS)]
        _put_lane_blocks(res_scr, 0, my_rows, jnp.concatenate(y_off, axis=1))
        is_mine = (block_row & (bb - 1)) == b
        decay = jnp.broadcast_to(cdec_scr[pl.ds(b, 1), :], (SUBLANES, D_SSM))
        for g in range(N_SSM_GROUPS):
            rhs = jnp.where(is_mine, bmat_scr[:, g * D_STATE:(g + 1) * D_STATE], 0.0).astype(BF16)
            upd = _mm(xt_scr[g * hw:(g + 1) * hw, :], rhs)
            for jj in range(hw // LANES):
                j = g * (hw // LANES) + jj
                for half, dec in enumerate(_split_pair(decay[:, j * LANES:(j + 1) * LANES])):
                    h0 = j * LANES + half * HEAD_DIM
                    u0 = jj * LANES + half * HEAD_DIM
                    sst_out_ref[b, h0:h0 + HEAD_DIM, :] = (
                        state[h0:h0 + HEAD_DIM] * jnp.broadcast_to(dec[0:1], (HEAD_DIM, D_STATE))
                        + upd[u0:u0 + HEAD_DIM])
        kk_scr[slot, 0:WINDOW, :] = ck_ref[b]
        kk_scr[slot, WINDOW:WINDOW + SEQ_ROWS, :] = _get_lane_blocks(seq_scr, c_k, KV_WIDTH, my_rows)
        vv_scr[slot, 0:WINDOW, :] = cv_ref[b]
        vv_scr[slot, WINDOW:WINDOW + SEQ_ROWS, :] = _get_lane_blocks(seq_scr, c_v, KV_WIDTH, my_rows)
        ck_out_ref[b] = kk_scr[slot, dl:dl + WINDOW, :]
        cv_out_ref[b] = vv_scr[slot, dl:dl + WINDOW, :]
        att = _attention_rows(_get_lane_blocks(seq_scr, c_q, D_ATTN, my_rows), kk_scr[slot], vv_scr[slot], sinks_ref)
        _put_lane_blocks(res_scr, D_SSM, my_rows, att)

    def per_group(i, carry):
        for slot in range(SAMPLE_UNROLL):
            one_sequence(i * SAMPLE_UNROLL + slot, slot)
        return carry

    lax.fori_loop(0, bb // SAMPLE_UNROLL, per_group, 0)

    for t in range(dl):
        own_rows = pl.ds(t, bb, stride=SEQ_ROWS)
        rows_t = slice(t * bb, (t + 1) * bb)
        y = ypart_scr[rows_t, :] + _get_lane_blocks(res_scr, 0, D_SSM, own_rows) * ecum_scr[rows_t, :]
        mix_ref[t, :, 0:D_SSM] = _gated_group_norm(y, z_ref[t], snorm_ref[...])
        mix_ref[t, :, D_SSM:D_SSM + D_ATTN] = _get_lane_blocks(res_scr, D_SSM, D_ATTN, own_rows)


def _mixer_sample(z, xbc, dtr, q, k, v, state_ssm, state_conv, cache_k, cache_v, pw, layer, carried):
    depth = state_ssm.shape[0]
    dl, nb = xbc.shape[0], xbc.shape[1]
    bb = min(SAMPLE_SEQS_PER_STEP, nb)
    assert bb & (bb - 1) == 0 and dl * bb <= SSD_CHUNK and dl <= SEQ_ROWS and bb % SAMPLE_UNROLL == 0
    tok = lambda width: pl.BlockSpec((dl, bb, width), lambda i: (0, i, 0))
    lblk = lambda *dims: pl.BlockSpec((None, bb) + dims, lambda i: (layer, i) + (0,) * len(dims))
    consts = _mixer_consts(pw)
    operands = (z, xbc, dtr, q, k, v, state_ssm, state_conv, cache_k, cache_v) + consts
    assert len(operands) == N_SAMPLE_IN
    seq_cols = D_ATTN + N_SSM_GROUPS * D_STATE + 2 * KV_WIDTH
    return pl.pallas_call(
        _mixer_sample_kernel,
        grid=(nb // bb,),
        in_specs=[tok(D_SSM), tok(CONV_DIM), tok(LANES), tok(D_ATTN), tok(KV_WIDTH), tok(KV_WIDTH),
                  lblk(D_SSM, D_STATE),
                  pl.BlockSpec((None, CONV_W - 1, bb, CONV_DIM), lambda i: (layer, 0, i, 0)),
                  lblk(WINDOW, KV_WIDTH), lblk(WINDOW, KV_WIDTH)]
                 + [_layer_block(cst, layer) for cst in consts]
                 + [pl.BlockSpec(memory_space=pl.ANY)] * len(carried),
        out_specs=[tok(D_SSM + D_ATTN), lblk(D_SSM, D_STATE), lblk(WINDOW, KV_WIDTH), lblk(WINDOW, KV_WIDTH)],
        out_shape=[jax.ShapeDtypeStruct((dl, nb, D_SSM + D_ATTN), F32),
                   jax.ShapeDtypeStruct((depth, nb, D_SSM, D_STATE), F32),
                   jax.ShapeDtypeStruct((depth, nb, WINDOW, KV_WIDTH), F32),
                   jax.ShapeDtypeStruct((depth, nb, WINDOW, KV_WIDTH), F32)],
        input_output_aliases={N_SAMPLE_IN + n: 1 + n for n in range(len(carried))},
        scratch_shapes=[pltpu.VMEM((dl * bb, D_SSM), F32),
                        pltpu.VMEM((dl * bb, D_SSM), F32),
                        pltpu.VMEM((bb, D_SSM), F32),
                        pltpu.VMEM((SSD_CHUNK, D_SSM), F32),
                        pltpu.VMEM((D_SSM, SSD_CHUNK), BF16),
                        pltpu.VMEM((SSD_CHUNK, N_SSM_GROUPS * D_STATE), F32),
                        pltpu.VMEM((seq_cols // LANES, bb * SEQ_ROWS, LANES), F32),
                        pltpu.VMEM(((D_SSM + D_ATTN) // LANES, bb * SEQ_ROWS, LANES), F32),
                        pltpu.VMEM((SAMPLE_UNROLL, KEY_PAD, KV_WIDTH), F32),
                        pltpu.VMEM((SAMPLE_UNROLL, KEY_PAD, KV_WIDTH), F32)],
        compiler_params=pltpu.CompilerParams(dimension_semantics=("arbitrary",),
                                             vmem_limit_bytes=VMEM_LIMIT),
        name="mixer_sample",
    )(*operands, *carried)


def _prepare_weights(p):
    bf = lambda w: w.astype(BF16)
    vec = lambda v: v[:, None, :]

    rep = lambda v: jnp.repeat(v, HEAD_DIM, axis=-1)[:, None, :]
    lane_pad = lambda v: jnp.pad(v, ((0, 0), (0, LANES - v.shape[-1])))[:, None, :]
    cuts = np.cumsum([D_SSM, CONV_DIM, N_SSM_HEADS, D_ATTN, KV_WIDTH])
    wz, wxbc, wdt, wq, wk, wv = jnp.split(bf(p['w_in']), cuts, axis=2)
    depth = p['w_in'].shape[0]
    seg = np.kron(np.eye(D_ATTN // HEAD_DIM, dtype=np.float32), np.ones((HEAD_DIM, HEAD_DIM), np.float32))
    heads = np.eye(LANES, dtype=np.float32)[:, :N_SSM_HEADS]
    expand = np.tile(np.kron(heads, np.ones((1, HEAD_DIM), np.float32)), (3, 1))
    return {
        'g_ffn1': vec(p['g_ffn1']), 'g_mix': vec(p['g_mix']), 'g_ffn2': vec(p['g_ffn2']), 'g_ple': vec(p['g_ple']),
        'w1_a': bf(p['w1_a']), 'w3_a': bf(p['w3_a']), 'w2_a': bf(p['w2_a']),
        'w1_b': bf(p['w1_b']), 'w3_b': bf(p['w3_b']), 'w2_b': bf(p['w2_b']),
        'w_in': jnp.concatenate(
            [wz, wxbc, wq, wk, wv, jnp.pad(wdt, ((0, 0), (0, 0), (0, LANES - N_SSM_HEADS)))], axis=2),
        'w_out': bf(p['w_out']), 'w_ple_gate': bf(p['w_ple_gate']), 'w_ple_proj': bf(p['w_ple_proj']),
        'q_gain': vec(jnp.tile(p['q_norm'], (1, N_HEADS))),
        'k_gain': vec(jnp.tile(p['k_norm'], (1, N_KV_HEADS))),
        'seg': jnp.broadcast_to(jnp.asarray(seg, BF16), (depth,) + seg.shape),
        'conv_w': p['conv_w'], 'conv_b': vec(p['conv_b']),
        'dt_bias': lane_pad(p['dt_bias']), 'a_log_rep': rep(p['a_log']),
        'd_skip': rep(p['d_skip']),
        'head_expand': jnp.broadcast_to(jnp.asarray(expand, BF16), (depth,) + expand.shape),
        'ssm_norm': vec(p['ssm_norm']),
        'sinks': jnp.broadcast_to(p['sinks'][:, :, None], (depth, N_HEADS, LANES)),
    }


def kernel(x_prompt, x_sample, state_ssm, state_conv, cache_k_win, cache_v_win, p_prompt, p_sample, g_ffn1, w1_a, w3_a, w2_a, g_mix, w_in, conv_w, conv_b, dt_bias, a_log, d_skip, ssm_norm, q_norm, k_norm, sinks, w_out, g_ffn2, w1_b, w3_b, w2_b, g_ple, w_ple_gate, w_ple_proj):
    params = dict(g_ffn1=g_ffn1, w1_a=w1_a, w3_a=w3_a, w2_a=w2_a, g_mix=g_mix, w_in=w_in, conv_w=conv_w,
                  conv_b=conv_b, dt_bias=dt_bias, a_log=a_log, d_skip=d_skip, ssm_norm=ssm_norm, q_norm=q_norm,
                  k_norm=k_norm, sinks=sinks, w_out=w_out, g_ffn2=g_ffn2, w1_b=w1_b, w3_b=w3_b, w2_b=w2_b,
                  g_ple=g_ple, w_ple_gate=w_ple_gate, w_ple_proj=w_ple_proj)
    depth = w_in.shape[0]
    bp, seq, _ = x_prompt.shape
    bs, dl, _ = x_sample.shape
    assert seq % SSD_CHUNK == 0 and seq >= WINDOW and dl >= CONV_W - 1 and dl <= SUBLANES
    assert (bp * seq) % TOKEN_TILE == 0 and (bs * dl) % min(TOKEN_TILE, bs * dl) == 0

    pw = _prepare_weights(params)
    sst_in = state_ssm.reshape(depth, bs, D_SSM, D_STATE)
    ck_in = cache_k_win.reshape(depth, bs, WINDOW, KV_WIDTH)
    cv_in = cache_v_win.reshape(depth, bs, WINDOW, KV_WIDTH)
    sconv_in = jnp.swapaxes(state_conv, 1, 2)
    pe_p = p_prompt.reshape(depth, bp * seq, D_PLE)
    pe_s = jnp.swapaxes(p_sample, 1, 2).reshape(depth, dl * bs, D_PLE)
    hp = x_prompt.reshape(bp * seq, D_MODEL)
    hs = jnp.swapaxes(x_sample, 0, 1).reshape(dl * bs, D_MODEL)
    ssm_p, conv_p, k_p, v_p, conv_s = [], [], [], [], []
    carried = ()
    for l in range(depth):
        hp, zp, xbcp, dtp, qp, kp, vp = _stage_a(hp, pw, l)
        hs, zs, xbcs, dts, qs, ks, vs = _stage_a(hs, pw, l)
        mixp, sp = _mixer_prompt(zp, xbcp, dtp, qp, kp, vp, pw, l, bp, seq)
        hp = _stage_c(hp, mixp, pe_p, pw, l)
        tm3 = lambda a: a.reshape(dl, bs, a.shape[-1])
        mixs, *carried = _mixer_sample(tm3(zs), tm3(xbcs), tm3(dts), tm3(qs), tm3(ks), tm3(vs),
                                       sst_in, sconv_in, ck_in, cv_in, pw, l, tuple(carried))
        hs = _stage_c(hs, mixs.reshape(dl * bs, D_MODEL), pe_s, pw, l)
        ssm_p.append(sp.reshape(bp, N_SSM_HEADS, HEAD_DIM, D_STATE))
        conv_p.append(xbcp.reshape(bp, seq, CONV_DIM)[:, seq - (CONV_W - 1):])
        last_window = lambda a: a.reshape(bp, seq, KV_WIDTH)[:, seq - WINDOW:].reshape(
            bp, WINDOW, N_KV_HEADS, HEAD_DIM)
        k_p.append(last_window(kp))
        v_p.append(last_window(vp))
        conv_s.append(jnp.swapaxes(tm3(xbcs)[dl - (CONV_W - 1):], 0, 1))
    ssm_s, k_s, v_s = carried
    return (hp.reshape(bp, seq, D_MODEL), jnp.swapaxes(hs.reshape(dl, bs, D_MODEL), 0, 1),
            jnp.stack(ssm_p), jnp.stack(conv_p), jnp.stack(k_p), jnp.stack(v_p),
            ssm_s.reshape(depth, bs, N_SSM_HEADS, HEAD_DIM, D_STATE), jnp.stack(conv_s),
            k_s.reshape(depth, bs, WINDOW, N_KV_HEADS, HEAD_DIM),
            v_s.reshape(depth, bs, WINDOW, N_KV_HEADS, HEAD_DIM))
```

```python
import functools

import numpy as np
import jax
import jax.numpy as jnp
from jax import lax
from jax.experimental import pallas as pl
from jax.experimental.pallas import tpu as pltpu

F32 = jnp.float32
BF16 = jnp.bfloat16

D_MODEL = 1024
HEAD_DIM = 64
D_SSM = 512
N_SSM_HEADS = 8
N_SSM_GROUPS = 2
D_STATE = 128
CONV_W = 4
CONV_DIM = D_SSM + 2 * N_SSM_GROUPS * D_STATE
SSD_CHUNK = 128
D_ATTN = 512
N_HEADS = 8
N_KV_HEADS = 2
Q_PER_KV = N_HEADS // N_KV_HEADS
KV_WIDTH = N_KV_HEADS * HEAD_DIM
WINDOW = 128
D_FF = 2752
D_PLE = 256
RMS_EPS = 1e-6

LANES = 128
SUBLANES = 8
MXU_DIM = 256
FF_CHUNK = MXU_DIM
FF_FULL_CHUNKS = D_FF // FF_CHUNK
FF_TAIL = D_FF - FF_FULL_CHUNKS * FF_CHUNK
FF_TAIL_PAD = -(-FF_TAIL // MXU_DIM) * MXU_DIM
TOKEN_TILE = 512
MIXER_TILE = 512
SAMPLE_SEQS_PER_STEP = 16
SAMPLE_UNROLL = 4
SEQ_ROWS = SUBLANES
KEY_PAD = 2 * WINDOW
VMEM_LIMIT = 56 * 1024 * 1024

OFF_Z = 0
OFF_XBC = OFF_Z + D_SSM
OFF_Q = OFF_XBC + CONV_DIM
OFF_K = OFF_Q + D_ATTN
OFF_V = OFF_K + KV_WIDTH
OFF_DT = OFF_V + KV_WIDTH
D_PROJ_PAD = OFF_DT + LANES

ALIBI_SLOPES = tuple(float(s) for s in np.power(
    np.float32(2.0), -8.0 * np.arange(1, N_HEADS + 1, dtype=np.float32) / N_HEADS))

_NT = (((1,), (1,)), ((), ()))


def _mm(a, b):
    return jnp.dot(a, b, preferred_element_type=F32)


def _mm_nt(a, b):
    return lax.dot_general(a, b, _NT, preferred_element_type=F32)


def _sigmoid(x):
    return 1.0 / (1.0 + jnp.exp(-x))


def _silu(x):
    return x * _sigmoid(x)


def _softplus(x):
    return jnp.maximum(x, 0.0) + jnp.log(1.0 + jnp.exp(-jnp.abs(x)))


def _rmsnorm(x, g):
    return x * lax.rsqrt(jnp.mean(x * x, axis=-1, keepdims=True) + RMS_EPS) * g


def _split3(x):
    hi = x.astype(BF16)
    rest = x - hi.astype(F32)
    mid = rest.astype(BF16)
    return hi, mid, (rest - mid.astype(F32)).astype(BF16)


def _stage_ff_tail(w1_ref, w3_ref, w2_ref, tail_refs):
    w1t_ref, w3t_ref, w2t_ref = tail_refs
    lo = FF_FULL_CHUNKS * FF_CHUNK
    for dst, src in ((w1t_ref, w1_ref), (w3t_ref, w3_ref)):
        dst[...] = jnp.zeros_like(dst)
        dst[:, 0:FF_TAIL] = src[:, lo:D_FF]
    w2t_ref[...] = jnp.zeros_like(w2t_ref)
    w2t_ref[0:FF_TAIL, :] = w2_ref[lo:D_FF, :]


def _ff_tail_scratch():
    return [pltpu.VMEM((D_MODEL, FF_TAIL_PAD), BF16), pltpu.VMEM((D_MODEL, FF_TAIL_PAD), BF16),
            pltpu.VMEM((FF_TAIL_PAD, D_MODEL), BF16)]


def _swiglu(xn, w1_ref, w3_ref, w2_ref, tail_refs):
    def chunk(acc, w1c, w3c, w2c):
        a = _mm(xn, w1c)
        b = _mm(xn, w3c)
        return acc + _mm((_silu(a) * b).astype(BF16), w2c)

    acc = jnp.zeros((xn.shape[0], D_MODEL), F32)
    for c in range(FF_FULL_CHUNKS):
        sl = slice(c * FF_CHUNK, (c + 1) * FF_CHUNK)
        acc = chunk(acc, w1_ref[:, sl], w3_ref[:, sl], w2_ref[sl, :])
    if FF_TAIL:
        acc = chunk(acc, *(r[...] for r in tail_refs))
    return acc


def _segment_sumsq(x, seg_ref):
    hi, mid, _ = _split3(x * x)
    seg = seg_ref[0:x.shape[1], 0:x.shape[1]]
    return _mm(hi, seg) + _mm(mid, seg)


def _stage_a_kernel(x_ref, g1_ref, w1_ref, w3_ref, w2_ref, gm_ref, win_ref, qg_ref, kg_ref, seg_ref,
                    h_ref, z_ref, xbc_ref, dt_ref, q_ref, k_ref, v_ref, *tail_refs):
    @pl.when(pl.program_id(0) == 0)
    def _():
        _stage_ff_tail(w1_ref, w3_ref, w2_ref, tail_refs)

    x = x_ref[...]
    h = x + 0.5 * _swiglu(_rmsnorm(x, g1_ref[...]).astype(BF16), w1_ref, w3_ref, w2_ref, tail_refs)
    h_ref[...] = h
    proj = _mm(_rmsnorm(h, gm_ref[...]).astype(BF16), win_ref[...])
    z_ref[...] = proj[:, OFF_Z:OFF_XBC]
    xbc_ref[...] = proj[:, OFF_XBC:OFF_Q]
    dt_ref[...] = proj[:, OFF_DT:D_PROJ_PAD]
    q = proj[:, OFF_Q:OFF_K]
    k = proj[:, OFF_K:OFF_V]
    q_ref[...] = q * lax.rsqrt(_segment_sumsq(q, seg_ref) * (1.0 / HEAD_DIM) + RMS_EPS) * qg_ref[...]
    k_ref[...] = k * lax.rsqrt(_segment_sumsq(k, seg_ref) * (1.0 / HEAD_DIM) + RMS_EPS) * kg_ref[...]
    v_ref[...] = proj[:, OFF_V:OFF_DT]


def _layer_block(arr, layer, **kwargs):
    return pl.BlockSpec((None,) + arr.shape[1:], lambda *_: (layer, 0, 0), **kwargs)


def _layer_resident(arr, layer):
    return _layer_block(arr, layer, pipeline_mode=pl.Buffered(1))


def _stage_a(x, pw, layer):
    n = x.shape[0]
    tm = min(TOKEN_TILE, n)
    row = lambda width: pl.BlockSpec((tm, width), lambda i: (i, 0))
    widths = (D_MODEL, D_SSM, CONV_DIM, LANES, D_ATTN, KV_WIDTH, KV_WIDTH)
    consts = (pw['g_ffn1'], pw['w1_a'], pw['w3_a'], pw['w2_a'], pw['g_mix'], pw['w_in'],
              pw['q_gain'], pw['k_gain'], pw['seg'])
    return pl.pallas_call(
        _stage_a_kernel,
        grid=(n // tm,),
        in_specs=[row(D_MODEL)] + [_layer_resident(c, layer) for c in consts],
        out_specs=[row(w) for w in widths],
        out_shape=[jax.ShapeDtypeStruct((n, w), F32) for w in widths],
        scratch_shapes=_ff_tail_scratch(),
        compiler_params=pltpu.CompilerParams(dimension_semantics=("arbitrary",),
                                             vmem_limit_bytes=VMEM_LIMIT),
        name="stage_a",
    )(x, *consts)


def _stage_c_kernel(h_ref, mix_ref, pe_ref, wout_ref, g2_ref, w1_ref, w3_ref, w2_ref, gp_ref,
                    wgate_ref, wproj_ref, o_ref, *tail_refs):
    @pl.when(pl.program_id(0) == 0)
    def _():
        _stage_ff_tail(w1_ref, w3_ref, w2_ref, tail_refs)

    o_ref[...] = _stage_c_compute(h_ref[...], mix_ref[...].astype(BF16), pe_ref, wout_ref, g2_ref, w1_ref,
                                  w3_ref, w2_ref, gp_ref, wgate_ref, wproj_ref, tail_refs)


def _stage_c_compute(h, mix, pe_ref, wout_ref, g2_ref, w1_ref, w3_ref, w2_ref, gp_ref, wgate_ref, wproj_ref,
                     tail_refs):
    h = h + _mm(mix, wout_ref[...])
    h = h + 0.5 * _swiglu(_rmsnorm(h, g2_ref[...]).astype(BF16), w1_ref, w3_ref, w2_ref, tail_refs)
    gate = _sigmoid(_mm(_rmsnorm(h, gp_ref[...]).astype(BF16), wgate_ref[...]))
    return h + gate * _mm(pe_ref[...].astype(BF16), wproj_ref[...])


def _stage_c(h, mix, pe, pw, layer):
    n = h.shape[0]
    tm = min(TOKEN_TILE, n)
    row = lambda width: pl.BlockSpec((tm, width), lambda i: (i, 0))
    consts = (pw['w_out'], pw['g_ffn2'], pw['w1_b'], pw['w3_b'], pw['w2_b'], pw['g_ple'],
              pw['w_ple_gate'], pw['w_ple_proj'])
    return pl.pallas_call(
        _stage_c_kernel,
        grid=(n // tm,),
        in_specs=[row(D_MODEL), row(D_MODEL), pl.BlockSpec((None, tm, D_PLE), lambda i: (layer, i, 0))]
                 + [_layer_resident(c, layer) for c in consts],
        out_specs=row(D_MODEL),
        out_shape=jax.ShapeDtypeStruct((n, D_MODEL), F32),
        scratch_shapes=_ff_tail_scratch(),
        compiler_params=pltpu.CompilerParams(dimension_semantics=("arbitrary",),
                                             vmem_limit_bytes=VMEM_LIMIT),
        name="stage_c",
    )(h, mix, pe, *consts)


def _lane_low_half(shape):
    return lax.broadcasted_iota(jnp.int32, shape, len(shape) - 1) < HEAD_DIM


def _split_pair(x):
    rolled = pltpu.roll(x, HEAD_DIM, 1)
    low = _lane_low_half(x.shape)
    return jnp.where(low, x, rolled), jnp.where(low, rolled, x)


def _head_query_rows(q, g):
    low = _lane_low_half((q.shape[0], LANES))
    blocks = []
    for jj in range(Q_PER_KV // 2):
        qp = q[:, (g * 2 + jj) * LANES:(g * 2 + jj + 1) * LANES]
        blocks += [jnp.where(low, qp, 0.0), jnp.where(low, 0.0, qp)]
    return jnp.concatenate(blocks, axis=0).astype(BF16)


def _attention_rows_scores(q, kk, vv):
    low_k = _lane_low_half((KEY_PAD, LANES))
    k_dup = _split_pair(kk)
    v_roll = pltpu.roll(vv, HEAD_DIM, 1)
    v_even = (jnp.where(low_k, vv, 1.0).astype(BF16), jnp.where(low_k, v_roll, 1.0).astype(BF16))
    v_odd = (jnp.where(low_k, 1.0, v_roll).astype(BF16), jnp.where(low_k, 1.0, vv).astype(BF16))
    scores = [_mm_nt(_head_query_rows(q, g), k_dup[g].astype(BF16)) for g in range(N_KV_HEADS)]
    return scores, (v_even, v_odd)


def _attention_rows_probs(scores, sinks_ref):
    r = scores[0].shape[0] // Q_PER_KV
    t_i = lax.broadcasted_iota(jnp.int32, (r, KEY_PAD), 0)
    s_i = lax.broadcasted_iota(jnp.int32, (r, KEY_PAD), 1)
    rel = t_i + WINDOW - s_i
    valid = (rel >= 0) & (rel < WINDOW)
    relf = rel.astype(F32)
    probs, sink_terms = [], []
    for head in range(N_HEADS):
        g, hh = divmod(head, Q_PER_KV)
        sh = jnp.where(valid, scores[g][hh * r:(hh + 1) * r] - ALIBI_SLOPES[head] * relf, -jnp.inf)
        sink = sinks_ref[head:head + 1, 0:1]
        m = jnp.maximum(jnp.max(sh, axis=-1, keepdims=True), sink)
        probs.append(jnp.exp(sh - m))
        sink_terms.append(jnp.broadcast_to(jnp.exp(sink - m), (r, LANES)))
    return probs, sink_terms


def _attention_rows_output(probs, sink_terms, v_operands):
    v_even, v_odd = v_operands
    r = probs[0].shape[0]
    low = _lane_low_half((r, LANES))
    outs = []
    for g in range(N_KV_HEADS):
        mine = probs[g * Q_PER_KV:(g + 1) * Q_PER_KV]
        o_even = _mm(jnp.concatenate(mine[0::2], axis=0).astype(BF16), v_even[g])
        o_odd = _mm(jnp.concatenate(mine[1::2], axis=0).astype(BF16), v_odd[g])
        for jj in range(Q_PER_KV // 2):
            oe, oo = o_even[jj * r:(jj + 1) * r], o_odd[jj * r:(jj + 1) * r]
            head = g * Q_PER_KV + 2 * jj
            den = (pltpu.roll(jnp.where(low, oo, oe), HEAD_DIM, 1)
                   + jnp.where(low, sink_terms[head], sink_terms[head + 1]))
            outs.append(jnp.where(low, oe, oo) / den)
    return jnp.concatenate(outs, axis=1)


def _fill_attention_bias(bias_scr):
    s_i = lax.broadcasted_iota(jnp.int32, (KEY_PAD, WINDOW), 0)
    t_i = lax.broadcasted_iota(jnp.int32, (KEY_PAD, WINDOW), 1)
    rel = t_i + WINDOW - s_i
    valid = (rel >= 0) & (rel < WINDOW)
    relf = rel.astype(F32)
    for head in range(N_HEADS):
        bias_scr[head * KEY_PAD:(head + 1) * KEY_PAD, :] = jnp.where(valid, -ALIBI_SLOPES[head] * relf, -jnp.inf)


def _attention_cols(q, kk, vv, bias_scr, sinks_ref, first_block):
    r = q.shape[0]
    k_dup = _split_pair(kk)
    vt = vv.T
    ones = jnp.ones((HEAD_DIM, KEY_PAD), F32)
    mask_prev = None if first_block is None else jnp.where(first_block, -jnp.inf, 0.0)
    qs = q * (HEAD_DIM ** -0.5)
    outs = []
    for g in range(N_KV_HEADS):
        st = _mm_nt(k_dup[g].astype(BF16), _head_query_rows(qs, g))
        probs, sink_terms = [], []
        for hh in range(Q_PER_KV):
            head = g * Q_PER_KV + hh
            sh = st[:, hh * r:(hh + 1) * r] + bias_scr[head * KEY_PAD:(head + 1) * KEY_PAD, :]
            prev = sh[0:WINDOW] if mask_prev is None else sh[0:WINDOW] + mask_prev
            cur = sh[WINDOW:KEY_PAD]
            sink = sinks_ref[head:head + 1, :]
            m = jnp.maximum(jnp.maximum(jnp.max(prev, axis=0, keepdims=True),
                                        jnp.max(cur, axis=0, keepdims=True)), sink)
            probs.append(jnp.concatenate([jnp.exp(prev - m), jnp.exp(cur - m)], axis=0))
            sink_terms.append(jnp.exp(sink - m))
        pr = jnp.concatenate(probs, axis=1).astype(BF16)
        v_ext = jnp.concatenate([vt[g * HEAD_DIM:(g + 1) * HEAD_DIM], ones], axis=0).astype(BF16)
        ot = _mm(v_ext, pr)
        den = ot[HEAD_DIM:HEAD_DIM + 1, :] + jnp.concatenate(sink_terms, axis=1)
        on = ot[0:HEAD_DIM, :] * (1.0 / den)
        for jj in range(Q_PER_KV // 2):
            pair = jnp.concatenate([on[:, (2 * jj) * r:(2 * jj + 1) * r],
                                    on[:, (2 * jj + 1) * r:(2 * jj + 2) * r]], axis=0)
            outs.append(pair.T)
    return jnp.concatenate(outs, axis=1)


def _gated_group_norm(y, z, gain):
    y = y * _silu(z)
    gw = D_SSM // N_SSM_GROUPS
    parts = [_rmsnorm(y[:, g * gw:(g + 1) * gw], gain[:, g * gw:(g + 1) * gw]) for g in range(N_SSM_GROUPS)]
    return jnp.concatenate(parts, axis=1)


def _prompt_tile_mixer(first, z_ref, xbc_ref, xprev_ref, dt_ref, q_ref, k_ref, kprev_ref, v_ref, vprev_ref,
                       cw_ref, cb_ref, dtb_ref, alog_ref, dskip_ref, snorm_ref, sinks_ref, expand_ref,
                       st_scr, xe_scr, bias_scr, store_mix):
    t = SSD_CHUNK
    rows = xbc_ref.shape[0]
    hw = D_SSM // N_SSM_GROUPS
    nbc = N_SSM_GROUPS * D_STATE

    x = xbc_ref[...]
    xe_scr[0:SUBLANES, :] = jnp.where(first, 0.0, xprev_ref[...])
    xe_scr[SUBLANES:SUBLANES + rows, :] = x
    cw = cw_ref[...]
    conv = cb_ref[...] + cw[CONV_W - 1:CONV_W] * x
    for back in range(1, CONV_W):
        conv = conv + cw[CONV_W - 1 - back:CONV_W - back] * xe_scr[SUBLANES - back:SUBLANES - back + rows, :]
    xc = _silu(conv)

    dt_raw = _mm(jnp.concatenate(_split3(dt_ref[...] + dtb_ref[...]), axis=1), expand_ref[...])
    dt_all = _softplus(dt_raw)
    a_rep = -jnp.exp(alog_ref[...])
    causal = lax.broadcasted_iota(jnp.int32, (t, t), 0) >= lax.broadcasted_iota(jnp.int32, (t, t), 1)
    ltri = causal.astype(BF16)
    low_b = _lane_low_half((t, LANES))
    st = jnp.where(first, 0.0, st_scr[...])
    for c in range(rows // t):
        sl = slice(c * t, (c + 1) * t)
        xs = xc[sl, 0:D_SSM]
        bm = xc[sl, D_SSM:D_SSM + nbc]
        cm = xc[sl, D_SSM + nbc:CONV_DIM]
        dt = dt_all[sl]
        cum3 = _mm(ltri, jnp.concatenate(_split3(dt * a_rep), axis=1))
        cum = cum3[:, 0:D_SSM] + cum3[:, D_SSM:2 * D_SSM] + cum3[:, 2 * D_SSM:3 * D_SSM]
        cum_last = cum[t - 1:t, :]
        ecum = jnp.exp(cum)
        xdt = xs * dt
        xdt_b = xdt.astype(BF16)
        xw_b = (xdt * jnp.exp(cum_last - cum)).astype(BF16)
        st_b = st.astype(BF16)
        y_in, y_off, st_new = [], [], []
        for g in range(N_SSM_GROUPS):
            bg = bm[:, g * D_STATE:(g + 1) * D_STATE]
            cg_b = cm[:, g * D_STATE:(g + 1) * D_STATE].astype(BF16)
            cb_t = _mm_nt(cg_b, bg.astype(BF16))
            y_off.append(_mm(cg_b, st_b[:, g * hw:(g + 1) * hw]))
            st_new.append(_mm(bg.T.astype(BF16), xw_b[:, g * hw:(g + 1) * hw]))
            for jj in range(hw // LANES):
                j = g * (hw // LANES) + jj
                scores = []
                for ch in _split_pair(cum[:, j * LANES:(j + 1) * LANES]):
                    decay = jnp.exp(jnp.where(causal, ch - ch.T, -jnp.inf))
                    scores.append((cb_t * decay).astype(BF16))
                xp = xdt_b[:, j * LANES:(j + 1) * LANES]
                zero = jnp.zeros_like(xp)
                rhs = jnp.concatenate([jnp.where(low_b, xp, zero), jnp.where(low_b, zero, xp)], axis=0)
                y_in.append(_mm(jnp.concatenate(scores, axis=1), rhs))
        y = jnp.concatenate(y_in, axis=1) + jnp.concatenate(y_off, axis=1) * ecum + dskip_ref[...] * xs
        st = st * ecum[t - 1:t, :] + jnp.concatenate(st_new, axis=1)
        ssd_out = _gated_group_norm(y, z_ref[sl, :], snorm_ref[...])

        k_before = kprev_ref[...] if c == 0 else k_ref[(c - 1) * t:c * t, :]
        v_before = vprev_ref[...] if c == 0 else v_ref[(c - 1) * t:c * t, :]
        kk = jnp.concatenate([k_before, k_ref[sl, :]], axis=0)
        vv = jnp.concatenate([v_before, v_ref[sl, :]], axis=0)
        attn_out = _attention_cols(q_ref[sl, :], kk, vv, bias_scr, sinks_ref, first if c == 0 else None)
        store_mix(sl, ssd_out, attn_out)
    st_scr[...] = st
    return st


def _mixer_prompt_kernel(*refs, tiles_per_seq):
    mix_ref, state_ref, st_scr, xe_scr, bias_scr = refs[-5:]
    j = pl.program_id(0)

    @pl.when(j == 0)
    def _():
        _fill_attention_bias(bias_scr)

    def store_mix(rows, ssd_out, attn_out):
        mix_ref[rows, 0:D_SSM] = ssd_out.astype(mix_ref.dtype)
        mix_ref[rows, D_SSM:D_SSM + D_ATTN] = attn_out.astype(mix_ref.dtype)

    st = _prompt_tile_mixer(lax.rem(j, tiles_per_seq) == 0, *refs[:-5], st_scr, xe_scr, bias_scr, store_mix)

    @pl.when(lax.rem(j, tiles_per_seq) == tiles_per_seq - 1)
    def _():
        for jb in range(D_SSM // LANES):
            state_ref[jb * LANES:(jb + 1) * LANES, :] = st[:, jb * LANES:(jb + 1) * LANES].T


def _mixer_consts(pw):
    return (pw['conv_w'], pw['conv_b'], pw['dt_bias'], pw['a_log_rep'], pw['d_skip'], pw['ssm_norm'], pw['sinks'],
            pw['head_expand'])


def _mixer_prompt(z, xbc, dtr, q, k, v, pw, layer, batch, seq):
    t = SSD_CHUNK
    tile = MIXER_TILE
    tiles_per_seq = seq // tile
    cur = lambda width: pl.BlockSpec((tile, width), lambda i: (i, 0))
    before = lambda rows, width: pl.BlockSpec((rows, width), lambda i: (jnp.maximum(i * (tile // rows) - 1, 0), 0))
    consts = _mixer_consts(pw)
    return pl.pallas_call(
        functools.partial(_mixer_prompt_kernel, tiles_per_seq=tiles_per_seq),
        grid=(batch * tiles_per_seq,),
        in_specs=[cur(D_SSM), cur(CONV_DIM), before(SUBLANES, CONV_DIM), cur(LANES), cur(D_ATTN),
                  cur(KV_WIDTH), before(t, KV_WIDTH), cur(KV_WIDTH), before(t, KV_WIDTH)]
                 + [_layer_block(cst, layer) for cst in consts],
        out_specs=[pl.BlockSpec((tile, D_SSM + D_ATTN), lambda i: (i, 0)),
                   pl.BlockSpec((None, D_SSM, D_STATE), lambda i: (i // tiles_per_seq, 0, 0))],
        out_shape=[jax.ShapeDtypeStruct((batch * seq, D_SSM + D_ATTN), BF16),
                   jax.ShapeDtypeStruct((batch, D_SSM, D_STATE), F32)],
        scratch_shapes=[pltpu.VMEM((D_STATE, D_SSM), F32),
                        pltpu.VMEM((SUBLANES + tile, CONV_DIM), F32),
                        pltpu.VMEM((N_HEADS * KEY_PAD, WINDOW), F32)],
        compiler_params=pltpu.CompilerParams(dimension_semantics=("arbitrary",),
                                             vmem_limit_bytes=VMEM_LIMIT),
        name="mixer_prompt",
    )(z, xbc, xbc, dtr, q, k, k, v, v, *consts)


N_SAMPLE_IN = 18


def _put_lane_blocks(scr, col0, rows, val):
    for j in range(val.shape[1] // LANES):
        scr[col0 // LANES + j, rows, :] = val[:, j * LANES:(j + 1) * LANES]


def _get_lane_blocks(scr, col0, width, rows):
    return jnp.concatenate([scr[col0 // LANES + j, rows, :] for j in range(width // LANES)], axis=1)


def _mixer_sample_kernel(*refs):
    (z_ref, xbc_ref, dt_ref, q_ref, k_ref, v_ref, sst_ref, sconv_ref, ck_ref, cv_ref,
     cw_ref, cb_ref, dtb_ref, alog_ref, dskip_ref, snorm_ref, sinks_ref, expand_ref) = refs[:N_SAMPLE_IN]
    (mix_ref, sst_out_ref, ck_out_ref, cv_out_ref,
     ypart_scr, ecum_scr, cdec_scr, xw_scr, xt_scr, bmat_scr, seq_scr, res_scr, kk_scr, vv_scr) = refs[-14:]
    dl, bb = xbc_ref.shape[0], xbc_ref.shape[1]
    hw = D_SSM // N_SSM_GROUPS
    nbc = N_SSM_GROUPS * D_STATE
    c_q, c_c, c_k, c_v = 0, D_ATTN, D_ATTN + nbc, D_ATTN + nbc + KV_WIDTH
    xw_scr[...] = jnp.zeros_like(xw_scr)
    bmat_scr[...] = jnp.zeros_like(bmat_scr)
    seq_scr[...] = jnp.zeros_like(seq_scr)
    kk_scr[...] = jnp.zeros_like(kk_scr)
    vv_scr[...] = jnp.zeros_like(vv_scr)

    cw = cw_ref[...]
    a_rep = -jnp.exp(alog_ref[...])
    xs, bm, cm, xdt, cum = [], [], [], [], []
    for t in range(dl):
        conv = cb_ref[...] + cw[CONV_W - 1:CONV_W] * xbc_ref[t]
        for back in range(1, CONV_W):
            src = xbc_ref[t - back] if t >= back else sconv_ref[CONV_W - 1 + t - back]
            conv = conv + cw[CONV_W - 1 - back:CONV_W - back] * src
        xc = _silu(conv)
        xs.append(xc[:, 0:D_SSM])
        bm.append(xc[:, D_SSM:D_SSM + nbc])
        cm.append(xc[:, D_SSM + nbc:CONV_DIM])
        dt_c = _softplus(dt_ref[t] + dtb_ref[...])
        dt = _mm(jnp.concatenate(_split3(dt_c), axis=1), expand_ref[...])
        cum.append(dt * a_rep if t == 0 else cum[-1] + dt * a_rep)
        xdt.append(xs[t] * dt)
    cdec_scr[...] = jnp.exp(cum[dl - 1])
    for t in range(dl):
        y = dskip_ref[...] * xs[t]
        for s in range(t + 1):
            prod = cm[t] * bm[s]
            dots = jnp.concatenate(
                [jnp.broadcast_to(jnp.sum(prod[:, g * D_STATE:(g + 1) * D_STATE], axis=-1, keepdims=True), (bb, hw))
                 for g in range(N_SSM_GROUPS)], axis=1)
            y = y + (dots * xdt[s] if s == t else dots * jnp.exp(cum[t] - cum[s]) * xdt[s])
        rows_t = slice(t * bb, (t + 1) * bb)
        ypart_scr[rows_t, :] = y
        ecum_scr[rows_t, :] = jnp.exp(cum[t])
        xw_scr[rows_t, :] = xdt[t] * jnp.exp(cum[dl - 1] - cum[t])
        bmat_scr[rows_t, :] = bm[t]
        own_rows = pl.ds(t, bb, stride=SEQ_ROWS)
        _put_lane_blocks(seq_scr, c_q, own_rows, q_ref[t] * (HEAD_DIM ** -0.5))
        _put_lane_blocks(seq_scr, c_c, own_rows, cm[t])
        _put_lane_blocks(seq_scr, c_k, own_rows, k_ref[t])
        _put_lane_blocks(seq_scr, c_v, own_rows, v_ref[t])
    xw = xw_scr[...]
    for j in range(D_SSM // LANES):
        xt_scr[j * LANES:(j + 1) * LANES, :] = xw[:, j * LANES:(j + 1) * LANES].T.astype(BF16)

    block_row = lax.broadcasted_iota(jnp.int32, (SSD_CHUNK, D_STATE), 0)

    def state_matmuls(b):
        my_rows = pl.ds(pl.multiple_of(b * SEQ_ROWS, SEQ_ROWS), SEQ_ROWS)
        state = sst_ref[b]
        state_b = state.astype(BF16)
        c_rows = jnp.concatenate([_get_lane_blocks(seq_scr, c_c, nbc, my_rows),
                                  jnp.zeros((SEQ_ROWS, nbc), F32)], axis=0).astype(BF16)
        y_off = [_mm_nt(c_rows[:, g * D_STATE:(g + 1) * D_STATE], state_b[g * hw:(g + 1) * hw, :])[0:SEQ_ROWS]
                 for g in range(N_SSM_GROUPS)]
        is_mine = (block_row & (bb - 1)) == b
        upd = [_mm(xt_scr[g * hw:(g + 1) * hw, :],
                   jnp.where(is_mine, bmat_scr[:, g * D_STATE:(g + 1) * D_STATE], 0.0).astype(BF16))
               for g in range(N_SSM_GROUPS)]
        return my_rows, state, y_off, upd

    def attention_scores(b, slot, my_rows):
        kk_scr[slot, 0:WINDOW, :] = ck_ref[b]
        kk_scr[slot, WINDOW:WINDOW + SEQ_ROWS, :] = _get_lane_blocks(seq_scr, c_k, KV_WIDTH, my_rows)
        vv_scr[slot, 0:WINDOW, :] = cv_ref[b]
        vv_scr[slot, WINDOW:WINDOW + SEQ_ROWS, :] = _get_lane_blocks(seq_scr, c_v, KV_WIDTH, my_rows)
        ck_out_ref[b] = kk_scr[slot, dl:dl + WINDOW, :]
        cv_out_ref[b] = vv_scr[slot, dl:dl + WINDOW, :]
        return _attention_rows_scores(_get_lane_blocks(seq_scr, c_q, D_ATTN, my_rows), kk_scr[slot], vv_scr[slot])

    def store_state(b, my_rows, state, y_off, upd):
        _put_lane_blocks(res_scr, 0, my_rows, jnp.concatenate(y_off, axis=1))
        decay = jnp.broadcast_to(cdec_scr[pl.ds(b, 1), :], (SUBLANES, D_SSM))
        for j in range(D_SSM // LANES):
            g, jj = divmod(j, hw // LANES)
            for half, dec in enumerate(_split_pair(decay[:, j * LANES:(j + 1) * LANES])):
                h0 = j * LANES + half * HEAD_DIM
                u0 = jj * LANES + half * HEAD_DIM
                sst_out_ref[b, h0:h0 + HEAD_DIM, :] = (
                    state[h0:h0 + HEAD_DIM] * jnp.broadcast_to(dec[0:1], (HEAD_DIM, D_STATE))
                    + upd[g][u0:u0 + HEAD_DIM])

    def per_group(i, carry):
        seqs = [i * SAMPLE_UNROLL + slot for slot in range(SAMPLE_UNROLL)]
        parts = [state_matmuls(b) for b in seqs]
        scored = [attention_scores(b, slot, parts[slot][0]) for slot, b in enumerate(seqs)]
        for slot, b in enumerate(seqs):
            store_state(b, *parts[slot])
        weights = [_attention_rows_probs(scores, sinks_ref) for scores, _ in scored]
        for slot in range(SAMPLE_UNROLL):
            att = _attention_rows_output(*weights[slot], scored[slot][1])
            _put_lane_blocks(res_scr, D_SSM, parts[slot][0], att)
        return carry

    lax.fori_loop(0, bb // SAMPLE_UNROLL, per_group, 0)

    for t in range(dl):
        own_rows = pl.ds(t, bb, stride=SEQ_ROWS)
        rows_t = slice(t * bb, (t + 1) * bb)
        y = ypart_scr[rows_t, :] + _get_lane_blocks(res_scr, 0, D_SSM, own_rows) * ecum_scr[rows_t, :]
        mix_ref[t, :, 0:D_SSM] = _gated_group_norm(y, z_ref[t], snorm_ref[...])
        mix_ref[t, :, D_SSM:D_SSM + D_ATTN] = _get_lane_blocks(res_scr, D_SSM, D_ATTN, own_rows)


def _mixer_sample(z, xbc, dtr, q, k, v, state_ssm, state_conv, cache_k, cache_v, pw, layer, carried):
    depth = state_ssm.shape[0]
    dl, nb = xbc.shape[0], xbc.shape[1]
    bb = min(SAMPLE_SEQS_PER_STEP, nb)
    assert bb & (bb - 1) == 0 and dl * bb <= SSD_CHUNK and dl <= SEQ_ROWS and bb % SAMPLE_UNROLL == 0
    tok = lambda width: pl.BlockSpec((dl, bb, width), lambda i: (0, i, 0))
    lblk = lambda *dims: pl.BlockSpec((None, bb) + dims, lambda i: (layer, i) + (0,) * len(dims))
    consts = _mixer_consts(pw)
    operands = (z, xbc, dtr, q, k, v, state_ssm, state_conv, cache_k, cache_v) + consts
    assert len(operands) == N_SAMPLE_IN
    seq_cols = D_ATTN + N_SSM_GROUPS * D_STATE + 2 * KV_WIDTH
    return pl.pallas_call(
        _mixer_sample_kernel,
        grid=(nb // bb,),
        in_specs=[tok(D_SSM), tok(CONV_DIM), tok(LANES), tok(D_ATTN), tok(KV_WIDTH), tok(KV_WIDTH),
                  lblk(D_SSM, D_STATE),
                  pl.BlockSpec((None, CONV_W - 1, bb, CONV_DIM), lambda i: (layer, 0, i, 0)),
                  lblk(WINDOW, KV_WIDTH), lblk(WINDOW, KV_WIDTH)]
                 + [_layer_block(cst, layer) for cst in consts]
                 + [pl.BlockSpec(memory_space=pl.ANY)] * len(carried),
        out_specs=[tok(D_SSM + D_ATTN), lblk(D_SSM, D_STATE), lblk(WINDOW, KV_WIDTH), lblk(WINDOW, KV_WIDTH)],
        out_shape=[jax.ShapeDtypeStruct((dl, nb, D_SSM + D_ATTN), F32),
                   jax.ShapeDtypeStruct((depth, nb, D_SSM, D_STATE), F32),
                   jax.ShapeDtypeStruct((depth, nb, WINDOW, KV_WIDTH), F32),
                   jax.ShapeDtypeStruct((depth, nb, WINDOW, KV_WIDTH), F32)],
        input_output_aliases={N_SAMPLE_IN + n: 1 + n for n in range(len(carried))},
        scratch_shapes=[pltpu.VMEM((dl * bb, D_SSM), F32),
                        pltpu.VMEM((dl * bb, D_SSM), F32),
                        pltpu.VMEM((bb, D_SSM), F32),
                        pltpu.VMEM((SSD_CHUNK, D_SSM), F32),
                        pltpu.VMEM((D_SSM, SSD_CHUNK), BF16),
                        pltpu.VMEM((SSD_CHUNK, N_SSM_GROUPS * D_STATE), F32),
                        pltpu.VMEM((seq_cols // LANES, bb * SEQ_ROWS, LANES), F32),
                        pltpu.VMEM(((D_SSM + D_ATTN) // LANES, bb * SEQ_ROWS, LANES), F32),
                        pltpu.VMEM((SAMPLE_UNROLL, KEY_PAD, KV_WIDTH), F32),
                        pltpu.VMEM((SAMPLE_UNROLL, KEY_PAD, KV_WIDTH), F32)],
        compiler_params=pltpu.CompilerParams(dimension_semantics=("arbitrary",),
                                             vmem_limit_bytes=VMEM_LIMIT),
        name="mixer_sample",
    )(*operands, *carried)


def _prepare_weights(p):
    bf = lambda w: w.astype(BF16)
    vec = lambda v: v[:, None, :]

    rep = lambda v: jnp.repeat(v, HEAD_DIM, axis=-1)[:, None, :]
    lane_pad = lambda v: jnp.pad(v, ((0, 0), (0, LANES - v.shape[-1])))[:, None, :]
    cuts = np.cumsum([D_SSM, CONV_DIM, N_SSM_HEADS, D_ATTN, KV_WIDTH])
    wz, wxbc, wdt, wq, wk, wv = jnp.split(bf(p['w_in']), cuts, axis=2)
    depth = p['w_in'].shape[0]
    seg = np.kron(np.eye(D_ATTN // HEAD_DIM, dtype=np.float32), np.ones((HEAD_DIM, HEAD_DIM), np.float32))
    heads = np.eye(LANES, dtype=np.float32)[:, :N_SSM_HEADS]
    expand = np.tile(np.kron(heads, np.ones((1, HEAD_DIM), np.float32)), (3, 1))
    return {
        'g_ffn1': vec(p['g_ffn1']), 'g_mix': vec(p['g_mix']), 'g_ffn2': vec(p['g_ffn2']), 'g_ple': vec(p['g_ple']),
        'w1_a': bf(p['w1_a']), 'w3_a': bf(p['w3_a']), 'w2_a': bf(p['w2_a']),
        'w1_b': bf(p['w1_b']), 'w3_b': bf(p['w3_b']), 'w2_b': bf(p['w2_b']),
        'w_in': jnp.concatenate(
            [wz, wxbc, wq, wk, wv, jnp.pad(wdt, ((0, 0), (0, 0), (0, LANES - N_SSM_HEADS)))], axis=2),
        'w_out': bf(p['w_out']), 'w_ple_gate': bf(p['w_ple_gate']), 'w_ple_proj': bf(p['w_ple_proj']),
        'q_gain': vec(jnp.tile(p['q_norm'], (1, N_HEADS))),
        'k_gain': vec(jnp.tile(p['k_norm'], (1, N_KV_HEADS))),
        'seg': jnp.broadcast_to(jnp.asarray(seg, BF16), (depth,) + seg.shape),
        'conv_w': p['conv_w'], 'conv_b': vec(p['conv_b']),
        'dt_bias': lane_pad(p['dt_bias']), 'a_log_rep': rep(p['a_log']),
        'd_skip': rep(p['d_skip']),
        'head_expand': jnp.broadcast_to(jnp.asarray(expand, BF16), (depth,) + expand.shape),
        'ssm_norm': vec(p['ssm_norm']),
        'sinks': jnp.broadcast_to(p['sinks'][:, :, None], (depth, N_HEADS, LANES)),
    }


def kernel(x_prompt, x_sample, state_ssm, state_conv, cache_k_win, cache_v_win, p_prompt, p_sample, g_ffn1, w1_a, w3_a, w2_a, g_mix, w_in, conv_w, conv_b, dt_bias, a_log, d_skip, ssm_norm, q_norm, k_norm, sinks, w_out, g_ffn2, w1_b, w3_b, w2_b, g_ple, w_ple_gate, w_ple_proj):
    params = dict(g_ffn1=g_ffn1, w1_a=w1_a, w3_a=w3_a, w2_a=w2_a, g_mix=g_mix, w_in=w_in, conv_w=conv_w,
                  conv_b=conv_b, dt_bias=dt_bias, a_log=a_log, d_skip=d_skip, ssm_norm=ssm_norm, q_norm=q_norm,
                  k_norm=k_norm, sinks=sinks, w_out=w_out, g_ffn2=g_ffn2, w1_b=w1_b, w3_b=w3_b, w2_b=w2_b,
                  g_ple=g_ple, w_ple_gate=w_ple_gate, w_ple_proj=w_ple_proj)
    depth = w_in.shape[0]
    bp, seq, _ = x_prompt.shape
    bs, dl, _ = x_sample.shape
    assert seq % SSD_CHUNK == 0 and seq >= WINDOW and dl >= CONV_W - 1 and dl <= SUBLANES
    assert (bp * seq) % TOKEN_TILE == 0 and (bs * dl) % min(TOKEN_TILE, bs * dl) == 0

    pw = _prepare_weights(params)
    sst_in = state_ssm.reshape(depth, bs, D_SSM, D_STATE)
    ck_in = cache_k_win.reshape(depth, bs, WINDOW, KV_WIDTH)
    cv_in = cache_v_win.reshape(depth, bs, WINDOW, KV_WIDTH)
    sconv_in = jnp.swapaxes(state_conv, 1, 2)
    pe_p = p_prompt.reshape(depth, bp * seq, D_PLE)
    pe_s = jnp.swapaxes(p_sample, 1, 2).reshape(depth, dl * bs, D_PLE)
    hp = x_prompt.reshape(bp * seq, D_MODEL)
    hs = jnp.swapaxes(x_sample, 0, 1).reshape(dl * bs, D_MODEL)
    ssm_p, conv_p, k_p, v_p, conv_s = [], [], [], [], []
    carried = ()
    for l in range(depth):
        hp, zp, xbcp, dtp, qp, kp, vp = _stage_a(hp, pw, l)
        hs, zs, xbcs, dts, qs, ks, vs = _stage_a(hs, pw, l)
        mixp, sp = _mixer_prompt(zp, xbcp, dtp, qp, kp, vp, pw, l, bp, seq)
        hp = _stage_c(hp, mixp, pe_p, pw, l)
        tm3 = lambda a: a.reshape(dl, bs, a.shape[-1])
        mixs, *carried = _mixer_sample(tm3(zs), tm3(xbcs), tm3(dts), tm3(qs), tm3(ks), tm3(vs),
                                       sst_in, sconv_in, ck_in, cv_in, pw, l, tuple(carried))
        hs = _stage_c(hs, mixs.reshape(dl * bs, D_MODEL), pe_s, pw, l)
        ssm_p.append(sp.reshape(bp, N_SSM_HEADS, HEAD_DIM, D_STATE))
        conv_p.append(xbcp.reshape(bp, seq, CONV_DIM)[:, seq - (CONV_W - 1):])
        last_window = lambda a: a.reshape(bp, seq, KV_WIDTH)[:, seq - WINDOW:].reshape(
            bp, WINDOW, N_KV_HEADS, HEAD_DIM)
        k_p.append(last_window(kp))
        v_p.append(last_window(vp))
        conv_s.append(jnp.swapaxes(tm3(xbcs)[dl - (CONV_W - 1):], 0, 1))
    ssm_s, k_s, v_s = carried
    return (hp.reshape(bp, seq, D_MODEL), jnp.swapaxes(hs.reshape(dl, bs, D_MODEL), 0, 1),
            jnp.stack(ssm_p), jnp.stack(conv_p), jnp.stack(k_p), jnp.stack(v_p),
            ssm_s.reshape(depth, bs, N_SSM_HEADS, HEAD_DIM, D_STATE), jnp.stack(conv_s),
            k_s.reshape(depth, bs, WINDOW, N_KV_HEADS, HEAD_DIM),
            v_s.reshape(depth, bs, WINDOW, N_KV_HEADS, HEAD_DIM))
```

```python
import functools

import numpy as np
import jax
import jax.numpy as jnp
from jax import lax
from jax.experimental import pallas as pl
from jax.experimental.pallas import tpu as pltpu

F32 = jnp.float32
BF16 = jnp.bfloat16

D_MODEL = 1024
HEAD_DIM = 64
D_SSM = 512
N_SSM_HEADS = 8
N_SSM_GROUPS = 2
D_STATE = 128
CONV_W = 4
CONV_DIM = D_SSM + 2 * N_SSM_GROUPS * D_STATE
SSD_CHUNK = 128
D_ATTN = 512
N_HEADS = 8
N_KV_HEADS = 2
Q_PER_KV = N_HEADS // N_KV_HEADS
KV_WIDTH = N_KV_HEADS * HEAD_DIM
WINDOW = 128
D_FF = 2752
D_PLE = 256
RMS_EPS = 1e-6

LANES = 128
SUBLANES = 8
MXU_DIM = 256
FF_CHUNK = MXU_DIM
FF_FULL_CHUNKS = D_FF // FF_CHUNK
FF_TAIL = D_FF - FF_FULL_CHUNKS * FF_CHUNK
FF_TAIL_PAD = -(-FF_TAIL // MXU_DIM) * MXU_DIM
CAST_ROWS = 1024
TOKEN_TILE = 512
MIXER_TILE = 512
SAMPLE_SEQS_PER_STEP = 16
SAMPLE_UNROLL = 4
SEQ_ROWS = SUBLANES
KEY_PAD = 2 * WINDOW
VMEM_LIMIT = 56 * 1024 * 1024

OFF_Z = 0
OFF_XBC = OFF_Z + D_SSM
OFF_Q = OFF_XBC + CONV_DIM
OFF_K = OFF_Q + D_ATTN
OFF_V = OFF_K + KV_WIDTH
OFF_DT = OFF_V + KV_WIDTH
D_PROJ_PAD = OFF_DT + LANES

ALIBI_SLOPES = tuple(float(s) for s in np.power(
    np.float32(2.0), -8.0 * np.arange(1, N_HEADS + 1, dtype=np.float32) / N_HEADS))

_NT = (((1,), (1,)), ((), ()))


def _mm(a, b):
    return jnp.dot(a, b, preferred_element_type=F32)


def _mm_nt(a, b):
    return lax.dot_general(a, b, _NT, preferred_element_type=F32)


def _sigmoid(x):
    return 1.0 / (1.0 + jnp.exp(-x))


def _silu(x):
    return x * _sigmoid(x)


def _softplus(x):
    return jnp.maximum(x, 0.0) + jnp.log(1.0 + jnp.exp(-jnp.abs(x)))


def _rmsnorm(x, g):
    return x * lax.rsqrt(jnp.mean(x * x, axis=-1, keepdims=True) + RMS_EPS) * g


def _split3(x):
    hi = x.astype(BF16)
    rest = x - hi.astype(F32)
    mid = rest.astype(BF16)
    return hi, mid, (rest - mid.astype(F32)).astype(BF16)


def _stage_ff_tail(w1_ref, w3_ref, w2_ref, tail_refs):
    w1t_ref, w3t_ref, w2t_ref = tail_refs
    lo = FF_FULL_CHUNKS * FF_CHUNK
    for dst, src in ((w1t_ref, w1_ref), (w3t_ref, w3_ref)):
        dst[...] = jnp.zeros_like(dst)
        dst[:, 0:FF_TAIL] = src[:, lo:D_FF]
    w2t_ref[...] = jnp.zeros_like(w2t_ref)
    w2t_ref[0:FF_TAIL, :] = w2_ref[lo:D_FF, :]


def _ff_tail_scratch():
    return [pltpu.VMEM((D_MODEL, FF_TAIL_PAD), BF16), pltpu.VMEM((D_MODEL, FF_TAIL_PAD), BF16),
            pltpu.VMEM((FF_TAIL_PAD, D_MODEL), BF16)]


def _swiglu(xn, w1_ref, w3_ref, w2_ref, tail_refs):
    def chunk(acc, w1c, w3c, w2c):
        a = _mm(xn, w1c)
        b = _mm(xn, w3c)
        return acc + _mm((_silu(a) * b).astype(BF16), w2c)

    acc = jnp.zeros((xn.shape[0], D_MODEL), F32)
    for c in range(FF_FULL_CHUNKS):
        sl = slice(c * FF_CHUNK, (c + 1) * FF_CHUNK)
        acc = chunk(acc, w1_ref[:, sl], w3_ref[:, sl], w2_ref[sl, :])
    if FF_TAIL:
        acc = chunk(acc, *(r[...] for r in tail_refs))
    return acc


def _segment_sumsq(x, seg_ref):
    hi, mid, _ = _split3(x * x)
    seg = seg_ref[0:x.shape[1], 0:x.shape[1]]
    return _mm(hi, seg) + _mm(mid, seg)


def _stage_a_kernel(x_ref, g1_ref, w1_ref, w3_ref, w2_ref, gm_ref, win_ref, qg_ref, kg_ref, seg_ref,
                    h_ref, z_ref, xbc_ref, dt_ref, q_ref, k_ref, v_ref, *tail_refs):
    @pl.when(pl.program_id(0) == 0)
    def _():
        _stage_ff_tail(w1_ref, w3_ref, w2_ref, tail_refs)

    x = x_ref[...]
    h = x + 0.5 * _swiglu(_rmsnorm(x, g1_ref[...]).astype(BF16), w1_ref, w3_ref, w2_ref, tail_refs)
    h_ref[...] = h
    proj = _mm(_rmsnorm(h, gm_ref[...]).astype(BF16), win_ref[...])
    z_ref[...] = proj[:, OFF_Z:OFF_XBC]
    xbc_ref[...] = proj[:, OFF_XBC:OFF_Q]
    dt_ref[...] = proj[:, OFF_DT:D_PROJ_PAD]
    q = proj[:, OFF_Q:OFF_K]
    k = proj[:, OFF_K:OFF_V]
    q_ref[...] = q * lax.rsqrt(_segment_sumsq(q, seg_ref) * (1.0 / HEAD_DIM) + RMS_EPS) * qg_ref[...]
    k_ref[...] = k * lax.rsqrt(_segment_sumsq(k, seg_ref) * (1.0 / HEAD_DIM) + RMS_EPS) * kg_ref[...]
    v_ref[...] = proj[:, OFF_V:OFF_DT]


def _layer_block(arr, layer, **kwargs):
    return pl.BlockSpec((None,) + arr.shape[1:], lambda *_: (layer, 0, 0), **kwargs)


def _layer_resident(arr, layer):
    return _layer_block(arr, layer, pipeline_mode=pl.Buffered(1))


def _stage_a(x, pw, layer):
    n = x.shape[0]
    tm = min(TOKEN_TILE, n)
    row = lambda width: pl.BlockSpec((tm, width), lambda i: (i, 0))
    widths = (D_MODEL, D_SSM, CONV_DIM, LANES, D_ATTN, KV_WIDTH, KV_WIDTH)
    consts = (pw['g_ffn1'], pw['w1_a'], pw['w3_a'], pw['w2_a'], pw['g_mix'], pw['w_in'],
              pw['q_gain'], pw['k_gain'], pw['seg'])
    return pl.pallas_call(
        _stage_a_kernel,
        grid=(n // tm,),
        in_specs=[row(D_MODEL)] + [_layer_resident(c, layer) for c in consts],
        out_specs=[row(w) for w in widths],
        out_shape=[jax.ShapeDtypeStruct((n, w), F32) for w in widths],
        scratch_shapes=_ff_tail_scratch(),
        compiler_params=pltpu.CompilerParams(dimension_semantics=("arbitrary",),
                                             vmem_limit_bytes=VMEM_LIMIT),
        name="stage_a",
    )(x, *consts)


def _stage_c_kernel(h_ref, mix_ref, pe_ref, wout_ref, g2_ref, w1_ref, w3_ref, w2_ref, gp_ref,
                    wgate_ref, wproj_ref, o_ref, *tail_refs):
    @pl.when(pl.program_id(0) == 0)
    def _():
        _stage_ff_tail(w1_ref, w3_ref, w2_ref, tail_refs)

    o_ref[...] = _stage_c_compute(h_ref[...], mix_ref[...].astype(BF16), pe_ref, wout_ref, g2_ref, w1_ref,
                                  w3_ref, w2_ref, gp_ref, wgate_ref, wproj_ref, tail_refs)


def _stage_c_compute(h, mix, pe_ref, wout_ref, g2_ref, w1_ref, w3_ref, w2_ref, gp_ref, wgate_ref, wproj_ref,
                     tail_refs):
    h = h + _mm(mix, wout_ref[...])
    h = h + 0.5 * _swiglu(_rmsnorm(h, g2_ref[...]).astype(BF16), w1_ref, w3_ref, w2_ref, tail_refs)
    gate = _sigmoid(_mm(_rmsnorm(h, gp_ref[...]).astype(BF16), wgate_ref[...]))
    return h + gate * _mm(pe_ref[...].astype(BF16), wproj_ref[...])


def _stage_c(h, mix, pe, pw, layer):
    n = h.shape[0]
    tm = min(TOKEN_TILE, n)
    row = lambda width: pl.BlockSpec((tm, width), lambda i: (i, 0))
    consts = (pw['w_out'], pw['g_ffn2'], pw['w1_b'], pw['w3_b'], pw['w2_b'], pw['g_ple'],
              pw['w_ple_gate'], pw['w_ple_proj'])
    return pl.pallas_call(
        _stage_c_kernel,
        grid=(n // tm,),
        in_specs=[row(D_MODEL), row(D_MODEL), pl.BlockSpec((None, tm, D_PLE), lambda i: (layer, i, 0))]
                 + [_layer_resident(c, layer) for c in consts],
        out_specs=row(D_MODEL),
        out_shape=jax.ShapeDtypeStruct((n, D_MODEL), F32),
        scratch_shapes=_ff_tail_scratch(),
        compiler_params=pltpu.CompilerParams(dimension_semantics=("arbitrary",),
                                             vmem_limit_bytes=VMEM_LIMIT),
        name="stage_c",
    )(h, mix, pe, *consts)


def _lane_low_half(shape):
    return lax.broadcasted_iota(jnp.int32, shape, len(shape) - 1) < HEAD_DIM


def _split_pair(x):
    rolled = pltpu.roll(x, HEAD_DIM, 1)
    low = _lane_low_half(x.shape)
    return jnp.where(low, x, rolled), jnp.where(low, rolled, x)


def _head_query_rows(q, g):
    low = _lane_low_half((q.shape[0], LANES))
    blocks = []
    for jj in range(Q_PER_KV // 2):
        qp = q[:, (g * 2 + jj) * LANES:(g * 2 + jj + 1) * LANES]
        blocks += [jnp.where(low, qp, 0.0), jnp.where(low, 0.0, qp)]
    return jnp.concatenate(blocks, axis=0).astype(BF16)


def _attention_rows_scores(q, kk, vv):
    low_k = _lane_low_half((KEY_PAD, LANES))
    k_dup = _split_pair(kk)
    v_roll = pltpu.roll(vv, HEAD_DIM, 1)
    v_even = (jnp.where(low_k, vv, 1.0).astype(BF16), jnp.where(low_k, v_roll, 1.0).astype(BF16))
    v_odd = (jnp.where(low_k, 1.0, v_roll).astype(BF16), jnp.where(low_k, 1.0, vv).astype(BF16))
    scores = [_mm_nt(_head_query_rows(q, g), k_dup[g].astype(BF16)) for g in range(N_KV_HEADS)]
    return scores, (v_even, v_odd)


def _attention_rows_probs(scores, sinks_ref):
    r = scores[0].shape[0] // Q_PER_KV
    t_i = lax.broadcasted_iota(jnp.int32, (r, KEY_PAD), 0)
    s_i = lax.broadcasted_iota(jnp.int32, (r, KEY_PAD), 1)
    rel = t_i + WINDOW - s_i
    valid = (rel >= 0) & (rel < WINDOW)
    relf = rel.astype(F32)
    probs, sink_terms = [], []
    for head in range(N_HEADS):
        g, hh = divmod(head, Q_PER_KV)
        sh = jnp.where(valid, scores[g][hh * r:(hh + 1) * r] - ALIBI_SLOPES[head] * relf, -jnp.inf)
        sink = sinks_ref[head:head + 1, 0:1]
        m = jnp.maximum(jnp.max(sh, axis=-1, keepdims=True), sink)
        probs.append(jnp.exp(sh - m))
        sink_terms.append(jnp.broadcast_to(jnp.exp(sink - m), (r, LANES)))
    return probs, sink_terms


def _attention_rows_output(probs, sink_terms, v_operands):
    v_even, v_odd = v_operands
    r = probs[0].shape[0]
    low = _lane_low_half((r, LANES))
    outs = []
    for g in range(N_KV_HEADS):
        mine = probs[g * Q_PER_KV:(g + 1) * Q_PER_KV]
        o_even = _mm(jnp.concatenate(mine[0::2], axis=0).astype(BF16), v_even[g])
        o_odd = _mm(jnp.concatenate(mine[1::2], axis=0).astype(BF16), v_odd[g])
        for jj in range(Q_PER_KV // 2):
            oe, oo = o_even[jj * r:(jj + 1) * r], o_odd[jj * r:(jj + 1) * r]
            head = g * Q_PER_KV + 2 * jj
            den = (pltpu.roll(jnp.where(low, oo, oe), HEAD_DIM, 1)
                   + jnp.where(low, sink_terms[head], sink_terms[head + 1]))
            outs.append(jnp.where(low, oe, oo) / den)
    return jnp.concatenate(outs, axis=1)


def _fill_attention_bias(bias_scr):
    s_i = lax.broadcasted_iota(jnp.int32, (KEY_PAD, WINDOW), 0)
    t_i = lax.broadcasted_iota(jnp.int32, (KEY_PAD, WINDOW), 1)
    rel = t_i + WINDOW - s_i
    valid = (rel >= 0) & (rel < WINDOW)
    relf = rel.astype(F32)
    for head in range(N_HEADS):
        bias_scr[head * KEY_PAD:(head + 1) * KEY_PAD, :] = jnp.where(valid, -ALIBI_SLOPES[head] * relf, -jnp.inf)


def _attention_cols_scores(q, kk):
    k_dup = _split_pair(kk)
    qs = q * (HEAD_DIM ** -0.5)
    return [_mm_nt(k_dup[g].astype(BF16), _head_query_rows(qs, g)) for g in range(N_KV_HEADS)]


def _attention_cols_probs(scores, bias_scr, sinks_ref, first_block):
    r = scores[0].shape[1] // Q_PER_KV
    mask_prev = None if first_block is None else jnp.where(first_block, -jnp.inf, 0.0)
    out = []
    for g in range(N_KV_HEADS):
        probs, sink_terms = [], []
        for hh in range(Q_PER_KV):
            head = g * Q_PER_KV + hh
            sh = scores[g][:, hh * r:(hh + 1) * r] + bias_scr[head * KEY_PAD:(head + 1) * KEY_PAD, :]
            prev = sh[0:WINDOW] if mask_prev is None else sh[0:WINDOW] + mask_prev
            cur = sh[WINDOW:KEY_PAD]
            sink = sinks_ref[head:head + 1, :]
            m = jnp.maximum(jnp.maximum(jnp.max(prev, axis=0, keepdims=True),
                                        jnp.max(cur, axis=0, keepdims=True)), sink)
            probs.append(jnp.concatenate([jnp.exp(prev - m), jnp.exp(cur - m)], axis=0))
            sink_terms.append(jnp.exp(sink - m))
        out.append((jnp.concatenate(probs, axis=1).astype(BF16), jnp.concatenate(sink_terms, axis=1)))
    return out


def _attention_cols_pv(probs, vv):
    vt = vv.T
    ones = jnp.ones((HEAD_DIM, KEY_PAD), F32)
    return [_mm(jnp.concatenate([vt[g * HEAD_DIM:(g + 1) * HEAD_DIM], ones], axis=0).astype(BF16), probs[g][0])
            for g in range(N_KV_HEADS)]


def _attention_cols_output(pv, probs):
    r = pv[0].shape[1] // Q_PER_KV
    outs = []
    for g in range(N_KV_HEADS):
        den = pv[g][HEAD_DIM:HEAD_DIM + 1, :] + probs[g][1]
        on = pv[g][0:HEAD_DIM, :] * (1.0 / den)
        for jj in range(Q_PER_KV // 2):
            pair = jnp.concatenate([on[:, (2 * jj) * r:(2 * jj + 1) * r],
                                    on[:, (2 * jj + 1) * r:(2 * jj + 2) * r]], axis=0)
            outs.append(pair.T)
    return jnp.concatenate(outs, axis=1)


def _gated_group_norm(y, z, gain):
    y = y * _silu(z)
    gw = D_SSM // N_SSM_GROUPS
    parts = [_rmsnorm(y[:, g * gw:(g + 1) * gw], gain[:, g * gw:(g + 1) * gw]) for g in range(N_SSM_GROUPS)]
    return jnp.concatenate(parts, axis=1)


def _prompt_tile_mixer(first, z_ref, xbc_ref, xprev_ref, dt_ref, q_ref, k_ref, kprev_ref, v_ref, vprev_ref,
                       cw_ref, cb_ref, dtb_ref, alog_ref, dskip_ref, snorm_ref, sinks_ref, expand_ref,
                       st_scr, xe_scr, bias_scr, store_mix, state_after):
    t = SSD_CHUNK
    rows = xbc_ref.shape[0]
    hw = D_SSM // N_SSM_GROUPS
    nbc = N_SSM_GROUPS * D_STATE

    dt_raw = _mm(jnp.concatenate(_split3(dt_ref[...] + dtb_ref[...]), axis=1), expand_ref[...])

    xe_scr[0:SUBLANES, :] = jnp.where(first, 0.0, xprev_ref[...])
    xe_scr[SUBLANES:SUBLANES + rows, :] = xbc_ref[...]
    cw = cw_ref[...]
    a_rep = -jnp.exp(alog_ref[...])
    causal = lax.broadcasted_iota(jnp.int32, (t, t), 0) >= lax.broadcasted_iota(jnp.int32, (t, t), 1)
    ltri = causal.astype(BF16)
    low_b = _lane_low_half((t, LANES))
    chunks = []
    for c in range(rows // t):
        sl = slice(c * t, (c + 1) * t)
        conv = cb_ref[...] + cw[CONV_W - 1:CONV_W] * xbc_ref[sl, :]
        for back in range(1, CONV_W):
            r0 = SUBLANES - back + c * t
            conv = conv + cw[CONV_W - 1 - back:CONV_W - back] * xe_scr[r0:r0 + t, :]
        xc = _silu(conv)
        dt = _softplus(dt_raw[sl])
        chunks.append(dict(
            sl=sl, dt=dt, xs=xc[:, 0:D_SSM], bm=xc[:, D_SSM:D_SSM + nbc], cm=xc[:, D_SSM + nbc:CONV_DIM],
            cum3=_mm(ltri, jnp.concatenate(_split3(dt * a_rep), axis=1))))
        yield

    for c, ch in enumerate(chunks):
        sl, xs, bm, cm = ch['sl'], ch['xs'], ch['bm'], ch['cm']
        cum3 = ch['cum3']
        cum = cum3[:, 0:D_SSM] + cum3[:, D_SSM:2 * D_SSM] + cum3[:, 2 * D_SSM:3 * D_SSM]
        xdt = xs * ch['dt']
        xw_b = (xdt * jnp.exp(cum[t - 1:t, :] - cum)).astype(BF16)
        cg_b = [cm[:, g * D_STATE:(g + 1) * D_STATE].astype(BF16) for g in range(N_SSM_GROUPS)]
        bg = [bm[:, g * D_STATE:(g + 1) * D_STATE] for g in range(N_SSM_GROUPS)]
        k_before = kprev_ref[...] if c == 0 else k_ref[(c - 1) * t:c * t, :]
        v_before = vprev_ref[...] if c == 0 else v_ref[(c - 1) * t:c * t, :]
        ch.update(
            cum=cum, ecum=jnp.exp(cum), xdt_b=xdt.astype(BF16), cg_b=cg_b,
            cb_t=[_mm_nt(cg_b[g], bg[g].astype(BF16)) for g in range(N_SSM_GROUPS)],
            st_new=[_mm(bg[g].T.astype(BF16), xw_b[:, g * hw:(g + 1) * hw]) for g in range(N_SSM_GROUPS)],
            att_scores=_attention_cols_scores(q_ref[sl, :], jnp.concatenate([k_before, k_ref[sl, :]], axis=0)),
            vv=jnp.concatenate([v_before, v_ref[sl, :]], axis=0))
        yield

    st = jnp.where(first, 0.0, st_scr[...])
    for c, ch in enumerate(chunks):
        y_in = []
        for j in range(D_SSM // LANES):
            cb_t = ch['cb_t'][j // (hw // LANES)]
            scores = []
            for col in _split_pair(ch['cum'][:, j * LANES:(j + 1) * LANES]):
                decay = jnp.exp(jnp.where(causal, col - col.T, -jnp.inf))
                scores.append((cb_t * decay).astype(BF16))
            xp = ch['xdt_b'][:, j * LANES:(j + 1) * LANES]
            zero = jnp.zeros_like(xp)
            rhs = jnp.concatenate([jnp.where(low_b, xp, zero), jnp.where(low_b, zero, xp)], axis=0)
            y_in.append(_mm(jnp.concatenate(scores, axis=1), rhs))
        yield

        probs = _attention_cols_probs(ch['att_scores'], bias_scr, sinks_ref, first if c == 0 else None)
        pv = _attention_cols_pv(probs, ch['vv'])
        st_b = st.astype(BF16)
        y_off = [_mm(ch['cg_b'][g], st_b[:, g * hw:(g + 1) * hw]) for g in range(N_SSM_GROUPS)]
        yield

        y = (jnp.concatenate(y_in, axis=1) + jnp.concatenate(y_off, axis=1) * ch['ecum']
             + dskip_ref[...] * ch['xs'])
        st = st * ch['ecum'][t - 1:t, :] + jnp.concatenate(ch['st_new'], axis=1)
        ssd_out = _gated_group_norm(y, z_ref[ch['sl'], :], snorm_ref[...])
        yield

        store_mix(ch['sl'], ssd_out, _attention_cols_output(pv, probs))
        yield
    st_scr[...] = st
    state_after.append(st)


def _mixer_prompt_kernel(*refs, tiles_per_seq):
    mix_ref, state_ref, st_scr, xe_scr, bias_scr = refs[-5:]
    j = pl.program_id(0)

    @pl.when(j == 0)
    def _():
        _fill_attention_bias(bias_scr)

    def store_mix(rows, ssd_out, attn_out):
        mix_ref[rows, 0:D_SSM] = ssd_out.astype(mix_ref.dtype)
        mix_ref[rows, D_SSM:D_SSM + D_ATTN] = attn_out.astype(mix_ref.dtype)

    state_after = []
    for _ in _prompt_tile_mixer(lax.rem(j, tiles_per_seq) == 0, *refs[:-5], st_scr, xe_scr, bias_scr,
                                store_mix, state_after):
        pass
    st = state_after[0]

    @pl.when(lax.rem(j, tiles_per_seq) == tiles_per_seq - 1)
    def _():
        for jb in range(D_SSM // LANES):
            state_ref[jb * LANES:(jb + 1) * LANES, :] = st[:, jb * LANES:(jb + 1) * LANES].T


def _mixer_consts(pw):
    return (pw['conv_w'], pw['conv_b'], pw['dt_bias'], pw['a_log_rep'], pw['d_skip'], pw['ssm_norm'], pw['sinks'],
            pw['head_expand'])


def _mixer_prompt(z, xbc, dtr, q, k, v, pw, layer, batch, seq):
    t = SSD_CHUNK
    tile = MIXER_TILE
    tiles_per_seq = seq // tile
    cur = lambda width: pl.BlockSpec((tile, width), lambda i: (i, 0))
    before = lambda rows, width: pl.BlockSpec((rows, width), lambda i: (jnp.maximum(i * (tile // rows) - 1, 0), 0))
    consts = _mixer_consts(pw)
    return pl.pallas_call(
        functools.partial(_mixer_prompt_kernel, tiles_per_seq=tiles_per_seq),
        grid=(batch * tiles_per_seq,),
        in_specs=[cur(D_SSM), cur(CONV_DIM), before(SUBLANES, CONV_DIM), cur(LANES), cur(D_ATTN),
                  cur(KV_WIDTH), before(t, KV_WIDTH), cur(KV_WIDTH), before(t, KV_WIDTH)]
                 + [_layer_block(cst, layer) for cst in consts],
        out_specs=[pl.BlockSpec((tile, D_SSM + D_ATTN), lambda i: (i, 0)),
                   pl.BlockSpec((None, D_SSM, D_STATE), lambda i: (i // tiles_per_seq, 0, 0))],
        out_shape=[jax.ShapeDtypeStruct((batch * seq, D_SSM + D_ATTN), BF16),
                   jax.ShapeDtypeStruct((batch, D_SSM, D_STATE), F32)],
        scratch_shapes=[pltpu.VMEM((D_STATE, D_SSM), F32),
                        pltpu.VMEM((SUBLANES + tile, CONV_DIM), F32),
                        pltpu.VMEM((N_HEADS * KEY_PAD, WINDOW), F32)],
        compiler_params=pltpu.CompilerParams(dimension_semantics=("arbitrary",),
                                             vmem_limit_bytes=VMEM_LIMIT),
        name="mixer_prompt",
    )(z, xbc, xbc, dtr, q, k, k, v, v, *consts)


N_SAMPLE_IN = 18


def _put_lane_blocks(scr, col0, rows, val):
    for j in range(val.shape[1] // LANES):
        scr[col0 // LANES + j, rows, :] = val[:, j * LANES:(j + 1) * LANES]


def _get_lane_blocks(scr, col0, width, rows):
    return jnp.concatenate([scr[col0 // LANES + j, rows, :] for j in range(width // LANES)], axis=1)


def _mixer_sample_kernel(*refs):
    (z_ref, xbc_ref, dt_ref, q_ref, k_ref, v_ref, sst_ref, sconv_ref, ck_ref, cv_ref,
     cw_ref, cb_ref, dtb_ref, alog_ref, dskip_ref, snorm_ref, sinks_ref, expand_ref) = refs[:N_SAMPLE_IN]
    (mix_ref, sst_out_ref, ck_out_ref, cv_out_ref,
     ypart_scr, ecum_scr, cdec_scr, xw_scr, xt_scr, bmat_scr, seq_scr, res_scr, kk_scr, vv_scr) = refs[-14:]
    dl, bb = xbc_ref.shape[0], xbc_ref.shape[1]
    hw = D_SSM // N_SSM_GROUPS
    nbc = N_SSM_GROUPS * D_STATE
    c_q, c_c, c_k, c_v = 0, D_ATTN, D_ATTN + nbc, D_ATTN + nbc + KV_WIDTH
    xw_scr[...] = jnp.zeros_like(xw_scr)
    bmat_scr[...] = jnp.zeros_like(bmat_scr)
    seq_scr[...] = jnp.zeros_like(seq_scr)
    kk_scr[...] = jnp.zeros_like(kk_scr)
    vv_scr[...] = jnp.zeros_like(vv_scr)

    cw = cw_ref[...]
    a_rep = -jnp.exp(alog_ref[...])
    xs, bm, cm, xdt, cum = [], [], [], [], []
    for t in range(dl):
        conv = cb_ref[...] + cw[CONV_W - 1:CONV_W] * xbc_ref[t]
        for back in range(1, CONV_W):
            src = xbc_ref[t - back] if t >= back else sconv_ref[CONV_W - 1 + t - back]
            conv = conv + cw[CONV_W - 1 - back:CONV_W - back] * src
        xc = _silu(conv)
        xs.append(xc[:, 0:D_SSM])
        bm.append(xc[:, D_SSM:D_SSM + nbc])
        cm.append(xc[:, D_SSM + nbc:CONV_DIM])
        dt_c = _softplus(dt_ref[t] + dtb_ref[...])
        dt = _mm(jnp.concatenate(_split3(dt_c), axis=1), expand_ref[...])
        cum.append(dt * a_rep if t == 0 else cum[-1] + dt * a_rep)
        xdt.append(xs[t] * dt)
    cdec_scr[...] = jnp.exp(cum[dl - 1])
    for t in range(dl):
        y = dskip_ref[...] * xs[t]
        for s in range(t + 1):
            prod = cm[t] * bm[s]
            dots = jnp.concatenate(
                [jnp.broadcast_to(jnp.sum(prod[:, g * D_STATE:(g + 1) * D_STATE], axis=-1, keepdims=True), (bb, hw))
                 for g in range(N_SSM_GROUPS)], axis=1)
            y = y + (dots * xdt[s] if s == t else dots * jnp.exp(cum[t] - cum[s]) * xdt[s])
        rows_t = slice(t * bb, (t + 1) * bb)
        ypart_scr[rows_t, :] = y
        ecum_scr[rows_t, :] = jnp.exp(cum[t])
        xw_scr[rows_t, :] = xdt[t] * jnp.exp(cum[dl - 1] - cum[t])
        bmat_scr[rows_t, :] = bm[t]
        own_rows = pl.ds(t, bb, stride=SEQ_ROWS)
        _put_lane_blocks(seq_scr, c_q, own_rows, q_ref[t] * (HEAD_DIM ** -0.5))
        _put_lane_blocks(seq_scr, c_c, own_rows, cm[t])
        _put_lane_blocks(seq_scr, c_k, own_rows, k_ref[t])
        _put_lane_blocks(seq_scr, c_v, own_rows, v_ref[t])
    xw = xw_scr[...]
    for j in range(D_SSM // LANES):
        xt_scr[j * LANES:(j + 1) * LANES, :] = xw[:, j * LANES:(j + 1) * LANES].T.astype(BF16)

    block_row = lax.broadcasted_iota(jnp.int32, (SSD_CHUNK, D_STATE), 0)

    def state_matmuls(b):
        my_rows = pl.ds(pl.multiple_of(b * SEQ_ROWS, SEQ_ROWS), SEQ_ROWS)
        state = sst_ref[b]
        state_b = state.astype(BF16)
        c_rows = jnp.concatenate([_get_lane_blocks(seq_scr, c_c, nbc, my_rows),
                                  jnp.zeros((SEQ_ROWS, nbc), F32)], axis=0).astype(BF16)
        y_off = [_mm_nt(c_rows[:, g * D_STATE:(g + 1) * D_STATE], state_b[g * hw:(g + 1) * hw, :])[0:SEQ_ROWS]
                 for g in range(N_SSM_GROUPS)]
        is_mine = (block_row & (bb - 1)) == b
        upd = [_mm(xt_scr[g * hw:(g + 1) * hw, :],
                   jnp.where(is_mine, bmat_scr[:, g * D_STATE:(g + 1) * D_STATE], 0.0).astype(BF16))
               for g in range(N_SSM_GROUPS)]
        return my_rows, state, y_off, upd

    def attention_scores(b, slot, my_rows):
        kk_scr[slot, 0:WINDOW, :] = ck_ref[b]
        kk_scr[slot, WINDOW:WINDOW + SEQ_ROWS, :] = _get_lane_blocks(seq_scr, c_k, KV_WIDTH, my_rows)
        vv_scr[slot, 0:WINDOW, :] = cv_ref[b]
        vv_scr[slot, WINDOW:WINDOW + SEQ_ROWS, :] = _get_lane_blocks(seq_scr, c_v, KV_WIDTH, my_rows)
        ck_out_ref[b] = kk_scr[slot, dl:dl + WINDOW, :]
        cv_out_ref[b] = vv_scr[slot, dl:dl + WINDOW, :]
        return _attention_rows_scores(_get_lane_blocks(seq_scr, c_q, D_ATTN, my_rows), kk_scr[slot], vv_scr[slot])

    def store_state(b, my_rows, state, y_off, upd):
        _put_lane_blocks(res_scr, 0, my_rows, jnp.concatenate(y_off, axis=1))
        decay = jnp.broadcast_to(cdec_scr[pl.ds(b, 1), :], (SUBLANES, D_SSM))
        for j in range(D_SSM // LANES):
            g, jj = divmod(j, hw // LANES)
            for half, dec in enumerate(_split_pair(decay[:, j * LANES:(j + 1) * LANES])):
                h0 = j * LANES + half * HEAD_DIM
                u0 = jj * LANES + half * HEAD_DIM
                sst_out_ref[b, h0:h0 + HEAD_DIM, :] = (
                    state[h0:h0 + HEAD_DIM] * jnp.broadcast_to(dec[0:1], (HEAD_DIM, D_STATE))
                    + upd[g][u0:u0 + HEAD_DIM])

    def per_group(i, carry):
        seqs = [i * SAMPLE_UNROLL + slot for slot in range(SAMPLE_UNROLL)]
        parts = [state_matmuls(b) for b in seqs]
        scored = [attention_scores(b, slot, parts[slot][0]) for slot, b in enumerate(seqs)]
        for slot, b in enumerate(seqs):
            store_state(b, *parts[slot])
        weights = [_attention_rows_probs(scores, sinks_ref) for scores, _ in scored]
        for slot in range(SAMPLE_UNROLL):
            att = _attention_rows_output(*weights[slot], scored[slot][1])
            _put_lane_blocks(res_scr, D_SSM, parts[slot][0], att)
        return carry

    lax.fori_loop(0, bb // SAMPLE_UNROLL, per_group, 0)

    for t in range(dl):
        own_rows = pl.ds(t, bb, stride=SEQ_ROWS)
        rows_t = slice(t * bb, (t + 1) * bb)
        y = ypart_scr[rows_t, :] + _get_lane_blocks(res_scr, 0, D_SSM, own_rows) * ecum_scr[rows_t, :]
        mix_ref[t, :, 0:D_SSM] = _gated_group_norm(y, z_ref[t], snorm_ref[...])
        mix_ref[t, :, D_SSM:D_SSM + D_ATTN] = _get_lane_blocks(res_scr, D_SSM, D_ATTN, own_rows)


def _mixer_sample(z, xbc, dtr, q, k, v, state_ssm, state_conv, cache_k, cache_v, pw, layer, carried):
    depth = state_ssm.shape[0]
    dl, nb = xbc.shape[0], xbc.shape[1]
    bb = min(SAMPLE_SEQS_PER_STEP, nb)
    assert bb & (bb - 1) == 0 and dl * bb <= SSD_CHUNK and dl <= SEQ_ROWS and bb % SAMPLE_UNROLL == 0
    tok = lambda width: pl.BlockSpec((dl, bb, width), lambda i: (0, i, 0))
    lblk = lambda *dims: pl.BlockSpec((None, bb) + dims, lambda i: (layer, i) + (0,) * len(dims))
    consts = _mixer_consts(pw)
    operands = (z, xbc, dtr, q, k, v, state_ssm, state_conv, cache_k, cache_v) + consts
    assert len(operands) == N_SAMPLE_IN
    seq_cols = D_ATTN + N_SSM_GROUPS * D_STATE + 2 * KV_WIDTH
    return pl.pallas_call(
        _mixer_sample_kernel,
        grid=(nb // bb,),
        in_specs=[tok(D_SSM), tok(CONV_DIM), tok(LANES), tok(D_ATTN), tok(KV_WIDTH), tok(KV_WIDTH),
                  lblk(D_SSM, D_STATE),
                  pl.BlockSpec((None, CONV_W - 1, bb, CONV_DIM), lambda i: (layer, 0, i, 0)),
                  lblk(WINDOW, KV_WIDTH), lblk(WINDOW, KV_WIDTH)]
                 + [_layer_block(cst, layer) for cst in consts]
                 + [pl.BlockSpec(memory_space=pl.ANY)] * len(carried),
        out_specs=[tok(D_SSM + D_ATTN), lblk(D_SSM, D_STATE), lblk(WINDOW, KV_WIDTH), lblk(WINDOW, KV_WIDTH)],
        out_shape=[jax.ShapeDtypeStruct((dl, nb, D_SSM + D_ATTN), F32),
                   jax.ShapeDtypeStruct((depth, nb, D_SSM, D_STATE), F32),
                   jax.ShapeDtypeStruct((depth, nb, WINDOW, KV_WIDTH), F32),
                   jax.ShapeDtypeStruct((depth, nb, WINDOW, KV_WIDTH), F32)],
        input_output_aliases={N_SAMPLE_IN + n: 1 + n for n in range(len(carried))},
        scratch_shapes=[pltpu.VMEM((dl * bb, D_SSM), F32),
                        pltpu.VMEM((dl * bb, D_SSM), F32),
                        pltpu.VMEM((bb, D_SSM), F32),
                        pltpu.VMEM((SSD_CHUNK, D_SSM), F32),
                        pltpu.VMEM((D_SSM, SSD_CHUNK), BF16),
                        pltpu.VMEM((SSD_CHUNK, N_SSM_GROUPS * D_STATE), F32),
                        pltpu.VMEM((seq_cols // LANES, bb * SEQ_ROWS, LANES), F32),
                        pltpu.VMEM(((D_SSM + D_ATTN) // LANES, bb * SEQ_ROWS, LANES), F32),
                        pltpu.VMEM((SAMPLE_UNROLL, KEY_PAD, KV_WIDTH), F32),
                        pltpu.VMEM((SAMPLE_UNROLL, KEY_PAD, KV_WIDTH), F32)],
        compiler_params=pltpu.CompilerParams(dimension_semantics=("arbitrary",),
                                             vmem_limit_bytes=VMEM_LIMIT),
        name="mixer_sample",
    )(*operands, *carried)


def _cast_kernel(w_ref, o_ref):
    o_ref[...] = w_ref[...].astype(o_ref.dtype)


def _to_bf16(w):
    depth, rows, cols = w.shape
    bf16_rows = 2 * SUBLANES
    slab = max(s for s in range(bf16_rows, min(rows, CAST_ROWS) + 1, bf16_rows) if rows % s == 0)
    spec = pl.BlockSpec((None, slab, cols), lambda l, i: (l, i, 0))
    return pl.pallas_call(
        _cast_kernel,
        grid=(depth, rows // slab),
        in_specs=[spec],
        out_specs=spec,
        out_shape=jax.ShapeDtypeStruct(w.shape, BF16),
        compiler_params=pltpu.CompilerParams(dimension_semantics=("arbitrary", "arbitrary"),
                                             vmem_limit_bytes=VMEM_LIMIT),
        name="to_bf16",
    )(w)


def _prepare_weights(p):
    bf = lambda w: w.astype(BF16)
    vec = lambda v: v[:, None, :]

    rep = lambda v: jnp.repeat(v, HEAD_DIM, axis=-1)[:, None, :]
    lane_pad = lambda v: jnp.pad(v, ((0, 0), (0, LANES - v.shape[-1])))[:, None, :]
    cuts = np.cumsum([D_SSM, CONV_DIM, N_SSM_HEADS, D_ATTN, KV_WIDTH])
    wz, wxbc, wdt, wq, wk, wv = jnp.split(bf(p['w_in']), cuts, axis=2)
    depth = p['w_in'].shape[0]
    seg = np.kron(np.eye(D_ATTN // HEAD_DIM, dtype=np.float32), np.ones((HEAD_DIM, HEAD_DIM), np.float32))
    heads = np.eye(LANES, dtype=np.float32)[:, :N_SSM_HEADS]
    expand = np.tile(np.kron(heads, np.ones((1, HEAD_DIM), np.float32)), (3, 1))
    return {
        'g_ffn1': vec(p['g_ffn1']), 'g_mix': vec(p['g_mix']), 'g_ffn2': vec(p['g_ffn2']), 'g_ple': vec(p['g_ple']),
        'w1_a': _to_bf16(p['w1_a']), 'w3_a': _to_bf16(p['w3_a']), 'w2_a': _to_bf16(p['w2_a']),
        'w1_b': _to_bf16(p['w1_b']), 'w3_b': _to_bf16(p['w3_b']), 'w2_b': _to_bf16(p['w2_b']),
        'w_in': jnp.concatenate(
            [wz, wxbc, wq, wk, wv, jnp.pad(wdt, ((0, 0), (0, 0), (0, LANES - N_SSM_HEADS)))], axis=2),
        'w_out': bf(p['w_out']), 'w_ple_gate': bf(p['w_ple_gate']), 'w_ple_proj': bf(p['w_ple_proj']),
        'q_gain': vec(jnp.tile(p['q_norm'], (1, N_HEADS))),
        'k_gain': vec(jnp.tile(p['k_norm'], (1, N_KV_HEADS))),
        'seg': jnp.broadcast_to(jnp.asarray(seg, BF16), (depth,) + seg.shape),
        'conv_w': p['conv_w'], 'conv_b': vec(p['conv_b']),
        'dt_bias': lane_pad(p['dt_bias']), 'a_log_rep': rep(p['a_log']),
        'd_skip': rep(p['d_skip']),
        'head_expand': jnp.broadcast_to(jnp.asarray(expand, BF16), (depth,) + expand.shape),
        'ssm_norm': vec(p['ssm_norm']),
        'sinks': jnp.broadcast_to(p['sinks'][:, :, None], (depth, N_HEADS, LANES)),
    }


def kernel(x_prompt, x_sample, state_ssm, state_conv, cache_k_win, cache_v_win, p_prompt, p_sample, g_ffn1, w1_a, w3_a, w2_a, g_mix, w_in, conv_w, conv_b, dt_bias, a_log, d_skip, ssm_norm, q_norm, k_norm, sinks, w_out, g_ffn2, w1_b, w3_b, w2_b, g_ple, w_ple_gate, w_ple_proj):
    params = dict(g_ffn1=g_ffn1, w1_a=w1_a, w3_a=w3_a, w2_a=w2_a, g_mix=g_mix, w_in=w_in, conv_w=conv_w,
                  conv_b=conv_b, dt_bias=dt_bias, a_log=a_log, d_skip=d_skip, ssm_norm=ssm_norm, q_norm=q_norm,
                  k_norm=k_norm, sinks=sinks, w_out=w_out, g_ffn2=g_ffn2, w1_b=w1_b, w3_b=w3_b, w2_b=w2_b,
                  g_ple=g_ple, w_ple_gate=w_ple_gate, w_ple_proj=w_ple_proj)
    depth = w_in.shape[0]
    bp, seq, _ = x_prompt.shape
    bs, dl, _ = x_sample.shape
    assert seq % SSD_CHUNK == 0 and seq >= WINDOW and dl >= CONV_W - 1 and dl <= SUBLANES
    assert (bp * seq) % TOKEN_TILE == 0 and (bs * dl) % min(TOKEN_TILE, bs * dl) == 0

    pw = _prepare_weights(params)
    sst_in = state_ssm.reshape(depth, bs, D_SSM, D_STATE)
    ck_in = cache_k_win.reshape(depth, bs, WINDOW, KV_WIDTH)
    cv_in = cache_v_win.reshape(depth, bs, WINDOW, KV_WIDTH)
    sconv_in = jnp.swapaxes(state_conv, 1, 2)
    pe_p = p_prompt.reshape(depth, bp * seq, D_PLE)
    pe_s = jnp.swapaxes(p_sample, 1, 2).reshape(depth, dl * bs, D_PLE)
    hp = x_prompt.reshape(bp * seq, D_MODEL)
    hs = jnp.swapaxes(x_sample, 0, 1).reshape(dl * bs, D_MODEL)
    ssm_p, conv_p, k_p, v_p, conv_s = [], [], [], [], []
    carried = ()
    for l in range(depth):
        hp, zp, xbcp, dtp, qp, kp, vp = _stage_a(hp, pw, l)
        hs, zs, xbcs, dts, qs, ks, vs = _stage_a(hs, pw, l)
        mixp, sp = _mixer_prompt(zp, xbcp, dtp, qp, kp, vp, pw, l, bp, seq)
        hp = _stage_c(hp, mixp, pe_p, pw, l)
        tm3 = lambda a: a.reshape(dl, bs, a.shape[-1])
        mixs, *carried = _mixer_sample(tm3(zs), tm3(xbcs), tm3(dts), tm3(qs), tm3(ks), tm3(vs),
                                       sst_in, sconv_in, ck_in, cv_in, pw, l, tuple(carried))
        hs = _stage_c(hs, mixs.reshape(dl * bs, D_MODEL), pe_s, pw, l)
        ssm_p.append(sp.reshape(bp, N_SSM_HEADS, HEAD_DIM, D_STATE))
        conv_p.append(xbcp.reshape(bp, seq, CONV_DIM)[:, seq - (CONV_W - 1):])
        last_window = lambda a: a.reshape(bp, seq, KV_WIDTH)[:, seq - WINDOW:].reshape(
            bp, WINDOW, N_KV_HEADS, HEAD_DIM)
        k_p.append(last_window(kp))
        v_p.append(last_window(vp))
        conv_s.append(jnp.swapaxes(tm3(xbcs)[dl - (CONV_W - 1):], 0, 1))
    ssm_s, k_s, v_s = carried
    return (hp.reshape(bp, seq, D_MODEL), jnp.swapaxes(hs.reshape(dl, bs, D_MODEL), 0, 1),
            jnp.stack(ssm_p), jnp.stack(conv_p), jnp.stack(k_p), jnp.stack(v_p),
            ssm_s.reshape(depth, bs, N_SSM_HEADS, HEAD_DIM, D_STATE), jnp.stack(conv_s),
            k_s.reshape(depth, bs, WINDOW, N_KV_HEADS, HEAD_DIM),
            v_s.reshape(depth, bs, WINDOW, N_KV_HEADS, HEAD_DIM))
```

```python
import functools

import numpy as np
import jax
import jax.numpy as jnp
from jax import lax
from jax.experimental import pallas as pl
from jax.experimental.pallas import tpu as pltpu

F32 = jnp.float32
BF16 = jnp.bfloat16

D_MODEL = 1024
HEAD_DIM = 64
D_SSM = 512
N_SSM_HEADS = 8
N_SSM_GROUPS = 2
D_STATE = 128
CONV_W = 4
CONV_DIM = D_SSM + 2 * N_SSM_GROUPS * D_STATE
SSD_CHUNK = 128
D_ATTN = 512
N_HEADS = 8
N_KV_HEADS = 2
Q_PER_KV = N_HEADS // N_KV_HEADS
KV_WIDTH = N_KV_HEADS * HEAD_DIM
WINDOW = 128
D_FF = 2752
D_PLE = 256
RMS_EPS = 1e-6

LANES = 128
SUBLANES = 8
MXU_DIM = 256
FF_CHUNK = MXU_DIM
FF_FULL_CHUNKS = D_FF // FF_CHUNK
FF_TAIL = D_FF - FF_FULL_CHUNKS * FF_CHUNK
FF_TAIL_PAD = -(-FF_TAIL // MXU_DIM) * MXU_DIM
TOKEN_TILE = 512
MIXER_TILE = 512
SAMPLE_SEQS_PER_STEP = 16
SAMPLE_UNROLL = 4
SEQ_ROWS = SUBLANES
KEY_PAD = 2 * WINDOW
VMEM_LIMIT = 56 * 1024 * 1024

OFF_Z = 0
OFF_XBC = OFF_Z + D_SSM
OFF_Q = OFF_XBC + CONV_DIM
OFF_K = OFF_Q + D_ATTN
OFF_V = OFF_K + KV_WIDTH
OFF_DT = OFF_V + KV_WIDTH
D_PROJ_PAD = OFF_DT + LANES

ALIBI_SLOPES = tuple(float(s) for s in np.power(
    np.float32(2.0), -8.0 * np.arange(1, N_HEADS + 1, dtype=np.float32) / N_HEADS))

_NT = (((1,), (1,)), ((), ()))


def _mm(a, b):
    return jnp.dot(a, b, preferred_element_type=F32)


def _mm_nt(a, b):
    return lax.dot_general(a, b, _NT, preferred_element_type=F32)


def _sigmoid(x):
    return 1.0 / (1.0 + jnp.exp(-x))


def _silu(x):
    return x * _sigmoid(x)


def _softplus(x):
    return jnp.maximum(x, 0.0) + jnp.log(1.0 + jnp.exp(-jnp.abs(x)))


def _rmsnorm(x, g):
    return x * lax.rsqrt(jnp.mean(x * x, axis=-1, keepdims=True) + RMS_EPS) * g


def _split3(x):
    hi = x.astype(BF16)
    rest = x - hi.astype(F32)
    mid = rest.astype(BF16)
    return hi, mid, (rest - mid.astype(F32)).astype(BF16)


def _stage_ff_tail(w1_ref, w3_ref, w2_ref, tail_refs):
    w1t_ref, w3t_ref, w2t_ref = tail_refs
    lo = FF_FULL_CHUNKS * FF_CHUNK
    for dst, src in ((w1t_ref, w1_ref), (w3t_ref, w3_ref)):
        dst[...] = jnp.zeros_like(dst)
        dst[:, 0:FF_TAIL] = src[:, lo:D_FF]
    w2t_ref[...] = jnp.zeros_like(w2t_ref)
    w2t_ref[0:FF_TAIL, :] = w2_ref[lo:D_FF, :]


def _ff_tail_scratch():
    return [pltpu.VMEM((D_MODEL, FF_TAIL_PAD), BF16), pltpu.VMEM((D_MODEL, FF_TAIL_PAD), BF16),
            pltpu.VMEM((FF_TAIL_PAD, D_MODEL), BF16)]


def _swiglu(xn, w1_ref, w3_ref, w2_ref, tail_refs):
    def chunk(acc, w1c, w3c, w2c):
        a = _mm(xn, w1c)
        b = _mm(xn, w3c)
        return acc + _mm((_silu(a) * b).astype(BF16), w2c)

    acc = jnp.zeros((xn.shape[0], D_MODEL), F32)
    for c in range(FF_FULL_CHUNKS):
        sl = slice(c * FF_CHUNK, (c + 1) * FF_CHUNK)
        acc = chunk(acc, w1_ref[:, sl], w3_ref[:, sl], w2_ref[sl, :])
    if FF_TAIL:
        acc = chunk(acc, *(r[...] for r in tail_refs))
    return acc


def _segment_sumsq(x, seg_ref):
    hi, mid, _ = _split3(x * x)
    seg = seg_ref[0:x.shape[1], 0:x.shape[1]]
    return _mm(hi, seg) + _mm(mid, seg)


def _stage_a_kernel(x_ref, g1_ref, w1_ref, w3_ref, w2_ref, gm_ref, win_ref, qg_ref, kg_ref, seg_ref,
                    h_ref, z_ref, xbc_ref, dt_ref, q_ref, k_ref, v_ref, *tail_refs):
    @pl.when(pl.program_id(0) == 0)
    def _():
        _stage_ff_tail(w1_ref, w3_ref, w2_ref, tail_refs)

    x = x_ref[...]
    h = x + 0.5 * _swiglu(_rmsnorm(x, g1_ref[...]).astype(BF16), w1_ref, w3_ref, w2_ref, tail_refs)
    h_ref[...] = h
    proj = _mm(_rmsnorm(h, gm_ref[...]).astype(BF16), win_ref[...])
    z_ref[...] = proj[:, OFF_Z:OFF_XBC]
    xbc_ref[...] = proj[:, OFF_XBC:OFF_Q]
    dt_ref[...] = proj[:, OFF_DT:D_PROJ_PAD]
    q = proj[:, OFF_Q:OFF_K]
    k = proj[:, OFF_K:OFF_V]
    q_ref[...] = q * lax.rsqrt(_segment_sumsq(q, seg_ref) * (1.0 / HEAD_DIM) + RMS_EPS) * qg_ref[...]
    k_ref[...] = k * lax.rsqrt(_segment_sumsq(k, seg_ref) * (1.0 / HEAD_DIM) + RMS_EPS) * kg_ref[...]
    v_ref[...] = proj[:, OFF_V:OFF_DT]


def _layer_block(arr, layer, **kwargs):
    return pl.BlockSpec((None,) + arr.shape[1:], lambda *_: (layer, 0, 0), **kwargs)


def _layer_resident(arr, layer):
    return _layer_block(arr, layer, pipeline_mode=pl.Buffered(1))


def _stage_a(x, pw, layer):
    n = x.shape[0]
    tm = min(TOKEN_TILE, n)
    row = lambda width: pl.BlockSpec((tm, width), lambda i: (i, 0))
    widths = (D_MODEL, D_SSM, CONV_DIM, LANES, D_ATTN, KV_WIDTH, KV_WIDTH)
    consts = (pw['g_ffn1'], pw['w1_a'], pw['w3_a'], pw['w2_a'], pw['g_mix'], pw['w_in'],
              pw['q_gain'], pw['k_gain'], pw['seg'])
    return pl.pallas_call(
        _stage_a_kernel,
        grid=(n // tm,),
        in_specs=[row(D_MODEL)] + [_layer_resident(c, layer) for c in consts],
        out_specs=[row(w) for w in widths],
        out_shape=[jax.ShapeDtypeStruct((n, w), F32) for w in widths],
        scratch_shapes=_ff_tail_scratch(),
        compiler_params=pltpu.CompilerParams(dimension_semantics=("arbitrary",),
                                             vmem_limit_bytes=VMEM_LIMIT),
        name="stage_a",
    )(x, *consts)


def _stage_c_kernel(h_ref, mix_ref, pe_ref, wout_ref, g2_ref, w1_ref, w3_ref, w2_ref, gp_ref,
                    wgate_ref, wproj_ref, o_ref, *tail_refs):
    @pl.when(pl.program_id(0) == 0)
    def _():
        _stage_ff_tail(w1_ref, w3_ref, w2_ref, tail_refs)

    o_ref[...] = _stage_c_compute(h_ref[...], mix_ref[...].astype(BF16), pe_ref, wout_ref, g2_ref, w1_ref,
                                  w3_ref, w2_ref, gp_ref, wgate_ref, wproj_ref, tail_refs)


def _stage_c_compute(h, mix, pe_ref, wout_ref, g2_ref, w1_ref, w3_ref, w2_ref, gp_ref, wgate_ref, wproj_ref,
                     tail_refs):
    h = h + _mm(mix, wout_ref[...])
    h = h + 0.5 * _swiglu(_rmsnorm(h, g2_ref[...]).astype(BF16), w1_ref, w3_ref, w2_ref, tail_refs)
    gate = _sigmoid(_mm(_rmsnorm(h, gp_ref[...]).astype(BF16), wgate_ref[...]))
    return h + gate * _mm(pe_ref[...].astype(BF16), wproj_ref[...])


def _stage_c(h, mix, pe, pw, layer):
    n = h.shape[0]
    tm = min(2 * TOKEN_TILE, n)
    row = lambda width: pl.BlockSpec((tm, width), lambda i: (i, 0))
    consts = (pw['w_out'], pw['g_ffn2'], pw['w1_b'], pw['w3_b'], pw['w2_b'], pw['g_ple'],
              pw['w_ple_gate'], pw['w_ple_proj'])
    return pl.pallas_call(
        _stage_c_kernel,
        grid=(n // tm,),
        in_specs=[row(D_MODEL), row(D_MODEL), pl.BlockSpec((None, tm, D_PLE), lambda i: (layer, i, 0))]
                 + [_layer_resident(c, layer) for c in consts],
        out_specs=row(D_MODEL),
        out_shape=jax.ShapeDtypeStruct((n, D_MODEL), F32),
        scratch_shapes=_ff_tail_scratch(),
        compiler_params=pltpu.CompilerParams(dimension_semantics=("arbitrary",),
                                             vmem_limit_bytes=VMEM_LIMIT),
        name="stage_c",
    )(h, mix, pe, *consts)


def _lane_low_half(shape):
    return lax.broadcasted_iota(jnp.int32, shape, len(shape) - 1) < HEAD_DIM


def _split_pair(x):
    rolled = pltpu.roll(x, HEAD_DIM, 1)
    low = _lane_low_half(x.shape)
    return jnp.where(low, x, rolled), jnp.where(low, rolled, x)


def _head_query_rows(q, g):
    low = _lane_low_half((q.shape[0], LANES))
    blocks = []
    for jj in range(Q_PER_KV // 2):
        qp = q[:, (g * 2 + jj) * LANES:(g * 2 + jj + 1) * LANES]
        blocks += [jnp.where(low, qp, 0.0), jnp.where(low, 0.0, qp)]
    return jnp.concatenate(blocks, axis=0).astype(BF16)


def _attention_rows_scores(q, kk, vv):
    low_k = _lane_low_half((KEY_PAD, LANES))
    k_dup = _split_pair(kk)
    v_roll = pltpu.roll(vv, HEAD_DIM, 1)
    v_even = (jnp.where(low_k, vv, 1.0).astype(BF16), jnp.where(low_k, v_roll, 1.0).astype(BF16))
    v_odd = (jnp.where(low_k, 1.0, v_roll).astype(BF16), jnp.where(low_k, 1.0, vv).astype(BF16))
    scores = [_mm_nt(_head_query_rows(q, g), k_dup[g].astype(BF16)) for g in range(N_KV_HEADS)]
    return scores, (v_even, v_odd)


def _attention_rows_probs(scores, sinks_ref):
    r = scores[0].shape[0] // Q_PER_KV
    t_i = lax.broadcasted_iota(jnp.int32, (r, KEY_PAD), 0)
    s_i = lax.broadcasted_iota(jnp.int32, (r, KEY_PAD), 1)
    rel = t_i + WINDOW - s_i
    valid = (rel >= 0) & (rel < WINDOW)
    relf = rel.astype(F32)
    probs, sink_terms = [], []
    for head in range(N_HEADS):
        g, hh = divmod(head, Q_PER_KV)
        sh = jnp.where(valid, scores[g][hh * r:(hh + 1) * r] - ALIBI_SLOPES[head] * relf, -jnp.inf)
        sink = sinks_ref[head:head + 1, 0:1]
        m = jnp.maximum(jnp.max(sh, axis=-1, keepdims=True), sink)
        probs.append(jnp.exp(sh - m))
        sink_terms.append(jnp.broadcast_to(jnp.exp(sink - m), (r, LANES)))
    return probs, sink_terms


def _attention_rows_output(probs, sink_terms, v_operands):
    v_even, v_odd = v_operands
    r = probs[0].shape[0]
    low = _lane_low_half((r, LANES))
    outs = []
    for g in range(N_KV_HEADS):
        mine = probs[g * Q_PER_KV:(g + 1) * Q_PER_KV]
        o_even = _mm(jnp.concatenate(mine[0::2], axis=0).astype(BF16), v_even[g])
        o_odd = _mm(jnp.concatenate(mine[1::2], axis=0).astype(BF16), v_odd[g])
        for jj in range(Q_PER_KV // 2):
            oe, oo = o_even[jj * r:(jj + 1) * r], o_odd[jj * r:(jj + 1) * r]
            head = g * Q_PER_KV + 2 * jj
            den = (pltpu.roll(jnp.where(low, oo, oe), HEAD_DIM, 1)
                   + jnp.where(low, sink_terms[head], sink_terms[head + 1]))
            outs.append(jnp.where(low, oe, oo) / den)
    return jnp.concatenate(outs, axis=1)


def _fill_attention_bias(bias_scr):
    s_i = lax.broadcasted_iota(jnp.int32, (KEY_PAD, WINDOW), 0)
    t_i = lax.broadcasted_iota(jnp.int32, (KEY_PAD, WINDOW), 1)
    rel = t_i + WINDOW - s_i
    valid = (rel >= 0) & (rel < WINDOW)
    relf = rel.astype(F32)
    for head in range(N_HEADS):
        bias_scr[head * KEY_PAD:(head + 1) * KEY_PAD, :] = jnp.where(valid, -ALIBI_SLOPES[head] * relf, -jnp.inf)


def _attention_cols_scores(q, kk):
    k_dup = _split_pair(kk)
    qs = q * (HEAD_DIM ** -0.5)
    return [_mm_nt(k_dup[g].astype(BF16), _head_query_rows(qs, g)) for g in range(N_KV_HEADS)]


def _attention_cols_probs(scores, bias_scr, sinks_ref, first_block):
    r = scores[0].shape[1] // Q_PER_KV
    mask_prev = None if first_block is None else jnp.where(first_block, -jnp.inf, 0.0)
    out = []
    for g in range(N_KV_HEADS):
        probs, sink_terms = [], []
        for hh in range(Q_PER_KV):
            head = g * Q_PER_KV + hh
            sh = scores[g][:, hh * r:(hh + 1) * r] + bias_scr[head * KEY_PAD:(head + 1) * KEY_PAD, :]
            prev = sh[0:WINDOW] if mask_prev is None else sh[0:WINDOW] + mask_prev
            cur = sh[WINDOW:KEY_PAD]
            sink = sinks_ref[head:head + 1, :]
            m = jnp.maximum(jnp.maximum(jnp.max(prev, axis=0, keepdims=True),
                                        jnp.max(cur, axis=0, keepdims=True)), sink)
            probs.append(jnp.concatenate([jnp.exp(prev - m), jnp.exp(cur - m)], axis=0))
            sink_terms.append(jnp.exp(sink - m))
        out.append((jnp.concatenate(probs, axis=1).astype(BF16), jnp.concatenate(sink_terms, axis=1)))
    return out


def _attention_cols_pv(probs, vv):
    vt = vv.T
    ones = jnp.ones((HEAD_DIM, KEY_PAD), F32)
    return [_mm(jnp.concatenate([vt[g * HEAD_DIM:(g + 1) * HEAD_DIM], ones], axis=0).astype(BF16), probs[g][0])
            for g in range(N_KV_HEADS)]


def _attention_cols_output(pv, probs):
    r = pv[0].shape[1] // Q_PER_KV
    outs = []
    for g in range(N_KV_HEADS):
        den = pv[g][HEAD_DIM:HEAD_DIM + 1, :] + probs[g][1]
        on = pv[g][0:HEAD_DIM, :] * (1.0 / den)
        for jj in range(Q_PER_KV // 2):
            pair = jnp.concatenate([on[:, (2 * jj) * r:(2 * jj + 1) * r],
                                    on[:, (2 * jj + 1) * r:(2 * jj + 2) * r]], axis=0)
            outs.append(pair.T)
    return jnp.concatenate(outs, axis=1)


def _gated_group_norm(y, z, gain):
    y = y * _silu(z)
    gw = D_SSM // N_SSM_GROUPS
    parts = [_rmsnorm(y[:, g * gw:(g + 1) * gw], gain[:, g * gw:(g + 1) * gw]) for g in range(N_SSM_GROUPS)]
    return jnp.concatenate(parts, axis=1)


def _prompt_tile_mixer(first, z_ref, xbc_ref, xprev_ref, dt_ref, q_ref, k_ref, kprev_ref, v_ref, vprev_ref,
                       cw_ref, cb_ref, dtb_ref, alog_ref, dskip_ref, snorm_ref, sinks_ref, expand_ref,
                       st_scr, xe_scr, bias_scr, store_mix, state_after):
    t = SSD_CHUNK
    rows = xbc_ref.shape[0]
    hw = D_SSM // N_SSM_GROUPS
    nbc = N_SSM_GROUPS * D_STATE

    dt_raw = _mm(jnp.concatenate(_split3(dt_ref[...] + dtb_ref[...]), axis=1), expand_ref[...])

    xe_scr[0:SUBLANES, :] = jnp.where(first, 0.0, xprev_ref[...])
    xe_scr[SUBLANES:SUBLANES + rows, :] = xbc_ref[...]
    cw = cw_ref[...]
    a_rep = -jnp.exp(alog_ref[...])
    causal = lax.broadcasted_iota(jnp.int32, (t, t), 0) >= lax.broadcasted_iota(jnp.int32, (t, t), 1)
    ltri = causal.astype(BF16)
    low_b = _lane_low_half((t, LANES))
    chunks = []
    for c in range(rows // t):
        sl = slice(c * t, (c + 1) * t)
        conv = cb_ref[...] + cw[CONV_W - 1:CONV_W] * xbc_ref[sl, :]
        for back in range(1, CONV_W):
            r0 = SUBLANES - back + c * t
            conv = conv + cw[CONV_W - 1 - back:CONV_W - back] * xe_scr[r0:r0 + t, :]
        xc = _silu(conv)
        dt = _softplus(dt_raw[sl])
        chunks.append(dict(
            sl=sl, dt=dt, xs=xc[:, 0:D_SSM], bm=xc[:, D_SSM:D_SSM + nbc], cm=xc[:, D_SSM + nbc:CONV_DIM],
            cum3=_mm(ltri, jnp.concatenate(_split3(dt * a_rep), axis=1))))
        yield

    for c, ch in enumerate(chunks):
        sl, xs, bm, cm = ch['sl'], ch['xs'], ch['bm'], ch['cm']
        cum3 = ch['cum3']
        cum = cum3[:, 0:D_SSM] + cum3[:, D_SSM:2 * D_SSM] + cum3[:, 2 * D_SSM:3 * D_SSM]
        xdt = xs * ch['dt']
        xw_b = (xdt * jnp.exp(cum[t - 1:t, :] - cum)).astype(BF16)
        cg_b = [cm[:, g * D_STATE:(g + 1) * D_STATE].astype(BF16) for g in range(N_SSM_GROUPS)]
        bg = [bm[:, g * D_STATE:(g + 1) * D_STATE] for g in range(N_SSM_GROUPS)]
        k_before = kprev_ref[...] if c == 0 else k_ref[(c - 1) * t:c * t, :]
        v_before = vprev_ref[...] if c == 0 else v_ref[(c - 1) * t:c * t, :]
        ch.update(
            cum=cum, ecum=jnp.exp(cum), xdt_b=xdt.astype(BF16), cg_b=cg_b,
            cb_t=[_mm_nt(cg_b[g], bg[g].astype(BF16)) for g in range(N_SSM_GROUPS)],
            st_new=[_mm(bg[g].T.astype(BF16), xw_b[:, g * hw:(g + 1) * hw]) for g in range(N_SSM_GROUPS)],
            att_scores=_attention_cols_scores(q_ref[sl, :], jnp.concatenate([k_before, k_ref[sl, :]], axis=0)),
            vv=jnp.concatenate([v_before, v_ref[sl, :]], axis=0))
        yield

    st = jnp.where(first, 0.0, st_scr[...])
    for c, ch in enumerate(chunks):
        y_in = []
        for j in range(D_SSM // LANES):
            cb_t = ch['cb_t'][j // (hw // LANES)]
            scores = []
            for col in _split_pair(ch['cum'][:, j * LANES:(j + 1) * LANES]):
                decay = jnp.exp(jnp.where(causal, col - col.T, -jnp.inf))
                scores.append((cb_t * decay).astype(BF16))
            xp = ch['xdt_b'][:, j * LANES:(j + 1) * LANES]
            zero = jnp.zeros_like(xp)
            rhs = jnp.concatenate([jnp.where(low_b, xp, zero), jnp.where(low_b, zero, xp)], axis=0)
            y_in.append(_mm(jnp.concatenate(scores, axis=1), rhs))
        yield

        probs = _attention_cols_probs(ch['att_scores'], bias_scr, sinks_ref, first if c == 0 else None)
        pv = _attention_cols_pv(probs, ch['vv'])
        st_b = st.astype(BF16)
        y_off = [_mm(ch['cg_b'][g], st_b[:, g * hw:(g + 1) * hw]) for g in range(N_SSM_GROUPS)]
        yield

        y = (jnp.concatenate(y_in, axis=1) + jnp.concatenate(y_off, axis=1) * ch['ecum']
             + dskip_ref[...] * ch['xs'])
        st = st * ch['ecum'][t - 1:t, :] + jnp.concatenate(ch['st_new'], axis=1)
        ssd_out = _gated_group_norm(y, z_ref[ch['sl'], :], snorm_ref[...])
        yield

        store_mix(ch['sl'], ssd_out, _attention_cols_output(pv, probs))
        yield
    st_scr[...] = st
    state_after.append(st)


def _mixer_prompt_kernel(*refs, tiles_per_seq):
    mix_ref, state_ref, st_scr, xe_scr, bias_scr = refs[-5:]
    j = pl.program_id(0)

    @pl.when(j == 0)
    def _():
        _fill_attention_bias(bias_scr)

    def store_mix(rows, ssd_out, attn_out):
        mix_ref[rows, 0:D_SSM] = ssd_out.astype(mix_ref.dtype)
        mix_ref[rows, D_SSM:D_SSM + D_ATTN] = attn_out.astype(mix_ref.dtype)

    state_after = []
    for _ in _prompt_tile_mixer(lax.rem(j, tiles_per_seq) == 0, *refs[:-5], st_scr, xe_scr, bias_scr,
                                store_mix, state_after):
        pass
    st = state_after[0]

    @pl.when(lax.rem(j, tiles_per_seq) == tiles_per_seq - 1)
    def _():
        for jb in range(D_SSM // LANES):
            state_ref[jb * LANES:(jb + 1) * LANES, :] = st[:, jb * LANES:(jb + 1) * LANES].T


def _mixer_consts(pw):
    return (pw['conv_w'], pw['conv_b'], pw['dt_bias'], pw['a_log_rep'], pw['d_skip'], pw['ssm_norm'], pw['sinks'],
            pw['head_expand'])


def _mixer_prompt(z, xbc, dtr, q, k, v, pw, layer, batch, seq):
    t = SSD_CHUNK
    tile = MIXER_TILE
    tiles_per_seq = seq // tile
    cur = lambda width: pl.BlockSpec((tile, width), lambda i: (i, 0))
    before = lambda rows, width: pl.BlockSpec((rows, width), lambda i: (jnp.maximum(i * (tile // rows) - 1, 0), 0))
    consts = _mixer_consts(pw)
    return pl.pallas_call(
        functools.partial(_mixer_prompt_kernel, tiles_per_seq=tiles_per_seq),
        grid=(batch * tiles_per_seq,),
        in_specs=[cur(D_SSM), cur(CONV_DIM), before(SUBLANES, CONV_DIM), cur(LANES), cur(D_ATTN),
                  cur(KV_WIDTH), before(t, KV_WIDTH), cur(KV_WIDTH), before(t, KV_WIDTH)]
                 + [_layer_block(cst, layer) for cst in consts],
        out_specs=[pl.BlockSpec((tile, D_SSM + D_ATTN), lambda i: (i, 0)),
                   pl.BlockSpec((None, D_SSM, D_STATE), lambda i: (i // tiles_per_seq, 0, 0))],
        out_shape=[jax.ShapeDtypeStruct((batch * seq, D_SSM + D_ATTN), BF16),
                   jax.ShapeDtypeStruct((batch, D_SSM, D_STATE), F32)],
        scratch_shapes=[pltpu.VMEM((D_STATE, D_SSM), F32),
                        pltpu.VMEM((SUBLANES + tile, CONV_DIM), F32),
                        pltpu.VMEM((N_HEADS * KEY_PAD, WINDOW), F32)],
        compiler_params=pltpu.CompilerParams(dimension_semantics=("arbitrary",),
                                             vmem_limit_bytes=VMEM_LIMIT),
        name="mixer_prompt",
    )(z, xbc, xbc, dtr, q, k, k, v, v, *consts)


N_SAMPLE_IN = 18


def _put_lane_blocks(scr, col0, rows, val):
    for j in range(val.shape[1] // LANES):
        scr[col0 // LANES + j, rows, :] = val[:, j * LANES:(j + 1) * LANES]


def _get_lane_blocks(scr, col0, width, rows):
    return jnp.concatenate([scr[col0 // LANES + j, rows, :] for j in range(width // LANES)], axis=1)


def _mixer_sample_kernel(*refs):
    (z_ref, xbc_ref, dt_ref, q_ref, k_ref, v_ref, sst_ref, sconv_ref, ck_ref, cv_ref,
     cw_ref, cb_ref, dtb_ref, alog_ref, dskip_ref, snorm_ref, sinks_ref, expand_ref) = refs[:N_SAMPLE_IN]
    (mix_ref, sst_out_ref, ck_out_ref, cv_out_ref,
     ypart_scr, ecum_scr, cdec_scr, xw_scr, xt_scr, bmat_scr, seq_scr, res_scr, kk_scr, vv_scr) = refs[-14:]
    dl, bb = xbc_ref.shape[0], xbc_ref.shape[1]
    hw = D_SSM // N_SSM_GROUPS
    nbc = N_SSM_GROUPS * D_STATE
    c_q, c_c, c_k, c_v = 0, D_ATTN, D_ATTN + nbc, D_ATTN + nbc + KV_WIDTH
    xw_scr[...] = jnp.zeros_like(xw_scr)
    bmat_scr[...] = jnp.zeros_like(bmat_scr)
    seq_scr[...] = jnp.zeros_like(seq_scr)
    kk_scr[...] = jnp.zeros_like(kk_scr)
    vv_scr[...] = jnp.zeros_like(vv_scr)

    cw = cw_ref[...]
    a_rep = -jnp.exp(alog_ref[...])
    xs, bm, cm, xdt, cum = [], [], [], [], []
    for t in range(dl):
        conv = cb_ref[...] + cw[CONV_W - 1:CONV_W] * xbc_ref[t]
        for back in range(1, CONV_W):
            src = xbc_ref[t - back] if t >= back else sconv_ref[CONV_W - 1 + t - back]
            conv = conv + cw[CONV_W - 1 - back:CONV_W - back] * src
        xc = _silu(conv)
        xs.append(xc[:, 0:D_SSM])
        bm.append(xc[:, D_SSM:D_SSM + nbc])
        cm.append(xc[:, D_SSM + nbc:CONV_DIM])
        dt_c = _softplus(dt_ref[t] + dtb_ref[...])
        dt = _mm(jnp.concatenate(_split3(dt_c), axis=1), expand_ref[...])
        cum.append(dt * a_rep if t == 0 else cum[-1] + dt * a_rep)
        xdt.append(xs[t] * dt)
    cdec_scr[...] = jnp.exp(cum[dl - 1])
    for t in range(dl):
        y = dskip_ref[...] * xs[t]
        for s in range(t + 1):
            prod = cm[t] * bm[s]
            dots = jnp.concatenate(
                [jnp.broadcast_to(jnp.sum(prod[:, g * D_STATE:(g + 1) * D_STATE], axis=-1, keepdims=True), (bb, hw))
                 for g in range(N_SSM_GROUPS)], axis=1)
            y = y + (dots * xdt[s] if s == t else dots * jnp.exp(cum[t] - cum[s]) * xdt[s])
        rows_t = slice(t * bb, (t + 1) * bb)
        ypart_scr[rows_t, :] = y
        ecum_scr[rows_t, :] = jnp.exp(cum[t])
        xw_scr[rows_t, :] = xdt[t] * jnp.exp(cum[dl - 1] - cum[t])
        bmat_scr[rows_t, :] = bm[t]
        own_rows = pl.ds(t, bb, stride=SEQ_ROWS)
        _put_lane_blocks(seq_scr, c_q, own_rows, q_ref[t] * (HEAD_DIM ** -0.5))
        _put_lane_blocks(seq_scr, c_c, own_rows, cm[t])
        _put_lane_blocks(seq_scr, c_k, own_rows, k_ref[t])
        _put_lane_blocks(seq_scr, c_v, own_rows, v_ref[t])
    xw = xw_scr[...]
    for j in range(D_SSM // LANES):
        xt_scr[j * LANES:(j + 1) * LANES, :] = xw[:, j * LANES:(j + 1) * LANES].T.astype(BF16)

    block_row = lax.broadcasted_iota(jnp.int32, (SSD_CHUNK, D_STATE), 0)

    def state_matmuls(b):
        my_rows = pl.ds(pl.multiple_of(b * SEQ_ROWS, SEQ_ROWS), SEQ_ROWS)
        state = sst_ref[b]
        state_b = state.astype(BF16)
        c_rows = jnp.concatenate([_get_lane_blocks(seq_scr, c_c, nbc, my_rows),
                                  jnp.zeros((SEQ_ROWS, nbc), F32)], axis=0).astype(BF16)
        y_off = [_mm_nt(c_rows[:, g * D_STATE:(g + 1) * D_STATE], state_b[g * hw:(g + 1) * hw, :])[0:SEQ_ROWS]
                 for g in range(N_SSM_GROUPS)]
        is_mine = (block_row & (bb - 1)) == b
        upd = [_mm(xt_scr[g * hw:(g + 1) * hw, :],
                   jnp.where(is_mine, bmat_scr[:, g * D_STATE:(g + 1) * D_STATE], 0.0).astype(BF16))
               for g in range(N_SSM_GROUPS)]
        return my_rows, state, y_off, upd

    def attention_scores(b, slot, my_rows):
        kk_scr[slot, 0:WINDOW, :] = ck_ref[b]
        kk_scr[slot, WINDOW:WINDOW + SEQ_ROWS, :] = _get_lane_blocks(seq_scr, c_k, KV_WIDTH, my_rows)
        vv_scr[slot, 0:WINDOW, :] = cv_ref[b]
        vv_scr[slot, WINDOW:WINDOW + SEQ_ROWS, :] = _get_lane_blocks(seq_scr, c_v, KV_WIDTH, my_rows)
        ck_out_ref[b] = kk_scr[slot, dl:dl + WINDOW, :]
        cv_out_ref[b] = vv_scr[slot, dl:dl + WINDOW, :]
        return _attention_rows_scores(_get_lane_blocks(seq_scr, c_q, D_ATTN, my_rows), kk_scr[slot], vv_scr[slot])

    def store_state(b, my_rows, state, y_off, upd):
        _put_lane_blocks(res_scr, 0, my_rows, jnp.concatenate(y_off, axis=1))
        decay = jnp.broadcast_to(cdec_scr[pl.ds(b, 1), :], (SUBLANES, D_SSM))
        for j in range(D_SSM // LANES):
            g, jj = divmod(j, hw // LANES)
            for half, dec in enumerate(_split_pair(decay[:, j * LANES:(j + 1) * LANES])):
                h0 = j * LANES + half * HEAD_DIM
                u0 = jj * LANES + half * HEAD_DIM
                sst_out_ref[b, h0:h0 + HEAD_DIM, :] = (
                    state[h0:h0 + HEAD_DIM] * jnp.broadcast_to(dec[0:1], (HEAD_DIM, D_STATE))
                    + upd[g][u0:u0 + HEAD_DIM])

    def per_group(i, carry):
        seqs = [i * SAMPLE_UNROLL + slot for slot in range(SAMPLE_UNROLL)]
        parts = [state_matmuls(b) for b in seqs]
        scored = [attention_scores(b, slot, parts[slot][0]) for slot, b in enumerate(seqs)]
        for slot, b in enumerate(seqs):
            store_state(b, *parts[slot])
        weights = [_attention_rows_probs(scores, sinks_ref) for scores, _ in scored]
        for slot in range(SAMPLE_UNROLL):
            att = _attention_rows_output(*weights[slot], scored[slot][1])
            _put_lane_blocks(res_scr, D_SSM, parts[slot][0], att)
        return carry

    lax.fori_loop(0, bb // SAMPLE_UNROLL, per_group, 0)

    for t in range(dl):
        own_rows = pl.ds(t, bb, stride=SEQ_ROWS)
        rows_t = slice(t * bb, (t + 1) * bb)
        y = ypart_scr[rows_t, :] + _get_lane_blocks(res_scr, 0, D_SSM, own_rows) * ecum_scr[rows_t, :]
        mix_ref[t, :, 0:D_SSM] = _gated_group_norm(y, z_ref[t], snorm_ref[...])
        mix_ref[t, :, D_SSM:D_SSM + D_ATTN] = _get_lane_blocks(res_scr, D_SSM, D_ATTN, own_rows)


def _mixer_sample(z, xbc, dtr, q, k, v, state_ssm, state_conv, cache_k, cache_v, pw, layer, carried):
    depth = state_ssm.shape[0]
    dl, nb = xbc.shape[0], xbc.shape[1]
    bb = min(SAMPLE_SEQS_PER_STEP, nb)
    assert bb & (bb - 1) == 0 and dl * bb <= SSD_CHUNK and dl <= SEQ_ROWS and bb % SAMPLE_UNROLL == 0
    tok = lambda width: pl.BlockSpec((dl, bb, width), lambda i: (0, i, 0))
    lblk = lambda *dims: pl.BlockSpec((None, bb) + dims, lambda i: (layer, i) + (0,) * len(dims))
    consts = _mixer_consts(pw)
    operands = (z, xbc, dtr, q, k, v, state_ssm, state_conv, cache_k, cache_v) + consts
    assert len(operands) == N_SAMPLE_IN
    seq_cols = D_ATTN + N_SSM_GROUPS * D_STATE + 2 * KV_WIDTH
    return pl.pallas_call(
        _mixer_sample_kernel,
        grid=(nb // bb,),
        in_specs=[tok(D_SSM), tok(CONV_DIM), tok(LANES), tok(D_ATTN), tok(KV_WIDTH), tok(KV_WIDTH),
                  lblk(D_SSM, D_STATE),
                  pl.BlockSpec((None, CONV_W - 1, bb, CONV_DIM), lambda i: (layer, 0, i, 0)),
                  lblk(WINDOW, KV_WIDTH), lblk(WINDOW, KV_WIDTH)]
                 + [_layer_block(cst, layer) for cst in consts]
                 + [pl.BlockSpec(memory_space=pl.ANY)] * len(carried),
        out_specs=[tok(D_SSM + D_ATTN), lblk(D_SSM, D_STATE), lblk(WINDOW, KV_WIDTH), lblk(WINDOW, KV_WIDTH)],
        out_shape=[jax.ShapeDtypeStruct((dl, nb, D_SSM + D_ATTN), F32),
                   jax.ShapeDtypeStruct((depth, nb, D_SSM, D_STATE), F32),
                   jax.ShapeDtypeStruct((depth, nb, WINDOW, KV_WIDTH), F32),
                   jax.ShapeDtypeStruct((depth, nb, WINDOW, KV_WIDTH), F32)],
        input_output_aliases={N_SAMPLE_IN + n: 1 + n for n in range(len(carried))},
        scratch_shapes=[pltpu.VMEM((dl * bb, D_SSM), F32),
                        pltpu.VMEM((dl * bb, D_SSM), F32),
                        pltpu.VMEM((bb, D_SSM), F32),
                        pltpu.VMEM((SSD_CHUNK, D_SSM), F32),
                        pltpu.VMEM((D_SSM, SSD_CHUNK), BF16),
                        pltpu.VMEM((SSD_CHUNK, N_SSM_GROUPS * D_STATE), F32),
                        pltpu.VMEM((seq_cols // LANES, bb * SEQ_ROWS, LANES), F32),
                        pltpu.VMEM(((D_SSM + D_ATTN) // LANES, bb * SEQ_ROWS, LANES), F32),
                        pltpu.VMEM((SAMPLE_UNROLL, KEY_PAD, KV_WIDTH), F32),
                        pltpu.VMEM((SAMPLE_UNROLL, KEY_PAD, KV_WIDTH), F32)],
        compiler_params=pltpu.CompilerParams(dimension_semantics=("arbitrary",),
                                             vmem_limit_bytes=VMEM_LIMIT),
        name="mixer_sample",
    )(*operands, *carried)


def _prepare_weights(p):
    bf = lambda w: w.astype(BF16)
    vec = lambda v: v[:, None, :]

    rep = lambda v: jnp.repeat(v, HEAD_DIM, axis=-1)[:, None, :]
    lane_pad = lambda v: jnp.pad(v, ((0, 0), (0, LANES - v.shape[-1])))[:, None, :]
    cuts = np.cumsum([D_SSM, CONV_DIM, N_SSM_HEADS, D_ATTN, KV_WIDTH])
    wz, wxbc, wdt, wq, wk, wv = jnp.split(bf(p['w_in']), cuts, axis=2)
    depth = p['w_in'].shape[0]
    seg = np.kron(np.eye(D_ATTN // HEAD_DIM, dtype=np.float32), np.ones((HEAD_DIM, HEAD_DIM), np.float32))
    heads = np.eye(LANES, dtype=np.float32)[:, :N_SSM_HEADS]
    expand = np.tile(np.kron(heads, np.ones((1, HEAD_DIM), np.float32)), (3, 1))
    return {
        'g_ffn1': vec(p['g_ffn1']), 'g_mix': vec(p['g_mix']), 'g_ffn2': vec(p['g_ffn2']), 'g_ple': vec(p['g_ple']),
        'w1_a': bf(p['w1_a']), 'w3_a': bf(p['w3_a']), 'w2_a': bf(p['w2_a']),
        'w1_b': bf(p['w1_b']), 'w3_b': bf(p['w3_b']), 'w2_b': bf(p['w2_b']),
        'w_in': jnp.concatenate(
            [wz, wxbc, wq, wk, wv, jnp.pad(wdt, ((0, 0), (0, 0), (0, LANES - N_SSM_HEADS)))], axis=2),
        'w_out': bf(p['w_out']), 'w_ple_gate': bf(p['w_ple_gate']), 'w_ple_proj': bf(p['w_ple_proj']),
        'q_gain': vec(jnp.tile(p['q_norm'], (1, N_HEADS))),
        'k_gain': vec(jnp.tile(p['k_norm'], (1, N_KV_HEADS))),
        'seg': jnp.broadcast_to(jnp.asarray(seg, BF16), (depth,) + seg.shape),
        'conv_w': p['conv_w'], 'conv_b': vec(p['conv_b']),
        'dt_bias': lane_pad(p['dt_bias']), 'a_log_rep': rep(p['a_log']),
        'd_skip': rep(p['d_skip']),
        'head_expand': jnp.broadcast_to(jnp.asarray(expand, BF16), (depth,) + expand.shape),
        'ssm_norm': vec(p['ssm_norm']),
        'sinks': jnp.broadcast_to(p['sinks'][:, :, None], (depth, N_HEADS, LANES)),
    }


def kernel(x_prompt, x_sample, state_ssm, state_conv, cache_k_win, cache_v_win, p_prompt, p_sample, g_ffn1, w1_a, w3_a, w2_a, g_mix, w_in, conv_w, conv_b, dt_bias, a_log, d_skip, ssm_norm, q_norm, k_norm, sinks, w_out, g_ffn2, w1_b, w3_b, w2_b, g_ple, w_ple_gate, w_ple_proj):
    params = dict(g_ffn1=g_ffn1, w1_a=w1_a, w3_a=w3_a, w2_a=w2_a, g_mix=g_mix, w_in=w_in, conv_w=conv_w,
                  conv_b=conv_b, dt_bias=dt_bias, a_log=a_log, d_skip=d_skip, ssm_norm=ssm_norm, q_norm=q_norm,
                  k_norm=k_norm, sinks=sinks, w_out=w_out, g_ffn2=g_ffn2, w1_b=w1_b, w3_b=w3_b, w2_b=w2_b,
                  g_ple=g_ple, w_ple_gate=w_ple_gate, w_ple_proj=w_ple_proj)
    depth = w_in.shape[0]
    bp, seq, _ = x_prompt.shape
    bs, dl, _ = x_sample.shape
    assert seq % SSD_CHUNK == 0 and seq >= WINDOW and dl >= CONV_W - 1 and dl <= SUBLANES
    assert (bp * seq) % TOKEN_TILE == 0 and (bs * dl) % min(TOKEN_TILE, bs * dl) == 0

    pw = _prepare_weights(params)
    sst_in = state_ssm.reshape(depth, bs, D_SSM, D_STATE)
    ck_in = cache_k_win.reshape(depth, bs, WINDOW, KV_WIDTH)
    cv_in = cache_v_win.reshape(depth, bs, WINDOW, KV_WIDTH)
    sconv_in = jnp.swapaxes(state_conv, 1, 2)
    pe_p = p_prompt.reshape(depth, bp * seq, D_PLE)
    pe_s = jnp.swapaxes(p_sample, 1, 2).reshape(depth, dl * bs, D_PLE)
    hp = x_prompt.reshape(bp * seq, D_MODEL)
    hs = jnp.swapaxes(x_sample, 0, 1).reshape(dl * bs, D_MODEL)
    ssm_p, conv_p, k_p, v_p, conv_s = [], [], [], [], []
    carried = ()
    for l in range(depth):
        hp, zp, xbcp, dtp, qp, kp, vp = _stage_a(hp, pw, l)
        hs, zs, xbcs, dts, qs, ks, vs = _stage_a(hs, pw, l)
        mixp, sp = _mixer_prompt(zp, xbcp, dtp, qp, kp, vp, pw, l, bp, seq)
        hp = _stage_c(hp, mixp, pe_p, pw, l)
        tm3 = lambda a: a.reshape(dl, bs, a.shape[-1])
        mixs, *carried = _mixer_sample(tm3(zs), tm3(xbcs), tm3(dts), tm3(qs), tm3(ks), tm3(vs),
                                       sst_in, sconv_in, ck_in, cv_in, pw, l, tuple(carried))
        hs = _stage_c(hs, mixs.reshape(dl * bs, D_MODEL), pe_s, pw, l)
        ssm_p.append(sp.reshape(bp, N_SSM_HEADS, HEAD_DIM, D_STATE))
        conv_p.append(xbcp.reshape(bp, seq, CONV_DIM)[:, seq - (CONV_W - 1):])
        last_window = lambda a: a.reshape(bp, seq, KV_WIDTH)[:, seq - WINDOW:].reshape(
            bp, WINDOW, N_KV_HEADS, HEAD_DIM)
        k_p.append(last_window(kp))
        v_p.append(last_window(vp))
        conv_s.append(jnp.swapaxes(tm3(xbcs)[dl - (CONV_W - 1):], 0, 1))
    ssm_s, k_s, v_s = carried
    return (hp.reshape(bp, seq, D_MODEL), jnp.swapaxes(hs.reshape(dl, bs, D_MODEL), 0, 1),
            jnp.stack(ssm_p), jnp.stack(conv_p), jnp.stack(k_p), jnp.stack(v_p),
            ssm_s.reshape(depth, bs, N_SSM_HEADS, HEAD_DIM, D_STATE), jnp.stack(conv_s),
            k_s.reshape(depth, bs, WINDOW, N_KV_HEADS, HEAD_DIM),
            v_s.reshape(depth, bs, WINDOW, N_KV_HEADS, HEAD_DIM))
```

```python
import functools

import numpy as np
import jax
import jax.numpy as jnp
from jax import lax
from jax.experimental import pallas as pl
from jax.experimental.pallas import tpu as pltpu

F32 = jnp.float32
BF16 = jnp.bfloat16

D_MODEL = 1024
HEAD_DIM = 64
D_SSM = 512
N_SSM_HEADS = 8
N_SSM_GROUPS = 2
D_STATE = 128
CONV_W = 4
CONV_DIM = D_SSM + 2 * N_SSM_GROUPS * D_STATE
SSD_CHUNK = 128
D_ATTN = 512
N_HEADS = 8
N_KV_HEADS = 2
Q_PER_KV = N_HEADS // N_KV_HEADS
KV_WIDTH = N_KV_HEADS * HEAD_DIM
WINDOW = 128
D_FF = 2752
D_PLE = 256
RMS_EPS = 1e-6

LANES = 128
SUBLANES = 8
MXU_DIM = 256
FF_CHUNK = MXU_DIM
FF_FULL_CHUNKS = D_FF // FF_CHUNK
FF_TAIL = D_FF - FF_FULL_CHUNKS * FF_CHUNK
FF_TAIL_PAD = -(-FF_TAIL // MXU_DIM) * MXU_DIM
TOKEN_TILE = 512
MIXER_TILE = 512
SAMPLE_SEQS_PER_STEP = 16
SAMPLE_UNROLL = 4
SEQ_ROWS = SUBLANES
KEY_PAD = 2 * WINDOW
VMEM_LIMIT = 56 * 1024 * 1024

OFF_Z = 0
OFF_XBC = OFF_Z + D_SSM
OFF_Q = OFF_XBC + CONV_DIM
OFF_K = OFF_Q + D_ATTN
OFF_V = OFF_K + KV_WIDTH
OFF_DT = OFF_V + KV_WIDTH
D_PROJ_PAD = OFF_DT + LANES

ALIBI_SLOPES = tuple(float(s) for s in np.power(
    np.float32(2.0), -8.0 * np.arange(1, N_HEADS + 1, dtype=np.float32) / N_HEADS))

LOG2_E = 1.4426950408889634
_NT = (((1,), (1,)), ((), ()))


def _mm(a, b):
    return jnp.dot(a, b, preferred_element_type=F32)


def _mm_nt(a, b):
    return lax.dot_general(a, b, _NT, preferred_element_type=F32)


def _sigmoid(x):
    return 1.0 / (1.0 + jnp.exp(-x))


def _silu(x):
    return x * _sigmoid(x)


def _softplus(x):
    return jnp.maximum(x, 0.0) + jnp.log(1.0 + jnp.exp(-jnp.abs(x)))


def _rmsnorm(x, g):
    return x * lax.rsqrt(jnp.mean(x * x, axis=-1, keepdims=True) + RMS_EPS) * g


def _split3(x):
    hi = x.astype(BF16)
    rest = x - hi.astype(F32)
    mid = rest.astype(BF16)
    return hi, mid, (rest - mid.astype(F32)).astype(BF16)


def _stage_ff_tail(w1_ref, w3_ref, w2_ref, tail_refs):
    w1t_ref, w3t_ref, w2t_ref = tail_refs
    lo = FF_FULL_CHUNKS * FF_CHUNK
    for dst, src in ((w1t_ref, w1_ref), (w3t_ref, w3_ref)):
        dst[...] = jnp.zeros_like(dst)
        dst[:, 0:FF_TAIL] = src[:, lo:D_FF]
    w2t_ref[...] = jnp.zeros_like(w2t_ref)
    w2t_ref[0:FF_TAIL, :] = w2_ref[lo:D_FF, :]


def _ff_tail_scratch():
    return [pltpu.VMEM((D_MODEL, FF_TAIL_PAD), BF16), pltpu.VMEM((D_MODEL, FF_TAIL_PAD), BF16),
            pltpu.VMEM((FF_TAIL_PAD, D_MODEL), BF16)]


def _swiglu(xn, w1_ref, w3_ref, w2_ref, tail_refs):
    def chunk(acc, w1c, w3c, w2c):
        a = _mm(xn, w1c)
        b = _mm(xn, w3c)
        return acc + _mm((_silu(a) * b).astype(BF16), w2c)

    acc = jnp.zeros((xn.shape[0], D_MODEL), F32)
    for c in range(FF_FULL_CHUNKS):
        sl = slice(c * FF_CHUNK, (c + 1) * FF_CHUNK)
        acc = chunk(acc, w1_ref[:, sl], w3_ref[:, sl], w2_ref[sl, :])
    if FF_TAIL:
        acc = chunk(acc, *(r[...] for r in tail_refs))
    return acc


def _segment_sumsq(x, seg_ref):
    hi, mid, _ = _split3(x * x)
    seg = seg_ref[0:x.shape[1], 0:x.shape[1]]
    return _mm(hi, seg) + _mm(mid, seg)


def _stage_a_kernel(x_ref, g1_ref, w1_ref, w3_ref, w2_ref, gm_ref, win_ref, qg_ref, kg_ref, seg_ref,
                    h_ref, z_ref, xbc_ref, dt_ref, q_ref, k_ref, v_ref, *tail_refs):
    @pl.when(pl.program_id(0) == 0)
    def _():
        _stage_ff_tail(w1_ref, w3_ref, w2_ref, tail_refs)

    x = x_ref[...]
    h = x + 0.5 * _swiglu(_rmsnorm(x, g1_ref[...]).astype(BF16), w1_ref, w3_ref, w2_ref, tail_refs)
    h_ref[...] = h
    proj = _mm(_rmsnorm(h, gm_ref[...]).astype(BF16), win_ref[...])
    z_ref[...] = proj[:, OFF_Z:OFF_XBC]
    xbc_ref[...] = proj[:, OFF_XBC:OFF_Q]
    dt_ref[...] = proj[:, OFF_DT:D_PROJ_PAD]
    q = proj[:, OFF_Q:OFF_K]
    k = proj[:, OFF_K:OFF_V]
    q_ref[...] = q * lax.rsqrt(_segment_sumsq(q, seg_ref) * (1.0 / HEAD_DIM) + RMS_EPS) * qg_ref[...]
    k_ref[...] = k * lax.rsqrt(_segment_sumsq(k, seg_ref) * (1.0 / HEAD_DIM) + RMS_EPS) * kg_ref[...]
    v_ref[...] = proj[:, OFF_V:OFF_DT]


def _layer_block(arr, layer, **kwargs):
    return pl.BlockSpec((None,) + arr.shape[1:], lambda *_: (layer, 0, 0), **kwargs)


def _layer_resident(arr, layer):
    return _layer_block(arr, layer, pipeline_mode=pl.Buffered(1))


def _stage_a(x, pw, layer):
    n = x.shape[0]
    tm = min(TOKEN_TILE, n)
    row = lambda width: pl.BlockSpec((tm, width), lambda i: (i, 0))
    widths = (D_MODEL, D_SSM, CONV_DIM, LANES, D_ATTN, KV_WIDTH, KV_WIDTH)
    consts = (pw['g_ffn1'], pw['w1_a'], pw['w3_a'], pw['w2_a'], pw['g_mix'], pw['w_in'],
              pw['q_gain'], pw['k_gain'], pw['seg'])
    return pl.pallas_call(
        _stage_a_kernel,
        grid=(n // tm,),
        in_specs=[row(D_MODEL)] + [_layer_resident(c, layer) for c in consts],
        out_specs=[row(w) for w in widths],
        out_shape=[jax.ShapeDtypeStruct((n, w), F32) for w in widths],
        scratch_shapes=_ff_tail_scratch(),
        compiler_params=pltpu.CompilerParams(dimension_semantics=("arbitrary",),
                                             vmem_limit_bytes=VMEM_LIMIT),
        name="stage_a",
    )(x, *consts)


def _stage_c_kernel(h_ref, mix_ref, pe_ref, wout_ref, g2_ref, w1_ref, w3_ref, w2_ref, gp_ref,
                    wgate_ref, wproj_ref, o_ref, *tail_refs):
    @pl.when(pl.program_id(0) == 0)
    def _():
        _stage_ff_tail(w1_ref, w3_ref, w2_ref, tail_refs)

    o_ref[...] = _stage_c_compute(h_ref[...], mix_ref[...].astype(BF16), pe_ref, wout_ref, g2_ref, w1_ref,
                                  w3_ref, w2_ref, gp_ref, wgate_ref, wproj_ref, tail_refs)


def _stage_c_compute(h, mix, pe_ref, wout_ref, g2_ref, w1_ref, w3_ref, w2_ref, gp_ref, wgate_ref, wproj_ref,
                     tail_refs):
    h = h + _mm(mix, wout_ref[...])
    h = h + 0.5 * _swiglu(_rmsnorm(h, g2_ref[...]).astype(BF16), w1_ref, w3_ref, w2_ref, tail_refs)
    gate = _sigmoid(_mm(_rmsnorm(h, gp_ref[...]).astype(BF16), wgate_ref[...]))
    return h + gate * _mm(pe_ref[...].astype(BF16), wproj_ref[...])


def _stage_c(h, mix, pe, pw, layer):
    n = h.shape[0]
    tm = min(2 * TOKEN_TILE, n)
    row = lambda width: pl.BlockSpec((tm, width), lambda i: (i, 0))
    consts = (pw['w_out'], pw['g_ffn2'], pw['w1_b'], pw['w3_b'], pw['w2_b'], pw['g_ple'],
              pw['w_ple_gate'], pw['w_ple_proj'])
    return pl.pallas_call(
        _stage_c_kernel,
        grid=(n // tm,),
        in_specs=[row(D_MODEL), row(D_MODEL), pl.BlockSpec((None, tm, D_PLE), lambda i: (layer, i, 0))]
                 + [_layer_resident(c, layer) for c in consts],
        out_specs=row(D_MODEL),
        out_shape=jax.ShapeDtypeStruct((n, D_MODEL), F32),
        scratch_shapes=_ff_tail_scratch(),
        compiler_params=pltpu.CompilerParams(dimension_semantics=("arbitrary",),
                                             vmem_limit_bytes=VMEM_LIMIT),
        name="stage_c",
    )(h, mix, pe, *consts)


def _lane_low_half(shape):
    return lax.broadcasted_iota(jnp.int32, shape, len(shape) - 1) < HEAD_DIM


def _split_pair(x):
    rolled = pltpu.roll(x, HEAD_DIM, 1)
    low = _lane_low_half(x.shape)
    return jnp.where(low, x, rolled), jnp.where(low, rolled, x)


def _head_query_rows(q, g):
    low = _lane_low_half((q.shape[0], LANES))
    blocks = []
    for jj in range(Q_PER_KV // 2):
        qp = q[:, (g * 2 + jj) * LANES:(g * 2 + jj + 1) * LANES]
        blocks += [jnp.where(low, qp, 0.0), jnp.where(low, 0.0, qp)]
    return jnp.concatenate(blocks, axis=0).astype(BF16)


def _attention_rows_scores(q, kk, vv):
    low_k = _lane_low_half((KEY_PAD, LANES))
    k_dup = _split_pair(kk)
    v_roll = pltpu.roll(vv, HEAD_DIM, 1)
    v_even = (jnp.where(low_k, vv, 1.0).astype(BF16), jnp.where(low_k, v_roll, 1.0).astype(BF16))
    v_odd = (jnp.where(low_k, 1.0, v_roll).astype(BF16), jnp.where(low_k, 1.0, vv).astype(BF16))
    scores = [_mm_nt(_head_query_rows(q, g), k_dup[g].astype(BF16)) for g in range(N_KV_HEADS)]
    return scores, (v_even, v_odd)


def _attention_rows_probs(scores, sinks_ref):
    r = scores[0].shape[0] // Q_PER_KV
    t_i = lax.broadcasted_iota(jnp.int32, (r, KEY_PAD), 0)
    s_i = lax.broadcasted_iota(jnp.int32, (r, KEY_PAD), 1)
    rel = t_i + WINDOW - s_i
    valid = (rel >= 0) & (rel < WINDOW)
    relf = rel.astype(F32)
    probs, sink_terms = [], []
    for head in range(N_HEADS):
        g, hh = divmod(head, Q_PER_KV)
        sh = jnp.where(valid, scores[g][hh * r:(hh + 1) * r] - ALIBI_SLOPES[head] * relf, -jnp.inf)
        sink = sinks_ref[head:head + 1, 0:1]
        m = jnp.maximum(jnp.max(sh, axis=-1, keepdims=True), sink)
        probs.append(jnp.exp(sh - m))
        sink_terms.append(jnp.broadcast_to(jnp.exp(sink - m), (r, LANES)))
    return probs, sink_terms


def _attention_rows_output(probs, sink_terms, v_operands):
    v_even, v_odd = v_operands
    r = probs[0].shape[0]
    low = _lane_low_half((r, LANES))
    outs = []
    for g in range(N_KV_HEADS):
        mine = probs[g * Q_PER_KV:(g + 1) * Q_PER_KV]
        o_even = _mm(jnp.concatenate(mine[0::2], axis=0).astype(BF16), v_even[g])
        o_odd = _mm(jnp.concatenate(mine[1::2], axis=0).astype(BF16), v_odd[g])
        for jj in range(Q_PER_KV // 2):
            oe, oo = o_even[jj * r:(jj + 1) * r], o_odd[jj * r:(jj + 1) * r]
            head = g * Q_PER_KV + 2 * jj
            den = (pltpu.roll(jnp.where(low, oo, oe), HEAD_DIM, 1)
                   + jnp.where(low, sink_terms[head], sink_terms[head + 1]))
            outs.append(jnp.where(low, oe, oo) / den)
    return jnp.concatenate(outs, axis=1)


def _fill_attention_bias(bias_scr):
    s_i = lax.broadcasted_iota(jnp.int32, (KEY_PAD, WINDOW), 0)
    t_i = lax.broadcasted_iota(jnp.int32, (KEY_PAD, WINDOW), 1)
    rel = t_i + WINDOW - s_i
    valid = (rel >= 0) & (rel < WINDOW)
    relf = rel.astype(F32)
    for head in range(N_HEADS):
        bias_scr[head * KEY_PAD:(head + 1) * KEY_PAD, :] = jnp.where(
            valid, (-ALIBI_SLOPES[head] * LOG2_E) * relf, -jnp.inf)


def _attention_cols_scores(q, kk):
    k_dup = _split_pair(kk)
    qs = q * (HEAD_DIM ** -0.5 * LOG2_E)
    return [_mm_nt(k_dup[g].astype(BF16), _head_query_rows(qs, g)) for g in range(N_KV_HEADS)]


def _attention_cols_probs(scores, bias_scr, sinks_ref, first_block):
    r = scores[0].shape[1] // Q_PER_KV
    mask_prev = None if first_block is None else jnp.where(first_block, -jnp.inf, 0.0)
    out = []
    for g in range(N_KV_HEADS):
        probs, sink_terms = [], []
        for hh in range(Q_PER_KV):
            head = g * Q_PER_KV + hh
            sh = scores[g][:, hh * r:(hh + 1) * r] + bias_scr[head * KEY_PAD:(head + 1) * KEY_PAD, :]
            prev = sh[0:WINDOW] if mask_prev is None else sh[0:WINDOW] + mask_prev
            cur = sh[WINDOW:KEY_PAD]
            sink = sinks_ref[head:head + 1, :] * LOG2_E
            m = jnp.maximum(jnp.maximum(jnp.max(prev, axis=0, keepdims=True),
                                        jnp.max(cur, axis=0, keepdims=True)), sink)
            probs.append(jnp.concatenate([jnp.exp2(prev - m), jnp.exp2(cur - m)], axis=0))
            sink_terms.append(jnp.exp2(sink - m))
        out.append((jnp.concatenate(probs, axis=1).astype(BF16), jnp.concatenate(sink_terms, axis=1)))
    return out


def _attention_cols_pv(probs, vv):
    vt = vv.T
    ones = jnp.ones((HEAD_DIM, KEY_PAD), F32)
    return [_mm(jnp.concatenate([vt[g * HEAD_DIM:(g + 1) * HEAD_DIM], ones], axis=0).astype(BF16), probs[g][0])
            for g in range(N_KV_HEADS)]


def _attention_cols_output(pv, probs):
    r = pv[0].shape[1] // Q_PER_KV
    outs = []
    for g in range(N_KV_HEADS):
        den = pv[g][HEAD_DIM:HEAD_DIM + 1, :] + probs[g][1]
        on = pv[g][0:HEAD_DIM, :] * (1.0 / den)
        for jj in range(Q_PER_KV // 2):
            pair = jnp.concatenate([on[:, (2 * jj) * r:(2 * jj + 1) * r],
                                    on[:, (2 * jj + 1) * r:(2 * jj + 2) * r]], axis=0)
            outs.append(pair.T)
    return jnp.concatenate(outs, axis=1)


def _gated_group_norm(y, z, gain):
    y = y * _silu(z)
    gw = D_SSM // N_SSM_GROUPS
    parts = [_rmsnorm(y[:, g * gw:(g + 1) * gw], gain[:, g * gw:(g + 1) * gw]) for g in range(N_SSM_GROUPS)]
    return jnp.concatenate(parts, axis=1)


def _prompt_tile_mixer(first, z_ref, xbc_ref, xprev_ref, dt_ref, q_ref, k_ref, kprev_ref, v_ref, vprev_ref,
                       cw_ref, cb_ref, dtb_ref, alog_ref, dskip_ref, snorm_ref, sinks_ref, expand_ref,
                       st_scr, xe_scr, bias_scr, store_mix, state_after):
    t = SSD_CHUNK
    rows = xbc_ref.shape[0]
    hw = D_SSM // N_SSM_GROUPS
    nbc = N_SSM_GROUPS * D_STATE

    dt_raw = _mm(jnp.concatenate(_split3(dt_ref[...] + dtb_ref[...]), axis=1), expand_ref[...])

    xe_scr[0:SUBLANES, :] = jnp.where(first, 0.0, xprev_ref[...])
    xe_scr[SUBLANES:SUBLANES + rows, :] = xbc_ref[...]
    cw = cw_ref[...]
    a_rep = -jnp.exp(alog_ref[...]) * LOG2_E
    causal = lax.broadcasted_iota(jnp.int32, (t, t), 0) >= lax.broadcasted_iota(jnp.int32, (t, t), 1)
    ltri = causal.astype(BF16)
    low_b = _lane_low_half((t, LANES))
    chunks = []
    for c in range(rows // t):
        sl = slice(c * t, (c + 1) * t)
        conv = cb_ref[...] + cw[CONV_W - 1:CONV_W] * xbc_ref[sl, :]
        for back in range(1, CONV_W):
            r0 = SUBLANES - back + c * t
            conv = conv + cw[CONV_W - 1 - back:CONV_W - back] * xe_scr[r0:r0 + t, :]
        xc = _silu(conv)
        dt = _softplus(dt_raw[sl])
        chunks.append(dict(
            sl=sl, dt=dt, xs=xc[:, 0:D_SSM], bm=xc[:, D_SSM:D_SSM + nbc], cm=xc[:, D_SSM + nbc:CONV_DIM],
            cum3=_mm(ltri, jnp.concatenate(_split3(dt * a_rep), axis=1))))
        yield

    for c, ch in enumerate(chunks):
        sl, xs, bm, cm = ch['sl'], ch['xs'], ch['bm'], ch['cm']
        cum3 = ch['cum3']
        cum = cum3[:, 0:D_SSM] + cum3[:, D_SSM:2 * D_SSM] + cum3[:, 2 * D_SSM:3 * D_SSM]
        xdt = xs * ch['dt']
        xw_b = (xdt * jnp.exp2(cum[t - 1:t, :] - cum)).astype(BF16)
        cg_b = [cm[:, g * D_STATE:(g + 1) * D_STATE].astype(BF16) for g in range(N_SSM_GROUPS)]
        bg = [bm[:, g * D_STATE:(g + 1) * D_STATE] for g in range(N_SSM_GROUPS)]
        k_before = kprev_ref[...] if c == 0 else k_ref[(c - 1) * t:c * t, :]
        v_before = vprev_ref[...] if c == 0 else v_ref[(c - 1) * t:c * t, :]
        ch.update(
            cum=cum, ecum=jnp.exp2(cum), xdt_b=xdt.astype(BF16), cg_b=cg_b,
            cb_t=[_mm_nt(cg_b[g], bg[g].astype(BF16)) for g in range(N_SSM_GROUPS)],
            st_new=[_mm(bg[g].T.astype(BF16), xw_b[:, g * hw:(g + 1) * hw]) for g in range(N_SSM_GROUPS)],
            att_scores=_attention_cols_scores(q_ref[sl, :], jnp.concatenate([k_before, k_ref[sl, :]], axis=0)),
            vv=jnp.concatenate([v_before, v_ref[sl, :]], axis=0))
        yield

    st = jnp.where(first, 0.0, st_scr[...])
    for c, ch in enumerate(chunks):
        y_in = []
        for j in range(D_SSM // LANES):
            cb_t = ch['cb_t'][j // (hw // LANES)]
            scores = []
            for col in _split_pair(ch['cum'][:, j * LANES:(j + 1) * LANES]):
                decay = jnp.exp2(jnp.where(causal, col - col.T, -jnp.inf))
                scores.append((cb_t * decay).astype(BF16))
            xp = ch['xdt_b'][:, j * LANES:(j + 1) * LANES]
            zero = jnp.zeros_like(xp)
            rhs = jnp.concatenate([jnp.where(low_b, xp, zero), jnp.where(low_b, zero, xp)], axis=0)
            y_in.append(_mm(jnp.concatenate(scores, axis=1), rhs))
        yield

        probs = _attention_cols_probs(ch['att_scores'], bias_scr, sinks_ref, first if c == 0 else None)
        pv = _attention_cols_pv(probs, ch['vv'])
        st_b = st.astype(BF16)
        y_off = [_mm(ch['cg_b'][g], st_b[:, g * hw:(g + 1) * hw]) for g in range(N_SSM_GROUPS)]
        yield

        y = (jnp.concatenate(y_in, axis=1) + jnp.concatenate(y_off, axis=1) * ch['ecum']
             + dskip_ref[...] * ch['xs'])
        st = st * ch['ecum'][t - 1:t, :] + jnp.concatenate(ch['st_new'], axis=1)
        ssd_out = _gated_group_norm(y, z_ref[ch['sl'], :], snorm_ref[...])
        yield

        store_mix(ch['sl'], ssd_out, _attention_cols_output(pv, probs))
        yield
    st_scr[...] = st
    state_after.append(st)


def _mixer_prompt_kernel(*refs, tiles_per_seq):
    mix_ref, state_ref, st_scr, xe_scr, bias_scr = refs[-5:]
    j = pl.program_id(0)

    @pl.when(j == 0)
    def _():
        _fill_attention_bias(bias_scr)

    def store_mix(rows, ssd_out, attn_out):
        mix_ref[rows, 0:D_SSM] = ssd_out.astype(mix_ref.dtype)
        mix_ref[rows, D_SSM:D_SSM + D_ATTN] = attn_out.astype(mix_ref.dtype)

    state_after = []
    for _ in _prompt_tile_mixer(lax.rem(j, tiles_per_seq) == 0, *refs[:-5], st_scr, xe_scr, bias_scr,
                                store_mix, state_after):
        pass
    st = state_after[0]

    @pl.when(lax.rem(j, tiles_per_seq) == tiles_per_seq - 1)
    def _():
        for jb in range(D_SSM // LANES):
            state_ref[jb * LANES:(jb + 1) * LANES, :] = st[:, jb * LANES:(jb + 1) * LANES].T


def _mixer_consts(pw):
    return (pw['conv_w'], pw['conv_b'], pw['dt_bias'], pw['a_log_rep'], pw['d_skip'], pw['ssm_norm'], pw['sinks'],
            pw['head_expand'])


def _mixer_prompt(z, xbc, dtr, q, k, v, pw, layer, batch, seq):
    t = SSD_CHUNK
    tile = MIXER_TILE
    tiles_per_seq = seq // tile
    cur = lambda width: pl.BlockSpec((tile, width), lambda i: (i, 0))
    before = lambda rows, width: pl.BlockSpec((rows, width), lambda i: (jnp.maximum(i * (tile // rows) - 1, 0), 0))
    consts = _mixer_consts(pw)
    return pl.pallas_call(
        functools.partial(_mixer_prompt_kernel, tiles_per_seq=tiles_per_seq),
        grid=(batch * tiles_per_seq,),
        in_specs=[cur(D_SSM), cur(CONV_DIM), before(SUBLANES, CONV_DIM), cur(LANES), cur(D_ATTN),
                  cur(KV_WIDTH), before(t, KV_WIDTH), cur(KV_WIDTH), before(t, KV_WIDTH)]
                 + [_layer_block(cst, layer) for cst in consts],
        out_specs=[pl.BlockSpec((tile, D_SSM + D_ATTN), lambda i: (i, 0)),
                   pl.BlockSpec((None, D_SSM, D_STATE), lambda i: (i // tiles_per_seq, 0, 0))],
        out_shape=[jax.ShapeDtypeStruct((batch * seq, D_SSM + D_ATTN), BF16),
                   jax.ShapeDtypeStruct((batch, D_SSM, D_STATE), F32)],
        scratch_shapes=[pltpu.VMEM((D_STATE, D_SSM), F32),
                        pltpu.VMEM((SUBLANES + tile, CONV_DIM), F32),
                        pltpu.VMEM((N_HEADS * KEY_PAD, WINDOW), F32)],
        compiler_params=pltpu.CompilerParams(dimension_semantics=("arbitrary",),
                                             vmem_limit_bytes=VMEM_LIMIT),
        name="mixer_prompt",
    )(z, xbc, xbc, dtr, q, k, k, v, v, *consts)


N_SAMPLE_IN = 18


def _put_lane_blocks(scr, col0, rows, val):
    for j in range(val.shape[1] // LANES):
        scr[col0 // LANES + j, rows, :] = val[:, j * LANES:(j + 1) * LANES]


def _get_lane_blocks(scr, col0, width, rows):
    return jnp.concatenate([scr[col0 // LANES + j, rows, :] for j in range(width // LANES)], axis=1)


def _mixer_sample_kernel(*refs):
    (z_ref, xbc_ref, dt_ref, q_ref, k_ref, v_ref, sst_ref, sconv_ref, ck_ref, cv_ref,
     cw_ref, cb_ref, dtb_ref, alog_ref, dskip_ref, snorm_ref, sinks_ref, expand_ref) = refs[:N_SAMPLE_IN]
    (mix_ref, sst_out_ref, ck_out_ref, cv_out_ref,
     ypart_scr, ecum_scr, cdec_scr, xw_scr, xt_scr, bmat_scr, seq_scr, res_scr, kk_scr, vv_scr) = refs[-14:]
    dl, bb = xbc_ref.shape[0], xbc_ref.shape[1]
    hw = D_SSM // N_SSM_GROUPS
    nbc = N_SSM_GROUPS * D_STATE
    c_q, c_c, c_k, c_v = 0, D_ATTN, D_ATTN + nbc, D_ATTN + nbc + KV_WIDTH
    xw_scr[...] = jnp.zeros_like(xw_scr)
    bmat_scr[...] = jnp.zeros_like(bmat_scr)
    seq_scr[...] = jnp.zeros_like(seq_scr)
    kk_scr[...] = jnp.zeros_like(kk_scr)
    vv_scr[...] = jnp.zeros_like(vv_scr)

    cw = cw_ref[...]
    a_rep = -jnp.exp(alog_ref[...])
    xs, bm, cm, xdt, cum = [], [], [], [], []
    for t in range(dl):
        conv = cb_ref[...] + cw[CONV_W - 1:CONV_W] * xbc_ref[t]
        for back in range(1, CONV_W):
            src = xbc_ref[t - back] if t >= back else sconv_ref[CONV_W - 1 + t - back]
            conv = conv + cw[CONV_W - 1 - back:CONV_W - back] * src
        xc = _silu(conv)
        xs.append(xc[:, 0:D_SSM])
        bm.append(xc[:, D_SSM:D_SSM + nbc])
        cm.append(xc[:, D_SSM + nbc:CONV_DIM])
        dt_c = _softplus(dt_ref[t] + dtb_ref[...])
        dt = _mm(jnp.concatenate(_split3(dt_c), axis=1), expand_ref[...])
        cum.append(dt * a_rep if t == 0 else cum[-1] + dt * a_rep)
        xdt.append(xs[t] * dt)
    cdec_scr[...] = jnp.exp(cum[dl - 1])
    for t in range(dl):
        y = dskip_ref[...] * xs[t]
        for s in range(t + 1):
            prod = cm[t] * bm[s]
            dots = jnp.concatenate(
                [jnp.broadcast_to(jnp.sum(prod[:, g * D_STATE:(g + 1) * D_STATE], axis=-1, keepdims=True), (bb, hw))
                 for g in range(N_SSM_GROUPS)], axis=1)
            y = y + (dots * xdt[s] if s == t else dots * jnp.exp(cum[t] - cum[s]) * xdt[s])
        rows_t = slice(t * bb, (t + 1) * bb)
        ypart_scr[rows_t, :] = y
        ecum_scr[rows_t, :] = jnp.exp(cum[t])
        xw_scr[rows_t, :] = xdt[t] * jnp.exp(cum[dl - 1] - cum[t])
        bmat_scr[rows_t, :] = bm[t]
        own_rows = pl.ds(t, bb, stride=SEQ_ROWS)
        _put_lane_blocks(seq_scr, c_q, own_rows, q_ref[t] * (HEAD_DIM ** -0.5))
        _put_lane_blocks(seq_scr, c_c, own_rows, cm[t])
        _put_lane_blocks(seq_scr, c_k, own_rows, k_ref[t])
        _put_lane_blocks(seq_scr, c_v, own_rows, v_ref[t])
    xw = xw_scr[...]
    for j in range(D_SSM // LANES):
        xt_scr[j * LANES:(j + 1) * LANES, :] = xw[:, j * LANES:(j + 1) * LANES].T.astype(BF16)

    block_row = lax.broadcasted_iota(jnp.int32, (SSD_CHUNK, D_STATE), 0)

    def state_matmuls(b):
        my_rows = pl.ds(pl.multiple_of(b * SEQ_ROWS, SEQ_ROWS), SEQ_ROWS)
        state = sst_ref[b]
        state_b = state.astype(BF16)
        c_rows = jnp.concatenate([_get_lane_blocks(seq_scr, c_c, nbc, my_rows),
                                  jnp.zeros((SEQ_ROWS, nbc), F32)], axis=0).astype(BF16)
        y_off = [_mm_nt(c_rows[:, g * D_STATE:(g + 1) * D_STATE], state_b[g * hw:(g + 1) * hw, :])[0:SEQ_ROWS]
                 for g in range(N_SSM_GROUPS)]
        is_mine = (block_row & (bb - 1)) == b
        upd = [_mm(xt_scr[g * hw:(g + 1) * hw, :],
                   jnp.where(is_mine, bmat_scr[:, g * D_STATE:(g + 1) * D_STATE], 0.0).astype(BF16))
               for g in range(N_SSM_GROUPS)]
        return my_rows, state, y_off, upd

    def attention_scores(b, slot, my_rows):
        kk_scr[slot, 0:WINDOW, :] = ck_ref[b]
        kk_scr[slot, WINDOW:WINDOW + SEQ_ROWS, :] = _get_lane_blocks(seq_scr, c_k, KV_WIDTH, my_rows)
        vv_scr[slot, 0:WINDOW, :] = cv_ref[b]
        vv_scr[slot, WINDOW:WINDOW + SEQ_ROWS, :] = _get_lane_blocks(seq_scr, c_v, KV_WIDTH, my_rows)
        ck_out_ref[b] = kk_scr[slot, dl:dl + WINDOW, :]
        cv_out_ref[b] = vv_scr[slot, dl:dl + WINDOW, :]
        return _attention_rows_scores(_get_lane_blocks(seq_scr, c_q, D_ATTN, my_rows), kk_scr[slot], vv_scr[slot])

    def store_state(b, my_rows, state, y_off, upd):
        _put_lane_blocks(res_scr, 0, my_rows, jnp.concatenate(y_off, axis=1))
        decay = jnp.broadcast_to(cdec_scr[pl.ds(b, 1), :], (SUBLANES, D_SSM))
        for j in range(D_SSM // LANES):
            g, jj = divmod(j, hw // LANES)
            for half, dec in enumerate(_split_pair(decay[:, j * LANES:(j + 1) * LANES])):
                h0 = j * LANES + half * HEAD_DIM
                u0 = jj * LANES + half * HEAD_DIM
                sst_out_ref[b, h0:h0 + HEAD_DIM, :] = (
                    state[h0:h0 + HEAD_DIM] * jnp.broadcast_to(dec[0:1], (HEAD_DIM, D_STATE))
                    + upd[g][u0:u0 + HEAD_DIM])

    def per_group(i, carry):
        seqs = [i * SAMPLE_UNROLL + slot for slot in range(SAMPLE_UNROLL)]
        parts = [state_matmuls(b) for b in seqs]
        scored = [attention_scores(b, slot, parts[slot][0]) for slot, b in enumerate(seqs)]
        for slot, b in enumerate(seqs):
            store_state(b, *parts[slot])
        weights = [_attention_rows_probs(scores, sinks_ref) for scores, _ in scored]
        for slot in range(SAMPLE_UNROLL):
            att = _attention_rows_output(*weights[slot], scored[slot][1])
            _put_lane_blocks(res_scr, D_SSM, parts[slot][0], att)
        return carry

    lax.fori_loop(0, bb // SAMPLE_UNROLL, per_group, 0)

    for t in range(dl):
        own_rows = pl.ds(t, bb, stride=SEQ_ROWS)
        rows_t = slice(t * bb, (t + 1) * bb)
        y = ypart_scr[rows_t, :] + _get_lane_blocks(res_scr, 0, D_SSM, own_rows) * ecum_scr[rows_t, :]
        mix_ref[t, :, 0:D_SSM] = _gated_group_norm(y, z_ref[t], snorm_ref[...])
        mix_ref[t, :, D_SSM:D_SSM + D_ATTN] = _get_lane_blocks(res_scr, D_SSM, D_ATTN, own_rows)


def _mixer_sample(z, xbc, dtr, q, k, v, state_ssm, state_conv, cache_k, cache_v, pw, layer, carried):
    depth = state_ssm.shape[0]
    dl, nb = xbc.shape[0], xbc.shape[1]
    bb = min(SAMPLE_SEQS_PER_STEP, nb)
    assert bb & (bb - 1) == 0 and dl * bb <= SSD_CHUNK and dl <= SEQ_ROWS and bb % SAMPLE_UNROLL == 0
    tok = lambda width: pl.BlockSpec((dl, bb, width), lambda i: (0, i, 0))
    lblk = lambda *dims: pl.BlockSpec((None, bb) + dims, lambda i: (layer, i) + (0,) * len(dims))
    consts = _mixer_consts(pw)
    operands = (z, xbc, dtr, q, k, v, state_ssm, state_conv, cache_k, cache_v) + consts
    assert len(operands) == N_SAMPLE_IN
    seq_cols = D_ATTN + N_SSM_GROUPS * D_STATE + 2 * KV_WIDTH
    return pl.pallas_call(
        _mixer_sample_kernel,
        grid=(nb // bb,),
        in_specs=[tok(D_SSM), tok(CONV_DIM), tok(LANES), tok(D_ATTN), tok(KV_WIDTH), tok(KV_WIDTH),
                  lblk(D_SSM, D_STATE),
                  pl.BlockSpec((None, CONV_W - 1, bb, CONV_DIM), lambda i: (layer, 0, i, 0)),
                  lblk(WINDOW, KV_WIDTH), lblk(WINDOW, KV_WIDTH)]
                 + [_layer_block(cst, layer) for cst in consts]
                 + [pl.BlockSpec(memory_space=pl.ANY)] * len(carried),
        out_specs=[tok(D_SSM + D_ATTN), lblk(D_SSM, D_STATE), lblk(WINDOW, KV_WIDTH), lblk(WINDOW, KV_WIDTH)],
        out_shape=[jax.ShapeDtypeStruct((dl, nb, D_SSM + D_ATTN), F32),
                   jax.ShapeDtypeStruct((depth, nb, D_SSM, D_STATE), F32),
                   jax.ShapeDtypeStruct((depth, nb, WINDOW, KV_WIDTH), F32),
                   jax.ShapeDtypeStruct((depth, nb, WINDOW, KV_WIDTH), F32)],
        input_output_aliases={N_SAMPLE_IN + n: 1 + n for n in range(len(carried))},
        scratch_shapes=[pltpu.VMEM((dl * bb, D_SSM), F32),
                        pltpu.VMEM((dl * bb, D_SSM), F32),
                        pltpu.VMEM((bb, D_SSM), F32),
                        pltpu.VMEM((SSD_CHUNK, D_SSM), F32),
                        pltpu.VMEM((D_SSM, SSD_CHUNK), BF16),
                        pltpu.VMEM((SSD_CHUNK, N_SSM_GROUPS * D_STATE), F32),
                        pltpu.VMEM((seq_cols // LANES, bb * SEQ_ROWS, LANES), F32),
                        pltpu.VMEM(((D_SSM + D_ATTN) // LANES, bb * SEQ_ROWS, LANES), F32),
                        pltpu.VMEM((SAMPLE_UNROLL, KEY_PAD, KV_WIDTH), F32),
                        pltpu.VMEM((SAMPLE_UNROLL, KEY_PAD, KV_WIDTH), F32)],
        compiler_params=pltpu.CompilerParams(dimension_semantics=("arbitrary",),
                                             vmem_limit_bytes=VMEM_LIMIT),
        name="mixer_sample",
    )(*operands, *carried)


def _prepare_weights(p):
    bf = lambda w: w.astype(BF16)
    vec = lambda v: v[:, None, :]

    rep = lambda v: jnp.repeat(v, HEAD_DIM, axis=-1)[:, None, :]
    lane_pad = lambda v: jnp.pad(v, ((0, 0), (0, LANES - v.shape[-1])))[:, None, :]
    cuts = np.cumsum([D_SSM, CONV_DIM, N_SSM_HEADS, D_ATTN, KV_WIDTH])
    wz, wxbc, wdt, wq, wk, wv = jnp.split(bf(p['w_in']), cuts, axis=2)
    depth = p['w_in'].shape[0]
    seg = np.kron(np.eye(D_ATTN // HEAD_DIM, dtype=np.float32), np.ones((HEAD_DIM, HEAD_DIM), np.float32))
    heads = np.eye(LANES, dtype=np.float32)[:, :N_SSM_HEADS]
    expand = np.tile(np.kron(heads, np.ones((1, HEAD_DIM), np.float32)), (3, 1))
    return {
        'g_ffn1': vec(p['g_ffn1']), 'g_mix': vec(p['g_mix']), 'g_ffn2': vec(p['g_ffn2']), 'g_ple': vec(p['g_ple']),
        'w1_a': bf(p['w1_a']), 'w3_a': bf(p['w3_a']), 'w2_a': bf(p['w2_a']),
        'w1_b': bf(p['w1_b']), 'w3_b': bf(p['w3_b']), 'w2_b': bf(p['w2_b']),
        'w_in': jnp.concatenate(
            [wz, wxbc, wq, wk, wv, jnp.pad(wdt, ((0, 0), (0, 0), (0, LANES - N_SSM_HEADS)))], axis=2),
        'w_out': bf(p['w_out']), 'w_ple_gate': bf(p['w_ple_gate']), 'w_ple_proj': bf(p['w_ple_proj']),
        'q_gain': vec(jnp.tile(p['q_norm'], (1, N_HEADS))),
        'k_gain': vec(jnp.tile(p['k_norm'], (1, N_KV_HEADS))),
        'seg': jnp.broadcast_to(jnp.asarray(seg, BF16), (depth,) + seg.shape),
        'conv_w': p['conv_w'], 'conv_b': vec(p['conv_b']),
        'dt_bias': lane_pad(p['dt_bias']), 'a_log_rep': rep(p['a_log']),
        'd_skip': rep(p['d_skip']),
        'head_expand': jnp.broadcast_to(jnp.asarray(expand, BF16), (depth,) + expand.shape),
        'ssm_norm': vec(p['ssm_norm']),
        'sinks': jnp.broadcast_to(p['sinks'][:, :, None], (depth, N_HEADS, LANES)),
    }


def kernel(x_prompt, x_sample, state_ssm, state_conv, cache_k_win, cache_v_win, p_prompt, p_sample, g_ffn1, w1_a, w3_a, w2_a, g_mix, w_in, conv_w, conv_b, dt_bias, a_log, d_skip, ssm_norm, q_norm, k_norm, sinks, w_out, g_ffn2, w1_b, w3_b, w2_b, g_ple, w_ple_gate, w_ple_proj):
    params = dict(g_ffn1=g_ffn1, w1_a=w1_a, w3_a=w3_a, w2_a=w2_a, g_mix=g_mix, w_in=w_in, conv_w=conv_w,
                  conv_b=conv_b, dt_bias=dt_bias, a_log=a_log, d_skip=d_skip, ssm_norm=ssm_norm, q_norm=q_norm,
                  k_norm=k_norm, sinks=sinks, w_out=w_out, g_ffn2=g_ffn2, w1_b=w1_b, w3_b=w3_b, w2_b=w2_b,
                  g_ple=g_ple, w_ple_gate=w_ple_gate, w_ple_proj=w_ple_proj)
    depth = w_in.shape[0]
    bp, seq, _ = x_prompt.shape
    bs, dl, _ = x_sample.shape
    assert seq % SSD_CHUNK == 0 and seq >= WINDOW and dl >= CONV_W - 1 and dl <= SUBLANES
    assert (bp * seq) % TOKEN_TILE == 0 and (bs * dl) % min(TOKEN_TILE, bs * dl) == 0

    pw = _prepare_weights(params)
    sst_in = state_ssm.reshape(depth, bs, D_SSM, D_STATE)
    ck_in = cache_k_win.reshape(depth, bs, WINDOW, KV_WIDTH)
    cv_in = cache_v_win.reshape(depth, bs, WINDOW, KV_WIDTH)
    sconv_in = jnp.swapaxes(state_conv, 1, 2)
    pe_p = p_prompt.reshape(depth, bp * seq, D_PLE)
    pe_s = jnp.swapaxes(p_sample, 1, 2).reshape(depth, dl * bs, D_PLE)
    hp = x_prompt.reshape(bp * seq, D_MODEL)
    hs = jnp.swapaxes(x_sample, 0, 1).reshape(dl * bs, D_MODEL)
    ssm_p, conv_p, k_p, v_p, conv_s = [], [], [], [], []
    carried = ()
    for l in range(depth):
        hp, zp, xbcp, dtp, qp, kp, vp = _stage_a(hp, pw, l)
        hs, zs, xbcs, dts, qs, ks, vs = _stage_a(hs, pw, l)
        mixp, sp = _mixer_prompt(zp, xbcp, dtp, qp, kp, vp, pw, l, bp, seq)
        hp = _stage_c(hp, mixp, pe_p, pw, l)
        tm3 = lambda a: a.reshape(dl, bs, a.shape[-1])
        mixs, *carried = _mixer_sample(tm3(zs), tm3(xbcs), tm3(dts), tm3(qs), tm3(ks), tm3(vs),
                                       sst_in, sconv_in, ck_in, cv_in, pw, l, tuple(carried))
        hs = _stage_c(hs, mixs.reshape(dl * bs, D_MODEL), pe_s, pw, l)
        ssm_p.append(sp.reshape(bp, N_SSM_HEADS, HEAD_DIM, D_STATE))
        conv_p.append(xbcp.reshape(bp, seq, CONV_DIM)[:, seq - (CONV_W - 1):])
        last_window = lambda a: a.reshape(bp, seq, KV_WIDTH)[:, seq - WINDOW:].reshape(
            bp, WINDOW, N_KV_HEADS, HEAD_DIM)
        k_p.append(last_window(kp))
        v_p.append(last_window(vp))
        conv_s.append(jnp.swapaxes(tm3(xbcs)[dl - (CONV_W - 1):], 0, 1))
    ssm_s, k_s, v_s = carried
    return (hp.reshape(bp, seq, D_MODEL), jnp.swapaxes(hs.reshape(dl, bs, D_MODEL), 0, 1),
            jnp.stack(ssm_p), jnp.stack(conv_p), jnp.stack(k_p), jnp.stack(v_p),
            ssm_s.reshape(depth, bs, N_SSM_HEADS, HEAD_DIM, D_STATE), jnp.stack(conv_s),
            k_s.reshape(depth, bs, WINDOW, N_KV_HEADS, HEAD_DIM),
            v_s.reshape(depth, bs, WINDOW, N_KV_HEADS, HEAD_DIM))
```

```python
import functools

import numpy as np
import jax
import jax.numpy as jnp
from jax import lax
from jax.experimental import pallas as pl
from jax.experimental.pallas import tpu as pltpu

F32 = jnp.float32
BF16 = jnp.bfloat16

D_MODEL = 1024
HEAD_DIM = 64
D_SSM = 512
N_SSM_HEADS = 8
N_SSM_GROUPS = 2
D_STATE = 128
CONV_W = 4
CONV_DIM = D_SSM + 2 * N_SSM_GROUPS * D_STATE
SSD_CHUNK = 128
D_ATTN = 512
N_HEADS = 8
N_KV_HEADS = 2
Q_PER_KV = N_HEADS // N_KV_HEADS
KV_WIDTH = N_KV_HEADS * HEAD_DIM
WINDOW = 128
D_FF = 2752
D_PLE = 256
RMS_EPS = 1e-6

LANES = 128
SUBLANES = 8
MXU_DIM = 256
FF_CHUNK = MXU_DIM
FF_FULL_CHUNKS = D_FF // FF_CHUNK
FF_TAIL = D_FF - FF_FULL_CHUNKS * FF_CHUNK
FF_TAIL_PAD = -(-FF_TAIL // MXU_DIM) * MXU_DIM
TOKEN_TILE = 512
MIXER_TILE = 1024
SAMPLE_SEQS_PER_STEP = 16
SAMPLE_UNROLL = 8
SEQ_ROWS = SUBLANES
KEY_PAD = 2 * WINDOW
VMEM_LIMIT = 56 * 1024 * 1024

OFF_Z = 0
OFF_XBC = OFF_Z + D_SSM
OFF_Q = OFF_XBC + CONV_DIM
OFF_K = OFF_Q + D_ATTN
OFF_V = OFF_K + KV_WIDTH
OFF_DT = OFF_V + KV_WIDTH
D_PROJ_PAD = OFF_DT + LANES

ALIBI_SLOPES = tuple(float(s) for s in np.power(
    np.float32(2.0), -8.0 * np.arange(1, N_HEADS + 1, dtype=np.float32) / N_HEADS))

LOG2_E = 1.4426950408889634
_NT = (((1,), (1,)), ((), ()))


def _mm(a, b):
    return jnp.dot(a, b, preferred_element_type=F32)


def _mm_nt(a, b):
    return lax.dot_general(a, b, _NT, preferred_element_type=F32)


def _sigmoid(x):
    return 1.0 / (1.0 + jnp.exp(-x))


def _silu(x):
    return x * _sigmoid(x)


def _softplus(x):
    return jnp.maximum(x, 0.0) + jnp.log(1.0 + jnp.exp(-jnp.abs(x)))


def _rmsnorm(x, g):
    return x * lax.rsqrt(jnp.mean(x * x, axis=-1, keepdims=True) + RMS_EPS) * g


def _split3(x):
    hi = x.astype(BF16)
    rest = x - hi.astype(F32)
    mid = rest.astype(BF16)
    return hi, mid, (rest - mid.astype(F32)).astype(BF16)


def _stage_ff_tail(w1_ref, w3_ref, w2_ref, tail_refs):
    w1t_ref, w3t_ref, w2t_ref = tail_refs
    lo = FF_FULL_CHUNKS * FF_CHUNK
    for dst, src in ((w1t_ref, w1_ref), (w3t_ref, w3_ref)):
        dst[...] = jnp.zeros_like(dst)
        dst[:, 0:FF_TAIL] = src[:, lo:D_FF]
    w2t_ref[...] = jnp.zeros_like(w2t_ref)
    w2t_ref[0:FF_TAIL, :] = w2_ref[lo:D_FF, :]


def _ff_tail_scratch():
    return [pltpu.VMEM((D_MODEL, FF_TAIL_PAD), BF16), pltpu.VMEM((D_MODEL, FF_TAIL_PAD), BF16),
            pltpu.VMEM((FF_TAIL_PAD, D_MODEL), BF16)]


def _swiglu(xn, w1_ref, w3_ref, w2_ref, tail_refs):
    def chunk(acc, w1c, w3c, w2c):
        a = _mm(xn, w1c)
        b = _mm(xn, w3c)
        return acc + _mm((_silu(a) * b).astype(BF16), w2c)

    acc = jnp.zeros((xn.shape[0], D_MODEL), F32)
    for c in range(FF_FULL_CHUNKS):
        sl = slice(c * FF_CHUNK, (c + 1) * FF_CHUNK)
        acc = chunk(acc, w1_ref[:, sl], w3_ref[:, sl], w2_ref[sl, :])
    if FF_TAIL:
        acc = chunk(acc, *(r[...] for r in tail_refs))
    return acc


def _segment_sumsq(x, seg_ref):
    hi, mid, _ = _split3(x * x)
    seg = seg_ref[0:x.shape[1], 0:x.shape[1]]
    return _mm(hi, seg) + _mm(mid, seg)


def _stage_a_kernel(x_ref, g1_ref, w1_ref, w3_ref, w2_ref, gm_ref, win_ref, qg_ref, kg_ref, seg_ref,
                    h_ref, z_ref, xbc_ref, dt_ref, q_ref, k_ref, v_ref, *tail_refs):
    @pl.when(pl.program_id(0) == 0)
    def _():
        _stage_ff_tail(w1_ref, w3_ref, w2_ref, tail_refs)

    x = x_ref[...]
    h = x + 0.5 * _swiglu(_rmsnorm(x, g1_ref[...]).astype(BF16), w1_ref, w3_ref, w2_ref, tail_refs)
    h_ref[...] = h
    proj = _mm(_rmsnorm(h, gm_ref[...]).astype(BF16), win_ref[...])
    z_ref[...] = proj[:, OFF_Z:OFF_XBC]
    xbc_ref[...] = proj[:, OFF_XBC:OFF_Q]
    dt_ref[...] = proj[:, OFF_DT:D_PROJ_PAD]
    q = proj[:, OFF_Q:OFF_K]
    k = proj[:, OFF_K:OFF_V]
    q_ref[...] = q * lax.rsqrt(_segment_sumsq(q, seg_ref) * (1.0 / HEAD_DIM) + RMS_EPS) * qg_ref[...]
    k_ref[...] = k * lax.rsqrt(_segment_sumsq(k, seg_ref) * (1.0 / HEAD_DIM) + RMS_EPS) * kg_ref[...]
    v_ref[...] = proj[:, OFF_V:OFF_DT]


def _layer_block(arr, layer, **kwargs):
    return pl.BlockSpec((None,) + arr.shape[1:], lambda *_: (layer, 0, 0), **kwargs)


def _layer_resident(arr, layer):
    return _layer_block(arr, layer, pipeline_mode=pl.Buffered(1))


def _stage_a(x, pw, layer):
    n = x.shape[0]
    tm = min(TOKEN_TILE, n)
    row = lambda width: pl.BlockSpec((tm, width), lambda i: (i, 0))
    widths = (D_MODEL, D_SSM, CONV_DIM, LANES, D_ATTN, KV_WIDTH, KV_WIDTH)
    consts = (pw['g_ffn1'], pw['w1_a'], pw['w3_a'], pw['w2_a'], pw['g_mix'], pw['w_in'],
              pw['q_gain'], pw['k_gain'], pw['seg'])
    return pl.pallas_call(
        _stage_a_kernel,
        grid=(n // tm,),
        in_specs=[row(D_MODEL)] + [_layer_resident(c, layer) for c in consts],
        out_specs=[row(w) for w in widths],
        out_shape=[jax.ShapeDtypeStruct((n, w), F32) for w in widths],
        scratch_shapes=_ff_tail_scratch(),
        compiler_params=pltpu.CompilerParams(dimension_semantics=("arbitrary",),
                                             vmem_limit_bytes=VMEM_LIMIT),
        name="stage_a",
    )(x, *consts)


def _stage_c_kernel(h_ref, mix_ref, pe_ref, wout_ref, g2_ref, w1_ref, w3_ref, w2_ref, gp_ref,
                    wgate_ref, wproj_ref, o_ref, *tail_refs):
    @pl.when(pl.program_id(0) == 0)
    def _():
        _stage_ff_tail(w1_ref, w3_ref, w2_ref, tail_refs)

    o_ref[...] = _stage_c_compute(h_ref[...], mix_ref[...].astype(BF16), pe_ref, wout_ref, g2_ref, w1_ref,
                                  w3_ref, w2_ref, gp_ref, wgate_ref, wproj_ref, tail_refs)


def _stage_c_compute(h, mix, pe_ref, wout_ref, g2_ref, w1_ref, w3_ref, w2_ref, gp_ref, wgate_ref, wproj_ref,
                     tail_refs):
    h = h + _mm(mix, wout_ref[...])
    h = h + 0.5 * _swiglu(_rmsnorm(h, g2_ref[...]).astype(BF16), w1_ref, w3_ref, w2_ref, tail_refs)
    gate = _sigmoid(_mm(_rmsnorm(h, gp_ref[...]).astype(BF16), wgate_ref[...]))
    return h + gate * _mm(pe_ref[...].astype(BF16), wproj_ref[...])


def _stage_c(h, mix, pe, pw, layer):
    n = h.shape[0]
    tm = min(2 * TOKEN_TILE, n)
    row = lambda width: pl.BlockSpec((tm, width), lambda i: (i, 0))
    consts = (pw['w_out'], pw['g_ffn2'], pw['w1_b'], pw['w3_b'], pw['w2_b'], pw['g_ple'],
              pw['w_ple_gate'], pw['w_ple_proj'])
    return pl.pallas_call(
        _stage_c_kernel,
        grid=(n // tm,),
        in_specs=[row(D_MODEL), row(D_MODEL), pl.BlockSpec((None, tm, D_PLE), lambda i: (layer, i, 0))]
                 + [_layer_resident(c, layer) for c in consts],
        out_specs=row(D_MODEL),
        out_shape=jax.ShapeDtypeStruct((n, D_MODEL), F32),
        scratch_shapes=_ff_tail_scratch(),
        compiler_params=pltpu.CompilerParams(dimension_semantics=("arbitrary",),
                                             vmem_limit_bytes=VMEM_LIMIT),
        name="stage_c",
    )(h, mix, pe, *consts)


def _lane_low_half(shape):
    return lax.broadcasted_iota(jnp.int32, shape, len(shape) - 1) < HEAD_DIM


def _split_pair(x):
    rolled = pltpu.roll(x, HEAD_DIM, 1)
    low = _lane_low_half(x.shape)
    return jnp.where(low, x, rolled), jnp.where(low, rolled, x)


def _head_query_rows(q, g):
    low = _lane_low_half((q.shape[0], LANES))
    blocks = []
    for jj in range(Q_PER_KV // 2):
        qp = q[:, (g * 2 + jj) * LANES:(g * 2 + jj + 1) * LANES]
        blocks += [jnp.where(low, qp, 0.0), jnp.where(low, 0.0, qp)]
    return jnp.concatenate(blocks, axis=0).astype(BF16)


def _attention_rows_scores(q, kk, vv):
    low_k = _lane_low_half((KEY_PAD, LANES))
    k_dup = _split_pair(kk)
    v_roll = pltpu.roll(vv, HEAD_DIM, 1)
    v_even = (jnp.where(low_k, vv, 1.0).astype(BF16), jnp.where(low_k, v_roll, 1.0).astype(BF16))
    v_odd = (jnp.where(low_k, 1.0, v_roll).astype(BF16), jnp.where(low_k, 1.0, vv).astype(BF16))
    scores = [_mm_nt(_head_query_rows(q, g), k_dup[g].astype(BF16)) for g in range(N_KV_HEADS)]
    return scores, (v_even, v_odd)


def _attention_rows_probs(scores, sinks_ref):
    r = scores[0].shape[0] // Q_PER_KV
    t_i = lax.broadcasted_iota(jnp.int32, (r, KEY_PAD), 0)
    s_i = lax.broadcasted_iota(jnp.int32, (r, KEY_PAD), 1)
    rel = t_i + WINDOW - s_i
    valid = (rel >= 0) & (rel < WINDOW)
    relf = rel.astype(F32)
    probs, sink_terms = [], []
    for head in range(N_HEADS):
        g, hh = divmod(head, Q_PER_KV)
        sh = jnp.where(valid, scores[g][hh * r:(hh + 1) * r] - ALIBI_SLOPES[head] * relf, -jnp.inf)
        sink = sinks_ref[head:head + 1, 0:1]
        m = jnp.maximum(jnp.max(sh, axis=-1, keepdims=True), sink)
        probs.append(jnp.exp(sh - m))
        sink_terms.append(jnp.broadcast_to(jnp.exp(sink - m), (r, LANES)))
    return probs, sink_terms


def _attention_rows_output(probs, sink_terms, v_operands):
    v_even, v_odd = v_operands
    r = probs[0].shape[0]
    low = _lane_low_half((r, LANES))
    outs = []
    for g in range(N_KV_HEADS):
        mine = probs[g * Q_PER_KV:(g + 1) * Q_PER_KV]
        o_even = _mm(jnp.concatenate(mine[0::2], axis=0).astype(BF16), v_even[g])
        o_odd = _mm(jnp.concatenate(mine[1::2], axis=0).astype(BF16), v_odd[g])
        for jj in range(Q_PER_KV // 2):
            oe, oo = o_even[jj * r:(jj + 1) * r], o_odd[jj * r:(jj + 1) * r]
            head = g * Q_PER_KV + 2 * jj
            den = (pltpu.roll(jnp.where(low, oo, oe), HEAD_DIM, 1)
                   + jnp.where(low, sink_terms[head], sink_terms[head + 1]))
            outs.append(jnp.where(low, oe, oo) / den)
    return jnp.concatenate(outs, axis=1)


def _fill_attention_bias(bias_scr):
    s_i = lax.broadcasted_iota(jnp.int32, (KEY_PAD, WINDOW), 0)
    t_i = lax.broadcasted_iota(jnp.int32, (KEY_PAD, WINDOW), 1)
    rel = t_i + WINDOW - s_i
    valid = (rel >= 0) & (rel < WINDOW)
    relf = rel.astype(F32)
    for head in range(N_HEADS):
        bias_scr[head * KEY_PAD:(head + 1) * KEY_PAD, :] = jnp.where(
            valid, (-ALIBI_SLOPES[head] * LOG2_E) * relf, -jnp.inf)


def _attention_cols_scores(q, kk):
    k_dup = _split_pair(kk)
    qs = q * (HEAD_DIM ** -0.5 * LOG2_E)
    return [_mm_nt(k_dup[g].astype(BF16), _head_query_rows(qs, g)) for g in range(N_KV_HEADS)]


def _attention_cols_probs(scores, bias_scr, sinks_ref, first_block):
    r = scores[0].shape[1] // Q_PER_KV
    mask_prev = None if first_block is None else jnp.where(first_block, -jnp.inf, 0.0)
    out = []
    for g in range(N_KV_HEADS):
        probs, sink_terms = [], []
        for hh in range(Q_PER_KV):
            head = g * Q_PER_KV + hh
            sh = scores[g][:, hh * r:(hh + 1) * r] + bias_scr[head * KEY_PAD:(head + 1) * KEY_PAD, :]
            prev = sh[0:WINDOW] if mask_prev is None else sh[0:WINDOW] + mask_prev
            cur = sh[WINDOW:KEY_PAD]
            sink = sinks_ref[head:head + 1, :] * LOG2_E
            m = jnp.maximum(jnp.maximum(jnp.max(prev, axis=0, keepdims=True),
                                        jnp.max(cur, axis=0, keepdims=True)), sink)
            probs.append(jnp.concatenate([jnp.exp2(prev - m), jnp.exp2(cur - m)], axis=0))
            sink_terms.append(jnp.exp2(sink - m))
        out.append((jnp.concatenate(probs, axis=1).astype(BF16), jnp.concatenate(sink_terms, axis=1)))
    return out


def _attention_cols_pv(probs, vv):
    vt = vv.T
    ones = jnp.ones((HEAD_DIM, KEY_PAD), F32)
    return [_mm(jnp.concatenate([vt[g * HEAD_DIM:(g + 1) * HEAD_DIM], ones], axis=0).astype(BF16), probs[g][0])
            for g in range(N_KV_HEADS)]


def _attention_cols_output(pv, probs):
    r = pv[0].shape[1] // Q_PER_KV
    outs = []
    for g in range(N_KV_HEADS):
        den = pv[g][HEAD_DIM:HEAD_DIM + 1, :] + probs[g][1]
        on = pv[g][0:HEAD_DIM, :] * (1.0 / den)
        for jj in range(Q_PER_KV // 2):
            pair = jnp.concatenate([on[:, (2 * jj) * r:(2 * jj + 1) * r],
                                    on[:, (2 * jj + 1) * r:(2 * jj + 2) * r]], axis=0)
            outs.append(pair.T)
    return jnp.concatenate(outs, axis=1)


def _gated_group_norm(y, z, gain):
    y = y * _silu(z)
    gw = D_SSM // N_SSM_GROUPS
    parts = [_rmsnorm(y[:, g * gw:(g + 1) * gw], gain[:, g * gw:(g + 1) * gw]) for g in range(N_SSM_GROUPS)]
    return jnp.concatenate(parts, axis=1)


def _prompt_tile_mixer(first, z_ref, xbc_ref, xprev_ref, dt_ref, q_ref, k_ref, kprev_ref, v_ref, vprev_ref,
                       cw_ref, cb_ref, dtb_ref, alog_ref, dskip_ref, snorm_ref, sinks_ref, expand_ref,
                       st_scr, xe_scr, bias_scr, store_mix, state_after):
    t = SSD_CHUNK
    rows = xbc_ref.shape[0]
    hw = D_SSM // N_SSM_GROUPS
    nbc = N_SSM_GROUPS * D_STATE

    dt_raw = _mm(jnp.concatenate(_split3(dt_ref[...] + dtb_ref[...]), axis=1), expand_ref[...])

    xe_scr[0:SUBLANES, :] = jnp.where(first, 0.0, xprev_ref[...])
    xe_scr[SUBLANES:SUBLANES + rows, :] = xbc_ref[...]
    cw = cw_ref[...]
    a_rep = -jnp.exp(alog_ref[...]) * LOG2_E
    causal = lax.broadcasted_iota(jnp.int32, (t, t), 0) >= lax.broadcasted_iota(jnp.int32, (t, t), 1)
    ltri = causal.astype(BF16)
    low_b = _lane_low_half((t, LANES))
    chunks = []
    for c in range(rows // t):
        sl = slice(c * t, (c + 1) * t)
        conv = cb_ref[...] + cw[CONV_W - 1:CONV_W] * xbc_ref[sl, :]
        for back in range(1, CONV_W):
            r0 = SUBLANES - back + c * t
            conv = conv + cw[CONV_W - 1 - back:CONV_W - back] * xe_scr[r0:r0 + t, :]
        xc = _silu(conv)
        dt = _softplus(dt_raw[sl])
        chunks.append(dict(
            sl=sl, dt=dt, xs=xc[:, 0:D_SSM], bm=xc[:, D_SSM:D_SSM + nbc], cm=xc[:, D_SSM + nbc:CONV_DIM],
            cum3=_mm(ltri, jnp.concatenate(_split3(dt * a_rep), axis=1))))
        yield

    for c, ch in enumerate(chunks):
        sl, xs, bm, cm = ch['sl'], ch['xs'], ch['bm'], ch['cm']
        cum3 = ch['cum3']
        cum = cum3[:, 0:D_SSM] + cum3[:, D_SSM:2 * D_SSM] + cum3[:, 2 * D_SSM:3 * D_SSM]
        xdt = xs * ch['dt']
        xw_b = (xdt * jnp.exp2(cum[t - 1:t, :] - cum)).astype(BF16)
        cg_b = [cm[:, g * D_STATE:(g + 1) * D_STATE].astype(BF16) for g in range(N_SSM_GROUPS)]
        bg = [bm[:, g * D_STATE:(g + 1) * D_STATE] for g in range(N_SSM_GROUPS)]
        k_before = kprev_ref[...] if c == 0 else k_ref[(c - 1) * t:c * t, :]
        v_before = vprev_ref[...] if c == 0 else v_ref[(c - 1) * t:c * t, :]
        ch.update(
            cum=cum, ecum=jnp.exp2(cum), xdt_b=xdt.astype(BF16), cg_b=cg_b,
            cb_t=[_mm_nt(cg_b[g], bg[g].astype(BF16)) for g in range(N_SSM_GROUPS)],
            st_new=[_mm(bg[g].T.astype(BF16), xw_b[:, g * hw:(g + 1) * hw]) for g in range(N_SSM_GROUPS)],
            att_scores=_attention_cols_scores(q_ref[sl, :], jnp.concatenate([k_before, k_ref[sl, :]], axis=0)),
            vv=jnp.concatenate([v_before, v_ref[sl, :]], axis=0))
        yield

    st = jnp.where(first, 0.0, st_scr[...])
    for c, ch in enumerate(chunks):
        y_in = []
        for j in range(D_SSM // LANES):
            cb_t = ch['cb_t'][j // (hw // LANES)]
            scores = []
            for col in _split_pair(ch['cum'][:, j * LANES:(j + 1) * LANES]):
                decay = jnp.exp2(jnp.where(causal, col - col.T, -jnp.inf))
                scores.append((cb_t * decay).astype(BF16))
            xp = ch['xdt_b'][:, j * LANES:(j + 1) * LANES]
            zero = jnp.zeros_like(xp)
            rhs = jnp.concatenate([jnp.where(low_b, xp, zero), jnp.where(low_b, zero, xp)], axis=0)
            y_in.append(_mm(jnp.concatenate(scores, axis=1), rhs))
        yield

        probs = _attention_cols_probs(ch['att_scores'], bias_scr, sinks_ref, first if c == 0 else None)
        pv = _attention_cols_pv(probs, ch['vv'])
        st_b = st.astype(BF16)
        y_off = [_mm(ch['cg_b'][g], st_b[:, g * hw:(g + 1) * hw]) for g in range(N_SSM_GROUPS)]
        yield

        y = (jnp.concatenate(y_in, axis=1) + jnp.concatenate(y_off, axis=1) * ch['ecum']
             + dskip_ref[...] * ch['xs'])
        st = st * ch['ecum'][t - 1:t, :] + jnp.concatenate(ch['st_new'], axis=1)
        ssd_out = _gated_group_norm(y, z_ref[ch['sl'], :], snorm_ref[...])
        yield

        store_mix(ch['sl'], ssd_out, _attention_cols_output(pv, probs))
        yield
    st_scr[...] = st
    state_after.append(st)


def _mixer_prompt_kernel(*refs, tiles_per_seq):
    mix_ref, state_ref, st_scr, xe_scr, bias_scr = refs[-5:]
    j = pl.program_id(0)

    @pl.when(j == 0)
    def _():
        _fill_attention_bias(bias_scr)

    def store_mix(rows, ssd_out, attn_out):
        mix_ref[rows, 0:D_SSM] = ssd_out.astype(mix_ref.dtype)
        mix_ref[rows, D_SSM:D_SSM + D_ATTN] = attn_out.astype(mix_ref.dtype)

    state_after = []
    for _ in _prompt_tile_mixer(lax.rem(j, tiles_per_seq) == 0, *refs[:-5], st_scr, xe_scr, bias_scr,
                                store_mix, state_after):
        pass
    st = state_after[0]

    @pl.when(lax.rem(j, tiles_per_seq) == tiles_per_seq - 1)
    def _():
        for jb in range(D_SSM // LANES):
            state_ref[jb * LANES:(jb + 1) * LANES, :] = st[:, jb * LANES:(jb + 1) * LANES].T


def _mixer_consts(pw):
    return (pw['conv_w'], pw['conv_b'], pw['dt_bias'], pw['a_log_rep'], pw['d_skip'], pw['ssm_norm'], pw['sinks'],
            pw['head_expand'])


def _mixer_prompt(z, xbc, dtr, q, k, v, pw, layer, batch, seq):
    t = SSD_CHUNK
    tile = MIXER_TILE
    tiles_per_seq = seq // tile
    cur = lambda width: pl.BlockSpec((tile, width), lambda i: (i, 0))
    before = lambda rows, width: pl.BlockSpec((rows, width), lambda i: (jnp.maximum(i * (tile // rows) - 1, 0), 0))
    consts = _mixer_consts(pw)
    return pl.pallas_call(
        functools.partial(_mixer_prompt_kernel, tiles_per_seq=tiles_per_seq),
        grid=(batch * tiles_per_seq,),
        in_specs=[cur(D_SSM), cur(CONV_DIM), before(SUBLANES, CONV_DIM), cur(LANES), cur(D_ATTN),
                  cur(KV_WIDTH), before(t, KV_WIDTH), cur(KV_WIDTH), before(t, KV_WIDTH)]
                 + [_layer_block(cst, layer) for cst in consts],
        out_specs=[pl.BlockSpec((tile, D_SSM + D_ATTN), lambda i: (i, 0)),
                   pl.BlockSpec((None, D_SSM, D_STATE), lambda i: (i // tiles_per_seq, 0, 0))],
        out_shape=[jax.ShapeDtypeStruct((batch * seq, D_SSM + D_ATTN), BF16),
                   jax.ShapeDtypeStruct((batch, D_SSM, D_STATE), F32)],
        scratch_shapes=[pltpu.VMEM((D_STATE, D_SSM), F32),
                        pltpu.VMEM((SUBLANES + tile, CONV_DIM), F32),
                        pltpu.VMEM((N_HEADS * KEY_PAD, WINDOW), F32)],
        compiler_params=pltpu.CompilerParams(dimension_semantics=("arbitrary",),
                                             vmem_limit_bytes=VMEM_LIMIT),
        name="mixer_prompt",
    )(z, xbc, xbc, dtr, q, k, k, v, v, *consts)


N_SAMPLE_IN = 18


def _put_lane_blocks(scr, col0, rows, val):
    for j in range(val.shape[1] // LANES):
        scr[col0 // LANES + j, rows, :] = val[:, j * LANES:(j + 1) * LANES]


def _get_lane_blocks(scr, col0, width, rows):
    return jnp.concatenate([scr[col0 // LANES + j, rows, :] for j in range(width // LANES)], axis=1)


def _mixer_sample_kernel(*refs):
    (z_ref, xbc_ref, dt_ref, q_ref, k_ref, v_ref, sst_ref, sconv_ref, ck_ref, cv_ref,
     cw_ref, cb_ref, dtb_ref, alog_ref, dskip_ref, snorm_ref, sinks_ref, expand_ref) = refs[:N_SAMPLE_IN]
    (mix_ref, sst_out_ref, ck_out_ref, cv_out_ref,
     ypart_scr, ecum_scr, cdec_scr, xw_scr, xt_scr, bmat_scr, seq_scr, res_scr, kk_scr, vv_scr) = refs[-14:]
    dl, bb = xbc_ref.shape[0], xbc_ref.shape[1]
    hw = D_SSM // N_SSM_GROUPS
    nbc = N_SSM_GROUPS * D_STATE
    c_q, c_c, c_k, c_v = 0, D_ATTN, D_ATTN + nbc, D_ATTN + nbc + KV_WIDTH
    xw_scr[...] = jnp.zeros_like(xw_scr)
    bmat_scr[...] = jnp.zeros_like(bmat_scr)
    seq_scr[...] = jnp.zeros_like(seq_scr)
    kk_scr[...] = jnp.zeros_like(kk_scr)
    vv_scr[...] = jnp.zeros_like(vv_scr)

    cw = cw_ref[...]
    a_rep = -jnp.exp(alog_ref[...])
    xs, bm, cm, xdt, cum = [], [], [], [], []
    for t in range(dl):
        conv = cb_ref[...] + cw[CONV_W - 1:CONV_W] * xbc_ref[t]
        for back in range(1, CONV_W):
            src = xbc_ref[t - back] if t >= back else sconv_ref[CONV_W - 1 + t - back]
            conv = conv + cw[CONV_W - 1 - back:CONV_W - back] * src
        xc = _silu(conv)
        xs.append(xc[:, 0:D_SSM])
        bm.append(xc[:, D_SSM:D_SSM + nbc])
        cm.append(xc[:, D_SSM + nbc:CONV_DIM])
        dt_c = _softplus(dt_ref[t] + dtb_ref[...])
        dt = _mm(jnp.concatenate(_split3(dt_c), axis=1), expand_ref[...])
        cum.append(dt * a_rep if t == 0 else cum[-1] + dt * a_rep)
        xdt.append(xs[t] * dt)
    cdec_scr[...] = jnp.exp(cum[dl - 1])
    for t in range(dl):
        y = dskip_ref[...] * xs[t]
        for s in range(t + 1):
            prod = cm[t] * bm[s]
            dots = jnp.concatenate(
                [jnp.broadcast_to(jnp.sum(prod[:, g * D_STATE:(g + 1) * D_STATE], axis=-1, keepdims=True), (bb, hw))
                 for g in range(N_SSM_GROUPS)], axis=1)
            y = y + (dots * xdt[s] if s == t else dots * jnp.exp(cum[t] - cum[s]) * xdt[s])
        rows_t = slice(t * bb, (t + 1) * bb)
        ypart_scr[rows_t, :] = y
        ecum_scr[rows_t, :] = jnp.exp(cum[t])
        xw_scr[rows_t, :] = xdt[t] * jnp.exp(cum[dl - 1] - cum[t])
        bmat_scr[rows_t, :] = bm[t]
        own_rows = pl.ds(t, bb, stride=SEQ_ROWS)
        _put_lane_blocks(seq_scr, c_q, own_rows, q_ref[t] * (HEAD_DIM ** -0.5))
        _put_lane_blocks(seq_scr, c_c, own_rows, cm[t])
        _put_lane_blocks(seq_scr, c_k, own_rows, k_ref[t])
        _put_lane_blocks(seq_scr, c_v, own_rows, v_ref[t])
    xw = xw_scr[...]
    for j in range(D_SSM // LANES):
        xt_scr[j * LANES:(j + 1) * LANES, :] = xw[:, j * LANES:(j + 1) * LANES].T.astype(BF16)

    block_row = lax.broadcasted_iota(jnp.int32, (SSD_CHUNK, D_STATE), 0)

    def state_matmuls(b):
        my_rows = pl.ds(pl.multiple_of(b * SEQ_ROWS, SEQ_ROWS), SEQ_ROWS)
        state = sst_ref[b]
        state_b = state.astype(BF16)
        c_rows = jnp.concatenate([_get_lane_blocks(seq_scr, c_c, nbc, my_rows),
                                  jnp.zeros((SEQ_ROWS, nbc), F32)], axis=0).astype(BF16)
        y_off = [_mm_nt(c_rows[:, g * D_STATE:(g + 1) * D_STATE], state_b[g * hw:(g + 1) * hw, :])[0:SEQ_ROWS]
                 for g in range(N_SSM_GROUPS)]
        is_mine = (block_row & (bb - 1)) == b
        upd = [_mm(xt_scr[g * hw:(g + 1) * hw, :],
                   jnp.where(is_mine, bmat_scr[:, g * D_STATE:(g + 1) * D_STATE], 0.0).astype(BF16))
               for g in range(N_SSM_GROUPS)]
        return my_rows, state, y_off, upd

    def attention_scores(b, slot, my_rows):
        kk_scr[slot, 0:WINDOW, :] = ck_ref[b]
        kk_scr[slot, WINDOW:WINDOW + SEQ_ROWS, :] = _get_lane_blocks(seq_scr, c_k, KV_WIDTH, my_rows)
        vv_scr[slot, 0:WINDOW, :] = cv_ref[b]
        vv_scr[slot, WINDOW:WINDOW + SEQ_ROWS, :] = _get_lane_blocks(seq_scr, c_v, KV_WIDTH, my_rows)
        ck_out_ref[b] = kk_scr[slot, dl:dl + WINDOW, :]
        cv_out_ref[b] = vv_scr[slot, dl:dl + WINDOW, :]
        return _attention_rows_scores(_get_lane_blocks(seq_scr, c_q, D_ATTN, my_rows), kk_scr[slot], vv_scr[slot])

    def store_state(b, my_rows, state, y_off, upd):
        _put_lane_blocks(res_scr, 0, my_rows, jnp.concatenate(y_off, axis=1))
        decay = jnp.broadcast_to(cdec_scr[pl.ds(b, 1), :], (SUBLANES, D_SSM))
        for j in range(D_SSM // LANES):
            g, jj = divmod(j, hw // LANES)
            for half, dec in enumerate(_split_pair(decay[:, j * LANES:(j + 1) * LANES])):
                h0 = j * LANES + half * HEAD_DIM
                u0 = jj * LANES + half * HEAD_DIM
                sst_out_ref[b, h0:h0 + HEAD_DIM, :] = (
                    state[h0:h0 + HEAD_DIM] * jnp.broadcast_to(dec[0:1], (HEAD_DIM, D_STATE))
                    + upd[g][u0:u0 + HEAD_DIM])

    def per_group(i, carry):
        seqs = [i * SAMPLE_UNROLL + slot for slot in range(SAMPLE_UNROLL)]
        parts = [state_matmuls(b) for b in seqs]
        scored = [attention_scores(b, slot, parts[slot][0]) for slot, b in enumerate(seqs)]
        for slot, b in enumerate(seqs):
            store_state(b, *parts[slot])
        weights = [_attention_rows_probs(scores, sinks_ref) for scores, _ in scored]
        for slot in range(SAMPLE_UNROLL):
            att = _attention_rows_output(*weights[slot], scored[slot][1])
            _put_lane_blocks(res_scr, D_SSM, parts[slot][0], att)
        return carry

    lax.fori_loop(0, bb // SAMPLE_UNROLL, per_group, 0)

    for t in range(dl):
        own_rows = pl.ds(t, bb, stride=SEQ_ROWS)
        rows_t = slice(t * bb, (t + 1) * bb)
        y = ypart_scr[rows_t, :] + _get_lane_blocks(res_scr, 0, D_SSM, own_rows) * ecum_scr[rows_t, :]
        mix_ref[t, :, 0:D_SSM] = _gated_group_norm(y, z_ref[t], snorm_ref[...])
        mix_ref[t, :, D_SSM:D_SSM + D_ATTN] = _get_lane_blocks(res_scr, D_SSM, D_ATTN, own_rows)


def _mixer_sample(z, xbc, dtr, q, k, v, state_ssm, state_conv, cache_k, cache_v, pw, layer, carried):
    depth = state_ssm.shape[0]
    dl, nb = xbc.shape[0], xbc.shape[1]
    bb = min(SAMPLE_SEQS_PER_STEP, nb)
    assert bb & (bb - 1) == 0 and dl * bb <= SSD_CHUNK and dl <= SEQ_ROWS and bb % SAMPLE_UNROLL == 0
    tok = lambda width: pl.BlockSpec((dl, bb, width), lambda i: (0, i, 0))
    lblk = lambda *dims: pl.BlockSpec((None, bb) + dims, lambda i: (layer, i) + (0,) * len(dims))
    consts = _mixer_consts(pw)
    operands = (z, xbc, dtr, q, k, v, state_ssm, state_conv, cache_k, cache_v) + consts
    assert len(operands) == N_SAMPLE_IN
    seq_cols = D_ATTN + N_SSM_GROUPS * D_STATE + 2 * KV_WIDTH
    return pl.pallas_call(
        _mixer_sample_kernel,
        grid=(nb // bb,),
        in_specs=[tok(D_SSM), tok(CONV_DIM), tok(LANES), tok(D_ATTN), tok(KV_WIDTH), tok(KV_WIDTH),
                  lblk(D_SSM, D_STATE),
                  pl.BlockSpec((None, CONV_W - 1, bb, CONV_DIM), lambda i: (layer, 0, i, 0)),
                  lblk(WINDOW, KV_WIDTH), lblk(WINDOW, KV_WIDTH)]
                 + [_layer_block(cst, layer) for cst in consts]
                 + [pl.BlockSpec(memory_space=pl.ANY)] * len(carried),
        out_specs=[tok(D_SSM + D_ATTN), lblk(D_SSM, D_STATE), lblk(WINDOW, KV_WIDTH), lblk(WINDOW, KV_WIDTH)],
        out_shape=[jax.ShapeDtypeStruct((dl, nb, D_SSM + D_ATTN), F32),
                   jax.ShapeDtypeStruct((depth, nb, D_SSM, D_STATE), F32),
                   jax.ShapeDtypeStruct((depth, nb, WINDOW, KV_WIDTH), F32),
                   jax.ShapeDtypeStruct((depth, nb, WINDOW, KV_WIDTH), F32)],
        input_output_aliases={N_SAMPLE_IN + n: 1 + n for n in range(len(carried))},
        scratch_shapes=[pltpu.VMEM((dl * bb, D_SSM), F32),
                        pltpu.VMEM((dl * bb, D_SSM), F32),
                        pltpu.VMEM((bb, D_SSM), F32),
                        pltpu.VMEM((SSD_CHUNK, D_SSM), F32),
                        pltpu.VMEM((D_SSM, SSD_CHUNK), BF16),
                        pltpu.VMEM((SSD_CHUNK, N_SSM_GROUPS * D_STATE), F32),
                        pltpu.VMEM((seq_cols // LANES, bb * SEQ_ROWS, LANES), F32),
                        pltpu.VMEM(((D_SSM + D_ATTN) // LANES, bb * SEQ_ROWS, LANES), F32),
                        pltpu.VMEM((SAMPLE_UNROLL, KEY_PAD, KV_WIDTH), F32),
                        pltpu.VMEM((SAMPLE_UNROLL, KEY_PAD, KV_WIDTH), F32)],
        compiler_params=pltpu.CompilerParams(dimension_semantics=("arbitrary",),
                                             vmem_limit_bytes=VMEM_LIMIT),
        name="mixer_sample",
    )(*operands, *carried)


def _prepare_weights(p):
    bf = lambda w: w.astype(BF16)
    vec = lambda v: v[:, None, :]

    rep = lambda v: jnp.repeat(v, HEAD_DIM, axis=-1)[:, None, :]
    lane_pad = lambda v: jnp.pad(v, ((0, 0), (0, LANES - v.shape[-1])))[:, None, :]
    cuts = np.cumsum([D_SSM, CONV_DIM, N_SSM_HEADS, D_ATTN, KV_WIDTH])
    wz, wxbc, wdt, wq, wk, wv = jnp.split(bf(p['w_in']), cuts, axis=2)
    depth = p['w_in'].shape[0]
    seg = np.kron(np.eye(D_ATTN // HEAD_DIM, dtype=np.float32), np.ones((HEAD_DIM, HEAD_DIM), np.float32))
    heads = np.eye(LANES, dtype=np.float32)[:, :N_SSM_HEADS]
    expand = np.tile(np.kron(heads, np.ones((1, HEAD_DIM), np.float32)), (3, 1))
    return {
        'g_ffn1': vec(p['g_ffn1']), 'g_mix': vec(p['g_mix']), 'g_ffn2': vec(p['g_ffn2']), 'g_ple': vec(p['g_ple']),
        'w1_a': bf(p['w1_a']), 'w3_a': bf(p['w3_a']), 'w2_a': bf(p['w2_a']),
        'w1_b': bf(p['w1_b']), 'w3_b': bf(p['w3_b']), 'w2_b': bf(p['w2_b']),
        'w_in': jnp.concatenate(
            [wz, wxbc, wq, wk, wv, jnp.pad(wdt, ((0, 0), (0, 0), (0, LANES - N_SSM_HEADS)))], axis=2),
        'w_out': bf(p['w_out']), 'w_ple_gate': bf(p['w_ple_gate']), 'w_ple_proj': bf(p['w_ple_proj']),
        'q_gain': vec(jnp.tile(p['q_norm'], (1, N_HEADS))),
        'k_gain': vec(jnp.tile(p['k_norm'], (1, N_KV_HEADS))),
        'seg': jnp.broadcast_to(jnp.asarray(seg, BF16), (depth,) + seg.shape),
        'conv_w': p['conv_w'], 'conv_b': vec(p['conv_b']),
        'dt_bias': lane_pad(p['dt_bias']), 'a_log_rep': rep(p['a_log']),
        'd_skip': rep(p['d_skip']),
        'head_expand': jnp.broadcast_to(jnp.asarray(expand, BF16), (depth,) + expand.shape),
        'ssm_norm': vec(p['ssm_norm']),
        'sinks': jnp.broadcast_to(p['sinks'][:, :, None], (depth, N_HEADS, LANES)),
    }


def kernel(x_prompt, x_sample, state_ssm, state_conv, cache_k_win, cache_v_win, p_prompt, p_sample, g_ffn1, w1_a, w3_a, w2_a, g_mix, w_in, conv_w, conv_b, dt_bias, a_log, d_skip, ssm_norm, q_norm, k_norm, sinks, w_out, g_ffn2, w1_b, w3_b, w2_b, g_ple, w_ple_gate, w_ple_proj):
    params = dict(g_ffn1=g_ffn1, w1_a=w1_a, w3_a=w3_a, w2_a=w2_a, g_mix=g_mix, w_in=w_in, conv_w=conv_w,
                  conv_b=conv_b, dt_bias=dt_bias, a_log=a_log, d_skip=d_skip, ssm_norm=ssm_norm, q_norm=q_norm,
                  k_norm=k_norm, sinks=sinks, w_out=w_out, g_ffn2=g_ffn2, w1_b=w1_b, w3_b=w3_b, w2_b=w2_b,
                  g_ple=g_ple, w_ple_gate=w_ple_gate, w_ple_proj=w_ple_proj)
    depth = w_in.shape[0]
    bp, seq, _ = x_prompt.shape
    bs, dl, _ = x_sample.shape
    assert seq % SSD_CHUNK == 0 and seq >= WINDOW and dl >= CONV_W - 1 and dl <= SUBLANES
    assert (bp * seq) % TOKEN_TILE == 0 and (bs * dl) % min(TOKEN_TILE, bs * dl) == 0

    pw = _prepare_weights(params)
    sst_in = state_ssm.reshape(depth, bs, D_SSM, D_STATE)
    ck_in = cache_k_win.reshape(depth, bs, WINDOW, KV_WIDTH)
    cv_in = cache_v_win.reshape(depth, bs, WINDOW, KV_WIDTH)
    sconv_in = jnp.swapaxes(state_conv, 1, 2)
    pe_p = p_prompt.reshape(depth, bp * seq, D_PLE)
    pe_s = jnp.swapaxes(p_sample, 1, 2).reshape(depth, dl * bs, D_PLE)
    hp = x_prompt.reshape(bp * seq, D_MODEL)
    hs = jnp.swapaxes(x_sample, 0, 1).reshape(dl * bs, D_MODEL)
    ssm_p, conv_p, k_p, v_p, conv_s = [], [], [], [], []
    carried = ()
    for l in range(depth):
        hp, zp, xbcp, dtp, qp, kp, vp = _stage_a(hp, pw, l)
        hs, zs, xbcs, dts, qs, ks, vs = _stage_a(hs, pw, l)
        mixp, sp = _mixer_prompt(zp, xbcp, dtp, qp, kp, vp, pw, l, bp, seq)
        hp = _stage_c(hp, mixp, pe_p, pw, l)
        tm3 = lambda a: a.reshape(dl, bs, a.shape[-1])
        mixs, *carried = _mixer_sample(tm3(zs), tm3(xbcs), tm3(dts), tm3(qs), tm3(ks), tm3(vs),
                                       sst_in, sconv_in, ck_in, cv_in, pw, l, tuple(carried))
        hs = _stage_c(hs, mixs.reshape(dl * bs, D_MODEL), pe_s, pw, l)
        ssm_p.append(sp.reshape(bp, N_SSM_HEADS, HEAD_DIM, D_STATE))
        conv_p.append(xbcp.reshape(bp, seq, CONV_DIM)[:, seq - (CONV_W - 1):])
        last_window = lambda a: a.reshape(bp, seq, KV_WIDTH)[:, seq - WINDOW:].reshape(
            bp, WINDOW, N_KV_HEADS, HEAD_DIM)
        k_p.append(last_window(kp))
        v_p.append(last_window(vp))
        conv_s.append(jnp.swapaxes(tm3(xbcs)[dl - (CONV_W - 1):], 0, 1))
    ssm_s, k_s, v_s = carried
    return (hp.reshape(bp, seq, D_MODEL), jnp.swapaxes(hs.reshape(dl, bs, D_MODEL), 0, 1),
            jnp.stack(ssm_p), jnp.stack(conv_p), jnp.stack(k_p), jnp.stack(v_p),
            ssm_s.reshape(depth, bs, N_SSM_HEADS, HEAD_DIM, D_STATE), jnp.stack(conv_s),
            k_s.reshape(depth, bs, WINDOW, N_KV_HEADS, HEAD_DIM),
            v_s.reshape(depth, bs, WINDOW, N_KV_HEADS, HEAD_DIM))
```

```python
import functools

import numpy as np
import jax
import jax.numpy as jnp
from jax import lax
from jax.experimental import pallas as pl
from jax.experimental.pallas import tpu as pltpu

F32 = jnp.float32
BF16 = jnp.bfloat16

D_MODEL = 1024
HEAD_DIM = 64
D_SSM = 512
N_SSM_HEADS = 8
N_SSM_GROUPS = 2
D_STATE = 128
CONV_W = 4
CONV_DIM = D_SSM + 2 * N_SSM_GROUPS * D_STATE
SSD_CHUNK = 128
D_ATTN = 512
N_HEADS = 8
N_KV_HEADS = 2
Q_PER_KV = N_HEADS // N_KV_HEADS
KV_WIDTH = N_KV_HEADS * HEAD_DIM
WINDOW = 128
D_FF = 2752
D_PLE = 256
RMS_EPS = 1e-6

LANES = 128
SUBLANES = 8
MXU_DIM = 256
FF_CHUNK = MXU_DIM
FF_FULL_CHUNKS = D_FF // FF_CHUNK
FF_TAIL = D_FF - FF_FULL_CHUNKS * FF_CHUNK
FF_TAIL_PAD = -(-FF_TAIL // MXU_DIM) * MXU_DIM
TOKEN_TILE = 512
STAGE_C_TILE = 1024
MIXER_TILE = 1024
SAMPLE_SEQS_PER_STEP = 16
SAMPLE_UNROLL = 8
SEQ_ROWS = SUBLANES
KEY_PAD = 2 * WINDOW
VMEM_LIMIT = 56 * 1024 * 1024

OFF_Z = 0
OFF_XBC = OFF_Z + D_SSM
OFF_Q = OFF_XBC + CONV_DIM
OFF_K = OFF_Q + D_ATTN
OFF_V = OFF_K + KV_WIDTH
OFF_DT = OFF_V + KV_WIDTH
D_PROJ_PAD = OFF_DT + LANES

ALIBI_SLOPES = tuple(float(s) for s in np.power(
    np.float32(2.0), -8.0 * np.arange(1, N_HEADS + 1, dtype=np.float32) / N_HEADS))

LOG2_E = 1.4426950408889634
_NT = (((1,), (1,)), ((), ()))


def _mm(a, b):
    return jnp.dot(a, b, preferred_element_type=F32)


def _mm_nt(a, b):
    return lax.dot_general(a, b, _NT, preferred_element_type=F32)


def _sigmoid(x):
    return 1.0 / (1.0 + jnp.exp(-x))


def _silu(x):
    return x * _sigmoid(x)


def _softplus(x):
    return jnp.maximum(x, 0.0) + jnp.log(1.0 + jnp.exp(-jnp.abs(x)))


def _rmsnorm(x, g):
    return x * lax.rsqrt(jnp.mean(x * x, axis=-1, keepdims=True) + RMS_EPS) * g


def _split3(x):
    hi = x.astype(BF16)
    rest = x - hi.astype(F32)
    mid = rest.astype(BF16)
    return hi, mid, (rest - mid.astype(F32)).astype(BF16)


def _stage_ff_tail(w1_ref, w3_ref, w2_ref, tail_refs):
    w1t_ref, w3t_ref, w2t_ref = tail_refs
    lo = FF_FULL_CHUNKS * FF_CHUNK
    for dst, src in ((w1t_ref, w1_ref), (w3t_ref, w3_ref)):
        dst[...] = jnp.zeros_like(dst)
        dst[:, 0:FF_TAIL] = src[:, lo:D_FF]
    w2t_ref[...] = jnp.zeros_like(w2t_ref)
    w2t_ref[0:FF_TAIL, :] = w2_ref[lo:D_FF, :]


def _ff_tail_scratch():
    return [pltpu.VMEM((D_MODEL, FF_TAIL_PAD), BF16), pltpu.VMEM((D_MODEL, FF_TAIL_PAD), BF16),
            pltpu.VMEM((FF_TAIL_PAD, D_MODEL), BF16)]


def _swiglu(xn, w1_ref, w3_ref, w2_ref, tail_refs):
    def chunk(acc, w1c, w3c, w2c):
        a = _mm(xn, w1c)
        b = _mm(xn, w3c)
        return acc + _mm((_silu(a) * b).astype(BF16), w2c)

    acc = jnp.zeros((xn.shape[0], D_MODEL), F32)
    for c in range(FF_FULL_CHUNKS):
        sl = slice(c * FF_CHUNK, (c + 1) * FF_CHUNK)
        acc = chunk(acc, w1_ref[:, sl], w3_ref[:, sl], w2_ref[sl, :])
    if FF_TAIL:
        acc = chunk(acc, *(r[...] for r in tail_refs))
    return acc


def _segment_sumsq(x, seg_ref):
    hi, mid, _ = _split3(x * x)
    seg = seg_ref[0:x.shape[1], 0:x.shape[1]]
    return _mm(hi, seg) + _mm(mid, seg)


def _stage_a_kernel(x_ref, g1_ref, w1_ref, w3_ref, w2_ref, gm_ref, win_ref, qg_ref, kg_ref, seg_ref,
                    h_ref, z_ref, xbc_ref, dt_ref, q_ref, k_ref, v_ref, *tail_refs):
    @pl.when(pl.program_id(0) == 0)
    def _():
        _stage_ff_tail(w1_ref, w3_ref, w2_ref, tail_refs)

    x = x_ref[...]
    h = x + 0.5 * _swiglu(_rmsnorm(x, g1_ref[...]).astype(BF16), w1_ref, w3_ref, w2_ref, tail_refs)
    h_ref[...] = h
    proj = _mm(_rmsnorm(h, gm_ref[...]).astype(BF16), win_ref[...])
    z_ref[...] = proj[:, OFF_Z:OFF_XBC]
    xbc_ref[...] = proj[:, OFF_XBC:OFF_Q]
    dt_ref[...] = proj[:, OFF_DT:D_PROJ_PAD]
    q = proj[:, OFF_Q:OFF_K]
    k = proj[:, OFF_K:OFF_V]
    q_ref[...] = q * lax.rsqrt(_segment_sumsq(q, seg_ref) * (1.0 / HEAD_DIM) + RMS_EPS) * qg_ref[...]
    k_ref[...] = k * lax.rsqrt(_segment_sumsq(k, seg_ref) * (1.0 / HEAD_DIM) + RMS_EPS) * kg_ref[...]
    v_ref[...] = proj[:, OFF_V:OFF_DT]


def _layer_block(arr, layer, **kwargs):
    return pl.BlockSpec((None,) + arr.shape[1:], lambda *_: (layer, 0, 0), **kwargs)


def _layer_resident(arr, layer):
    return _layer_block(arr, layer, pipeline_mode=pl.Buffered(1))


def _stage_a(x, pw, layer):
    n = x.shape[0]
    tm = min(TOKEN_TILE, n)
    row = lambda width: pl.BlockSpec((tm, width), lambda i: (i, 0))
    widths = (D_MODEL, D_SSM, CONV_DIM, LANES, D_ATTN, KV_WIDTH, KV_WIDTH)
    consts = (pw['g_ffn1'], pw['w1_a'], pw['w3_a'], pw['w2_a'], pw['g_mix'], pw['w_in'],
              pw['q_gain'], pw['k_gain'], pw['seg'])
    return pl.pallas_call(
        _stage_a_kernel,
        grid=(n // tm,),
        in_specs=[row(D_MODEL)] + [_layer_resident(c, layer) for c in consts],
        out_specs=[row(w) for w in widths],
        out_shape=[jax.ShapeDtypeStruct((n, w), F32) for w in widths],
        scratch_shapes=_ff_tail_scratch(),
        compiler_params=pltpu.CompilerParams(dimension_semantics=("arbitrary",),
                                             vmem_limit_bytes=VMEM_LIMIT),
        name="stage_a",
    )(x, *consts)


def _stage_c_kernel(h_ref, mix_ref, pe_ref, wout_ref, g2_ref, w1_ref, w3_ref, w2_ref, gp_ref,
                    wgate_ref, wproj_ref, o_ref, *tail_refs):
    @pl.when(pl.program_id(0) == 0)
    def _():
        _stage_ff_tail(w1_ref, w3_ref, w2_ref, tail_refs)

    o_ref[...] = _stage_c_compute(h_ref[...], mix_ref[...].astype(BF16), pe_ref, wout_ref, g2_ref, w1_ref,
                                  w3_ref, w2_ref, gp_ref, wgate_ref, wproj_ref, tail_refs)


def _stage_c_compute(h, mix, pe_ref, wout_ref, g2_ref, w1_ref, w3_ref, w2_ref, gp_ref, wgate_ref, wproj_ref,
                     tail_refs):
    h = h + _mm(mix, wout_ref[...])
    h = h + 0.5 * _swiglu(_rmsnorm(h, g2_ref[...]).astype(BF16), w1_ref, w3_ref, w2_ref, tail_refs)
    gate = _sigmoid(_mm(_rmsnorm(h, gp_ref[...]).astype(BF16), wgate_ref[...]))
    return h + gate * _mm(pe_ref[...].astype(BF16), wproj_ref[...])


def _stage_c(h, mix, pe, pw, layer):
    n = h.shape[0]
    tm = min(STAGE_C_TILE, n)
    row = lambda width: pl.BlockSpec((tm, width), lambda i: (i, 0))
    consts = (pw['w_out'], pw['g_ffn2'], pw['w1_b'], pw['w3_b'], pw['w2_b'], pw['g_ple'],
              pw['w_ple_gate'], pw['w_ple_proj'])
    return pl.pallas_call(
        _stage_c_kernel,
        grid=(n // tm,),
        in_specs=[row(D_MODEL), row(D_MODEL), pl.BlockSpec((None, tm, D_PLE), lambda i: (layer, i, 0))]
                 + [_layer_resident(c, layer) for c in consts],
        out_specs=row(D_MODEL),
        out_shape=jax.ShapeDtypeStruct((n, D_MODEL), F32),
        scratch_shapes=_ff_tail_scratch(),
        compiler_params=pltpu.CompilerParams(dimension_semantics=("arbitrary",),
                                             vmem_limit_bytes=VMEM_LIMIT),
        name="stage_c",
    )(h, mix, pe, *consts)


def _lane_low_half(shape):
    return lax.broadcasted_iota(jnp.int32, shape, len(shape) - 1) < HEAD_DIM


def _split_pair(x):
    rolled = pltpu.roll(x, HEAD_DIM, 1)
    low = _lane_low_half(x.shape)
    return jnp.where(low, x, rolled), jnp.where(low, rolled, x)


def _head_query_rows(q, g):
    low = _lane_low_half((q.shape[0], LANES))
    blocks = []
    for jj in range(Q_PER_KV // 2):
        qp = q[:, (g * 2 + jj) * LANES:(g * 2 + jj + 1) * LANES]
        blocks += [jnp.where(low, qp, 0.0), jnp.where(low, 0.0, qp)]
    return jnp.concatenate(blocks, axis=0).astype(BF16)


def _attention_rows_scores(q, kk, vv):
    low_k = _lane_low_half((KEY_PAD, LANES))
    k_dup = _split_pair(kk)
    v_roll = pltpu.roll(vv, HEAD_DIM, 1)
    v_even = (jnp.where(low_k, vv, 1.0).astype(BF16), jnp.where(low_k, v_roll, 1.0).astype(BF16))
    v_odd = (jnp.where(low_k, 1.0, v_roll).astype(BF16), jnp.where(low_k, 1.0, vv).astype(BF16))
    scores = [_mm_nt(_head_query_rows(q, g), k_dup[g].astype(BF16)) for g in range(N_KV_HEADS)]
    return scores, (v_even, v_odd)


def _attention_rows_probs(scores, sinks_ref):
    r = scores[0].shape[0] // Q_PER_KV
    t_i = lax.broadcasted_iota(jnp.int32, (r, KEY_PAD), 0)
    s_i = lax.broadcasted_iota(jnp.int32, (r, KEY_PAD), 1)
    rel = t_i + WINDOW - s_i
    valid = (rel >= 0) & (rel < WINDOW)
    relf = rel.astype(F32)
    probs, sink_terms = [], []
    for head in range(N_HEADS):
        g, hh = divmod(head, Q_PER_KV)
        sh = jnp.where(valid, scores[g][hh * r:(hh + 1) * r] - ALIBI_SLOPES[head] * relf, -jnp.inf)
        sink = sinks_ref[head:head + 1, 0:1]
        m = jnp.maximum(jnp.max(sh, axis=-1, keepdims=True), sink)
        probs.append(jnp.exp(sh - m))
        sink_terms.append(jnp.broadcast_to(jnp.exp(sink - m), (r, LANES)))
    return probs, sink_terms


def _attention_rows_output(probs, sink_terms, v_operands):
    v_even, v_odd = v_operands
    r = probs[0].shape[0]
    low = _lane_low_half((r, LANES))
    outs = []
    for g in range(N_KV_HEADS):
        mine = probs[g * Q_PER_KV:(g + 1) * Q_PER_KV]
        o_even = _mm(jnp.concatenate(mine[0::2], axis=0).astype(BF16), v_even[g])
        o_odd = _mm(jnp.concatenate(mine[1::2], axis=0).astype(BF16), v_odd[g])
        for jj in range(Q_PER_KV // 2):
            oe, oo = o_even[jj * r:(jj + 1) * r], o_odd[jj * r:(jj + 1) * r]
            head = g * Q_PER_KV + 2 * jj
            den = (pltpu.roll(jnp.where(low, oo, oe), HEAD_DIM, 1)
                   + jnp.where(low, sink_terms[head], sink_terms[head + 1]))
            outs.append(jnp.where(low, oe, oo) / den)
    return jnp.concatenate(outs, axis=1)


def _fill_attention_bias(bias_scr):
    s_i = lax.broadcasted_iota(jnp.int32, (KEY_PAD, WINDOW), 0)
    t_i = lax.broadcasted_iota(jnp.int32, (KEY_PAD, WINDOW), 1)
    rel = t_i + WINDOW - s_i
    valid = (rel >= 0) & (rel < WINDOW)
    relf = rel.astype(F32)
    for head in range(N_HEADS):
        bias_scr[head * KEY_PAD:(head + 1) * KEY_PAD, :] = jnp.where(
            valid, (-ALIBI_SLOPES[head] * LOG2_E) * relf, -jnp.inf)


def _attention_cols_scores(q, kk):
    k_dup = _split_pair(kk)
    qs = q * (HEAD_DIM ** -0.5 * LOG2_E)
    return [_mm_nt(k_dup[g].astype(BF16), _head_query_rows(qs, g)) for g in range(N_KV_HEADS)]


def _attention_cols_probs(scores, bias_scr, sinks_ref, first_block):
    r = scores[0].shape[1] // Q_PER_KV
    mask_prev = None if first_block is None else jnp.where(first_block, -jnp.inf, 0.0)
    out = []
    for g in range(N_KV_HEADS):
        probs, sink_terms = [], []
        for hh in range(Q_PER_KV):
            head = g * Q_PER_KV + hh
            sh = scores[g][:, hh * r:(hh + 1) * r] + bias_scr[head * KEY_PAD:(head + 1) * KEY_PAD, :]
            prev = sh[0:WINDOW] if mask_prev is None else sh[0:WINDOW] + mask_prev
            cur = sh[WINDOW:KEY_PAD]
            sink = sinks_ref[head:head + 1, :] * LOG2_E
            m = jnp.maximum(jnp.maximum(jnp.max(prev, axis=0, keepdims=True),
                                        jnp.max(cur, axis=0, keepdims=True)), sink)
            probs.append(jnp.concatenate([jnp.exp2(prev - m), jnp.exp2(cur - m)], axis=0))
            sink_terms.append(jnp.exp2(sink - m))
        out.append((jnp.concatenate(probs, axis=1).astype(BF16), jnp.concatenate(sink_terms, axis=1)))
    return out


def _attention_cols_pv(probs, vv):
    vt = vv.T
    ones = jnp.ones((HEAD_DIM, KEY_PAD), F32)
    return [_mm(jnp.concatenate([vt[g * HEAD_DIM:(g + 1) * HEAD_DIM], ones], axis=0).astype(BF16), probs[g][0])
            for g in range(N_KV_HEADS)]


def _attention_cols_output(pv, probs):
    r = pv[0].shape[1] // Q_PER_KV
    outs = []
    for g in range(N_KV_HEADS):
        den = pv[g][HEAD_DIM:HEAD_DIM + 1, :] + probs[g][1]
        on = pv[g][0:HEAD_DIM, :] * (1.0 / den)
        for jj in range(Q_PER_KV // 2):
            pair = jnp.concatenate([on[:, (2 * jj) * r:(2 * jj + 1) * r],
                                    on[:, (2 * jj + 1) * r:(2 * jj + 2) * r]], axis=0)
            outs.append(pair.T)
    return jnp.concatenate(outs, axis=1)


def _gated_group_norm(y, z, gain):
    y = y * _silu(z)
    gw = D_SSM // N_SSM_GROUPS
    parts = [_rmsnorm(y[:, g * gw:(g + 1) * gw], gain[:, g * gw:(g + 1) * gw]) for g in range(N_SSM_GROUPS)]
    return jnp.concatenate(parts, axis=1)


def _prompt_tile_mixer(first, z_ref, xbc_ref, xprev_ref, dt_ref, q_ref, k_ref, kprev_ref, v_ref, vprev_ref,
                       cw_ref, cb_ref, dtb_ref, alog_ref, dskip_ref, snorm_ref, sinks_ref, expand_ref,
                       st_scr, xe_scr, bias_scr, store_mix, state_after):
    t = SSD_CHUNK
    rows = xbc_ref.shape[0]
    hw = D_SSM // N_SSM_GROUPS
    nbc = N_SSM_GROUPS * D_STATE

    dt_raw = _mm(jnp.concatenate(_split3(dt_ref[...] + dtb_ref[...]), axis=1), expand_ref[...])

    xe_scr[0:SUBLANES, :] = jnp.where(first, 0.0, xprev_ref[...])
    xe_scr[SUBLANES:SUBLANES + rows, :] = xbc_ref[...]
    cw = cw_ref[...]
    a_rep = -jnp.exp(alog_ref[...]) * LOG2_E
    causal = lax.broadcasted_iota(jnp.int32, (t, t), 0) >= lax.broadcasted_iota(jnp.int32, (t, t), 1)
    ltri = causal.astype(BF16)
    low_b = _lane_low_half((t, LANES))
    chunks = []
    for c in range(rows // t):
        sl = slice(c * t, (c + 1) * t)
        conv = cb_ref[...] + cw[CONV_W - 1:CONV_W] * xbc_ref[sl, :]
        for back in range(1, CONV_W):
            r0 = SUBLANES - back + c * t
            conv = conv + cw[CONV_W - 1 - back:CONV_W - back] * xe_scr[r0:r0 + t, :]
        xc = _silu(conv)
        dt = _softplus(dt_raw[sl])
        chunks.append(dict(
            sl=sl, dt=dt, xs=xc[:, 0:D_SSM], bm=xc[:, D_SSM:D_SSM + nbc], cm=xc[:, D_SSM + nbc:CONV_DIM],
            cum3=_mm(ltri, jnp.concatenate(_split3(dt * a_rep), axis=1))))
        yield

    for c, ch in enumerate(chunks):
        sl, xs, bm, cm = ch['sl'], ch['xs'], ch['bm'], ch['cm']
        cum3 = ch['cum3']
        cum = cum3[:, 0:D_SSM] + cum3[:, D_SSM:2 * D_SSM] + cum3[:, 2 * D_SSM:3 * D_SSM]
        xdt = xs * ch['dt']
        xw_b = (xdt * jnp.exp2(cum[t - 1:t, :] - cum)).astype(BF16)
        cg_b = [cm[:, g * D_STATE:(g + 1) * D_STATE].astype(BF16) for g in range(N_SSM_GROUPS)]
        bg = [bm[:, g * D_STATE:(g + 1) * D_STATE] for g in range(N_SSM_GROUPS)]
        k_before = kprev_ref[...] if c == 0 else k_ref[(c - 1) * t:c * t, :]
        v_before = vprev_ref[...] if c == 0 else v_ref[(c - 1) * t:c * t, :]
        ch.update(
            cum=cum, ecum=jnp.exp2(cum), xdt_b=xdt.astype(BF16), cg_b=cg_b,
            cb_t=[_mm_nt(cg_b[g], bg[g].astype(BF16)) for g in range(N_SSM_GROUPS)],
            st_new=[_mm(bg[g].T.astype(BF16), xw_b[:, g * hw:(g + 1) * hw]) for g in range(N_SSM_GROUPS)],
            att_scores=_attention_cols_scores(q_ref[sl, :], jnp.concatenate([k_before, k_ref[sl, :]], axis=0)),
            vv=jnp.concatenate([v_before, v_ref[sl, :]], axis=0))
        yield

    st = jnp.where(first, 0.0, st_scr[...])
    for c, ch in enumerate(chunks):
        y_in = []
        for j in range(D_SSM // LANES):
            cb_t = ch['cb_t'][j // (hw // LANES)]
            scores = []
            for col in _split_pair(ch['cum'][:, j * LANES:(j + 1) * LANES]):
                decay = jnp.exp2(jnp.where(causal, col - col.T, -jnp.inf))
                scores.append((cb_t * decay).astype(BF16))
            xp = ch['xdt_b'][:, j * LANES:(j + 1) * LANES]
            zero = jnp.zeros_like(xp)
            rhs = jnp.concatenate([jnp.where(low_b, xp, zero), jnp.where(low_b, zero, xp)], axis=0)
            y_in.append(_mm(jnp.concatenate(scores, axis=1), rhs))
        yield

        probs = _attention_cols_probs(ch['att_scores'], bias_scr, sinks_ref, first if c == 0 else None)
        pv = _attention_cols_pv(probs, ch['vv'])
        st_b = st.astype(BF16)
        y_off = [_mm(ch['cg_b'][g], st_b[:, g * hw:(g + 1) * hw]) for g in range(N_SSM_GROUPS)]
        yield

        y = (jnp.concatenate(y_in, axis=1) + jnp.concatenate(y_off, axis=1) * ch['ecum']
             + dskip_ref[...] * ch['xs'])
        st = st * ch['ecum'][t - 1:t, :] + jnp.concatenate(ch['st_new'], axis=1)
        ssd_out = _gated_group_norm(y, z_ref[ch['sl'], :], snorm_ref[...])
        yield

        store_mix(ch['sl'], ssd_out, _attention_cols_output(pv, probs))
        yield
    st_scr[...] = st
    state_after.append(st)


def _mixer_prompt_kernel(*refs, tiles_per_seq):
    mix_ref, state_ref, st_scr, xe_scr, bias_scr = refs[-5:]
    j = pl.program_id(0)

    @pl.when(j == 0)
    def _():
        _fill_attention_bias(bias_scr)

    def store_mix(rows, ssd_out, attn_out):
        mix_ref[rows, 0:D_SSM] = ssd_out.astype(mix_ref.dtype)
        mix_ref[rows, D_SSM:D_SSM + D_ATTN] = attn_out.astype(mix_ref.dtype)

    state_after = []
    for _ in _prompt_tile_mixer(lax.rem(j, tiles_per_seq) == 0, *refs[:-5], st_scr, xe_scr, bias_scr,
                                store_mix, state_after):
        pass
    st = state_after[0]

    @pl.when(lax.rem(j, tiles_per_seq) == tiles_per_seq - 1)
    def _():
        for jb in range(D_SSM // LANES):
            state_ref[jb * LANES:(jb + 1) * LANES, :] = st[:, jb * LANES:(jb + 1) * LANES].T


def _mixer_consts(pw):
    return (pw['conv_w'], pw['conv_b'], pw['dt_bias'], pw['a_log_rep'], pw['d_skip'], pw['ssm_norm'], pw['sinks'],
            pw['head_expand'])


def _mixer_prompt(z, xbc, dtr, q, k, v, pw, layer, batch, seq):
    t = SSD_CHUNK
    tile = MIXER_TILE
    tiles_per_seq = seq // tile
    cur = lambda width: pl.BlockSpec((tile, width), lambda i: (i, 0))
    before = lambda rows, width: pl.BlockSpec((rows, width), lambda i: (jnp.maximum(i * (tile // rows) - 1, 0), 0))
    consts = _mixer_consts(pw)
    return pl.pallas_call(
        functools.partial(_mixer_prompt_kernel, tiles_per_seq=tiles_per_seq),
        grid=(batch * tiles_per_seq,),
        in_specs=[cur(D_SSM), cur(CONV_DIM), before(SUBLANES, CONV_DIM), cur(LANES), cur(D_ATTN),
                  cur(KV_WIDTH), before(t, KV_WIDTH), cur(KV_WIDTH), before(t, KV_WIDTH)]
                 + [_layer_block(cst, layer) for cst in consts],
        out_specs=[pl.BlockSpec((tile, D_SSM + D_ATTN), lambda i: (i, 0)),
                   pl.BlockSpec((None, D_SSM, D_STATE), lambda i: (i // tiles_per_seq, 0, 0))],
        out_shape=[jax.ShapeDtypeStruct((batch * seq, D_SSM + D_ATTN), BF16),
                   jax.ShapeDtypeStruct((batch, D_SSM, D_STATE), F32)],
        scratch_shapes=[pltpu.VMEM((D_STATE, D_SSM), F32),
                        pltpu.VMEM((SUBLANES + tile, CONV_DIM), F32),
                        pltpu.VMEM((N_HEADS * KEY_PAD, WINDOW), F32)],
        compiler_params=pltpu.CompilerParams(dimension_semantics=("arbitrary",),
                                             vmem_limit_bytes=VMEM_LIMIT),
        name="mixer_prompt",
    )(z, xbc, xbc, dtr, q, k, k, v, v, *consts)


N_SAMPLE_IN = 18


def _put_lane_blocks(scr, col0, rows, val):
    for j in range(val.shape[1] // LANES):
        scr[col0 // LANES + j, rows, :] = val[:, j * LANES:(j + 1) * LANES]


def _get_lane_blocks(scr, col0, width, rows):
    return jnp.concatenate([scr[col0 // LANES + j, rows, :] for j in range(width // LANES)], axis=1)


def _mixer_sample_kernel(*refs):
    (z_ref, xbc_ref, dt_ref, q_ref, k_ref, v_ref, sst_ref, sconv_ref, ck_ref, cv_ref,
     cw_ref, cb_ref, dtb_ref, alog_ref, dskip_ref, snorm_ref, sinks_ref, expand_ref) = refs[:N_SAMPLE_IN]
    (mix_ref, sst_out_ref, ck_out_ref, cv_out_ref,
     ypart_scr, ecum_scr, cdec_scr, xw_scr, xt_scr, bmat_scr, seq_scr, res_scr, kk_scr, vv_scr) = refs[-14:]
    dl, bb = xbc_ref.shape[0], xbc_ref.shape[1]
    hw = D_SSM // N_SSM_GROUPS
    nbc = N_SSM_GROUPS * D_STATE
    c_q, c_c, c_k, c_v = 0, D_ATTN, D_ATTN + nbc, D_ATTN + nbc + KV_WIDTH
    xw_scr[...] = jnp.zeros_like(xw_scr)
    bmat_scr[...] = jnp.zeros_like(bmat_scr)
    seq_scr[...] = jnp.zeros_like(seq_scr)
    kk_scr[...] = jnp.zeros_like(kk_scr)
    vv_scr[...] = jnp.zeros_like(vv_scr)

    cw = cw_ref[...]
    a_rep = -jnp.exp(alog_ref[...])
    xs, bm, cm, xdt, cum = [], [], [], [], []
    for t in range(dl):
        conv = cb_ref[...] + cw[CONV_W - 1:CONV_W] * xbc_ref[t]
        for back in range(1, CONV_W):
            src = xbc_ref[t - back] if t >= back else sconv_ref[CONV_W - 1 + t - back]
            conv = conv + cw[CONV_W - 1 - back:CONV_W - back] * src
        xc = _silu(conv)
        xs.append(xc[:, 0:D_SSM])
        bm.append(xc[:, D_SSM:D_SSM + nbc])
        cm.append(xc[:, D_SSM + nbc:CONV_DIM])
        dt_c = _softplus(dt_ref[t] + dtb_ref[...])
        dt = _mm(jnp.concatenate(_split3(dt_c), axis=1), expand_ref[...])
        cum.append(dt * a_rep if t == 0 else cum[-1] + dt * a_rep)
        xdt.append(xs[t] * dt)
    cdec_scr[...] = jnp.exp(cum[dl - 1])
    for t in range(dl):
        y = dskip_ref[...] * xs[t]
        for s in range(t + 1):
            prod = cm[t] * bm[s]
            dots = jnp.concatenate(
                [jnp.broadcast_to(jnp.sum(prod[:, g * D_STATE:(g + 1) * D_STATE], axis=-1, keepdims=True), (bb, hw))
                 for g in range(N_SSM_GROUPS)], axis=1)
            y = y + (dots * xdt[s] if s == t else dots * jnp.exp(cum[t] - cum[s]) * xdt[s])
        rows_t = slice(t * bb, (t + 1) * bb)
        ypart_scr[rows_t, :] = y
        ecum_scr[rows_t, :] = jnp.exp(cum[t])
        xw_scr[rows_t, :] = xdt[t] * jnp.exp(cum[dl - 1] - cum[t])
        bmat_scr[rows_t, :] = bm[t]
        own_rows = pl.ds(t, bb, stride=SEQ_ROWS)
        _put_lane_blocks(seq_scr, c_q, own_rows, q_ref[t] * (HEAD_DIM ** -0.5))
        _put_lane_blocks(seq_scr, c_c, own_rows, cm[t])
        _put_lane_blocks(seq_scr, c_k, own_rows, k_ref[t])
        _put_lane_blocks(seq_scr, c_v, own_rows, v_ref[t])
    xw = xw_scr[...]
    for j in range(D_SSM // LANES):
        xt_scr[j * LANES:(j + 1) * LANES, :] = xw[:, j * LANES:(j + 1) * LANES].T.astype(BF16)

    block_row = lax.broadcasted_iota(jnp.int32, (SSD_CHUNK, D_STATE), 0)

    def state_matmuls(b):
        my_rows = pl.ds(pl.multiple_of(b * SEQ_ROWS, SEQ_ROWS), SEQ_ROWS)
        state = sst_ref[b]
        state_b = state.astype(BF16)
        c_rows = jnp.concatenate([_get_lane_blocks(seq_scr, c_c, nbc, my_rows),
                                  jnp.zeros((SEQ_ROWS, nbc), F32)], axis=0).astype(BF16)
        y_off = [_mm_nt(c_rows[:, g * D_STATE:(g + 1) * D_STATE], state_b[g * hw:(g + 1) * hw, :])[0:SEQ_ROWS]
                 for g in range(N_SSM_GROUPS)]
        is_mine = (block_row & (bb - 1)) == b
        upd = [_mm(xt_scr[g * hw:(g + 1) * hw, :],
                   jnp.where(is_mine, bmat_scr[:, g * D_STATE:(g + 1) * D_STATE], 0.0).astype(BF16))
               for g in range(N_SSM_GROUPS)]
        return my_rows, state, y_off, upd

    def attention_scores(b, slot, my_rows):
        kk_scr[slot, 0:WINDOW, :] = ck_ref[b]
        kk_scr[slot, WINDOW:WINDOW + SEQ_ROWS, :] = _get_lane_blocks(seq_scr, c_k, KV_WIDTH, my_rows)
        vv_scr[slot, 0:WINDOW, :] = cv_ref[b]
        vv_scr[slot, WINDOW:WINDOW + SEQ_ROWS, :] = _get_lane_blocks(seq_scr, c_v, KV_WIDTH, my_rows)
        ck_out_ref[b] = kk_scr[slot, dl:dl + WINDOW, :]
        cv_out_ref[b] = vv_scr[slot, dl:dl + WINDOW, :]
        return _attention_rows_scores(_get_lane_blocks(seq_scr, c_q, D_ATTN, my_rows), kk_scr[slot], vv_scr[slot])

    def store_state(b, my_rows, state, y_off, upd):
        _put_lane_blocks(res_scr, 0, my_rows, jnp.concatenate(y_off, axis=1))
        decay = jnp.broadcast_to(cdec_scr[pl.ds(b, 1), :], (SUBLANES, D_SSM))
        for j in range(D_SSM // LANES):
            g, jj = divmod(j, hw // LANES)
            for half, dec in enumerate(_split_pair(decay[:, j * LANES:(j + 1) * LANES])):
                h0 = j * LANES + half * HEAD_DIM
                u0 = jj * LANES + half * HEAD_DIM
                sst_out_ref[b, h0:h0 + HEAD_DIM, :] = (
                    state[h0:h0 + HEAD_DIM] * jnp.broadcast_to(dec[0:1], (HEAD_DIM, D_STATE))
                    + upd[g][u0:u0 + HEAD_DIM])

    def per_group(i, carry):
        seqs = [i * SAMPLE_UNROLL + slot for slot in range(SAMPLE_UNROLL)]
        parts = [state_matmuls(b) for b in seqs]
        scored = [attention_scores(b, slot, parts[slot][0]) for slot, b in enumerate(seqs)]
        for slot, b in enumerate(seqs):
            store_state(b, *parts[slot])
        weights = [_attention_rows_probs(scores, sinks_ref) for scores, _ in scored]
        for slot in range(SAMPLE_UNROLL):
            att = _attention_rows_output(*weights[slot], scored[slot][1])
            _put_lane_blocks(res_scr, D_SSM, parts[slot][0], att)
        return carry

    lax.fori_loop(0, bb // SAMPLE_UNROLL, per_group, 0)

    for t in range(dl):
        own_rows = pl.ds(t, bb, stride=SEQ_ROWS)
        rows_t = slice(t * bb, (t + 1) * bb)
        y = ypart_scr[rows_t, :] + _get_lane_blocks(res_scr, 0, D_SSM, own_rows) * ecum_scr[rows_t, :]
        mix_ref[t, :, 0:D_SSM] = _gated_group_norm(y, z_ref[t], snorm_ref[...])
        mix_ref[t, :, D_SSM:D_SSM + D_ATTN] = _get_lane_blocks(res_scr, D_SSM, D_ATTN, own_rows)


def _mixer_sample(z, xbc, dtr, q, k, v, state_ssm, state_conv, cache_k, cache_v, pw, layer, carried):
    depth = state_ssm.shape[0]
    dl, nb = xbc.shape[0], xbc.shape[1]
    bb = min(SAMPLE_SEQS_PER_STEP, nb)
    assert bb & (bb - 1) == 0 and dl * bb <= SSD_CHUNK and dl <= SEQ_ROWS and bb % SAMPLE_UNROLL == 0
    tok = lambda width: pl.BlockSpec((dl, bb, width), lambda i: (0, i, 0))
    lblk = lambda *dims: pl.BlockSpec((None, bb) + dims, lambda i: (layer, i) + (0,) * len(dims))
    consts = _mixer_consts(pw)
    operands = (z, xbc, dtr, q, k, v, state_ssm, state_conv, cache_k, cache_v) + consts
    assert len(operands) == N_SAMPLE_IN
    seq_cols = D_ATTN + N_SSM_GROUPS * D_STATE + 2 * KV_WIDTH
    return pl.pallas_call(
        _mixer_sample_kernel,
        grid=(nb // bb,),
        in_specs=[tok(D_SSM), tok(CONV_DIM), tok(LANES), tok(D_ATTN), tok(KV_WIDTH), tok(KV_WIDTH),
                  lblk(D_SSM, D_STATE),
                  pl.BlockSpec((None, CONV_W - 1, bb, CONV_DIM), lambda i: (layer, 0, i, 0)),
                  lblk(WINDOW, KV_WIDTH), lblk(WINDOW, KV_WIDTH)]
                 + [_layer_block(cst, layer) for cst in consts]
                 + [pl.BlockSpec(memory_space=pl.ANY)] * len(carried),
        out_specs=[tok(D_SSM + D_ATTN), lblk(D_SSM, D_STATE), lblk(WINDOW, KV_WIDTH), lblk(WINDOW, KV_WIDTH)],
        out_shape=[jax.ShapeDtypeStruct((dl, nb, D_SSM + D_ATTN), F32),
                   jax.ShapeDtypeStruct((depth, nb, D_SSM, D_STATE), F32),
                   jax.ShapeDtypeStruct((depth, nb, WINDOW, KV_WIDTH), F32),
                   jax.ShapeDtypeStruct((depth, nb, WINDOW, KV_WIDTH), F32)],
        input_output_aliases={N_SAMPLE_IN + n: 1 + n for n in range(len(carried))},
        scratch_shapes=[pltpu.VMEM((dl * bb, D_SSM), F32),
                        pltpu.VMEM((dl * bb, D_SSM), F32),
                        pltpu.VMEM((bb, D_SSM), F32),
                        pltpu.VMEM((SSD_CHUNK, D_SSM), F32),
                        pltpu.VMEM((D_SSM, SSD_CHUNK), BF16),
                        pltpu.VMEM((SSD_CHUNK, N_SSM_GROUPS * D_STATE), F32),
                        pltpu.VMEM((seq_cols // LANES, bb * SEQ_ROWS, LANES), F32),
                        pltpu.VMEM(((D_SSM + D_ATTN) // LANES, bb * SEQ_ROWS, LANES), F32),
                        pltpu.VMEM((SAMPLE_UNROLL, KEY_PAD, KV_WIDTH), F32),
                        pltpu.VMEM((SAMPLE_UNROLL, KEY_PAD, KV_WIDTH), F32)],
        compiler_params=pltpu.CompilerParams(dimension_semantics=("arbitrary",),
                                             vmem_limit_bytes=VMEM_LIMIT),
        name="mixer_sample",
    )(*operands, *carried)


def _prepare_weights(p):
    bf = lambda w: w.astype(BF16)
    vec = lambda v: v[:, None, :]

    rep = lambda v: jnp.repeat(v, HEAD_DIM, axis=-1)[:, None, :]
    lane_pad = lambda v: jnp.pad(v, ((0, 0), (0, LANES - v.shape[-1])))[:, None, :]
    cuts = np.cumsum([D_SSM, CONV_DIM, N_SSM_HEADS, D_ATTN, KV_WIDTH])
    wz, wxbc, wdt, wq, wk, wv = jnp.split(bf(p['w_in']), cuts, axis=2)
    depth = p['w_in'].shape[0]
    seg = np.kron(np.eye(D_ATTN // HEAD_DIM, dtype=np.float32), np.ones((HEAD_DIM, HEAD_DIM), np.float32))
    heads = np.eye(LANES, dtype=np.float32)[:, :N_SSM_HEADS]
    expand = np.tile(np.kron(heads, np.ones((1, HEAD_DIM), np.float32)), (3, 1))
    return {
        'g_ffn1': vec(p['g_ffn1']), 'g_mix': vec(p['g_mix']), 'g_ffn2': vec(p['g_ffn2']), 'g_ple': vec(p['g_ple']),
        'w1_a': bf(p['w1_a']), 'w3_a': bf(p['w3_a']), 'w2_a': bf(p['w2_a']),
        'w1_b': bf(p['w1_b']), 'w3_b': bf(p['w3_b']), 'w2_b': bf(p['w2_b']),
        'w_in': jnp.concatenate(
            [wz, wxbc, wq, wk, wv, jnp.pad(wdt, ((0, 0), (0, 0), (0, LANES - N_SSM_HEADS)))], axis=2),
        'w_out': bf(p['w_out']), 'w_ple_gate': bf(p['w_ple_gate']), 'w_ple_proj': bf(p['w_ple_proj']),
        'q_gain': vec(jnp.tile(p['q_norm'], (1, N_HEADS))),
        'k_gain': vec(jnp.tile(p['k_norm'], (1, N_KV_HEADS))),
        'seg': jnp.broadcast_to(jnp.asarray(seg, BF16), (depth,) + seg.shape),
        'conv_w': p['conv_w'], 'conv_b': vec(p['conv_b']),
        'dt_bias': lane_pad(p['dt_bias']), 'a_log_rep': rep(p['a_log']),
        'd_skip': rep(p['d_skip']),
        'head_expand': jnp.broadcast_to(jnp.asarray(expand, BF16), (depth,) + expand.shape),
        'ssm_norm': vec(p['ssm_norm']),
        'sinks': jnp.broadcast_to(p['sinks'][:, :, None], (depth, N_HEADS, LANES)),
    }


def kernel(x_prompt, x_sample, state_ssm, state_conv, cache_k_win, cache_v_win, p_prompt, p_sample, g_ffn1, w1_a, w3_a, w2_a, g_mix, w_in, conv_w, conv_b, dt_bias, a_log, d_skip, ssm_norm, q_norm, k_norm, sinks, w_out, g_ffn2, w1_b, w3_b, w2_b, g_ple, w_ple_gate, w_ple_proj):
    params = dict(g_ffn1=g_ffn1, w1_a=w1_a, w3_a=w3_a, w2_a=w2_a, g_mix=g_mix, w_in=w_in, conv_w=conv_w,
                  conv_b=conv_b, dt_bias=dt_bias, a_log=a_log, d_skip=d_skip, ssm_norm=ssm_norm, q_norm=q_norm,
                  k_norm=k_norm, sinks=sinks, w_out=w_out, g_ffn2=g_ffn2, w1_b=w1_b, w3_b=w3_b, w2_b=w2_b,
                  g_ple=g_ple, w_ple_gate=w_ple_gate, w_ple_proj=w_ple_proj)
    depth = w_in.shape[0]
    bp, seq, _ = x_prompt.shape
    bs, dl, _ = x_sample.shape
    assert seq % MIXER_TILE == 0 and MIXER_TILE % SSD_CHUNK == 0 and seq >= WINDOW
    assert CONV_W - 1 <= dl <= SEQ_ROWS and bs % SAMPLE_SEQS_PER_STEP == 0
    assert (bp * seq) % TOKEN_TILE == 0 and (bp * seq) % STAGE_C_TILE == 0 and bs * dl <= TOKEN_TILE

    pw = _prepare_weights(params)
    sst_in = state_ssm.reshape(depth, bs, D_SSM, D_STATE)
    ck_in = cache_k_win.reshape(depth, bs, WINDOW, KV_WIDTH)
    cv_in = cache_v_win.reshape(depth, bs, WINDOW, KV_WIDTH)
    sconv_in = jnp.swapaxes(state_conv, 1, 2)
    pe_p = p_prompt.reshape(depth, bp * seq, D_PLE)
    pe_s = jnp.swapaxes(p_sample, 1, 2).reshape(depth, dl * bs, D_PLE)
    hp = x_prompt.reshape(bp * seq, D_MODEL)
    hs = jnp.swapaxes(x_sample, 0, 1).reshape(dl * bs, D_MODEL)
    ssm_p, conv_p, k_p, v_p, conv_s = [], [], [], [], []
    carried = ()
    for l in range(depth):
        hp, zp, xbcp, dtp, qp, kp, vp = _stage_a(hp, pw, l)
        hs, zs, xbcs, dts, qs, ks, vs = _stage_a(hs, pw, l)
        mixp, sp = _mixer_prompt(zp, xbcp, dtp, qp, kp, vp, pw, l, bp, seq)
        hp = _stage_c(hp, mixp, pe_p, pw, l)
        tm3 = lambda a: a.reshape(dl, bs, a.shape[-1])
        mixs, *carried = _mixer_sample(tm3(zs), tm3(xbcs), tm3(dts), tm3(qs), tm3(ks), tm3(vs),
                                       sst_in, sconv_in, ck_in, cv_in, pw, l, tuple(carried))
        hs = _stage_c(hs, mixs.reshape(dl * bs, D_MODEL), pe_s, pw, l)
        ssm_p.append(sp.reshape(bp, N_SSM_HEADS, HEAD_DIM, D_STATE))
        conv_p.append(xbcp.reshape(bp, seq, CONV_DIM)[:, seq - (CONV_W - 1):])
        last_window = lambda a: a.reshape(bp, seq, KV_WIDTH)[:, seq - WINDOW:].reshape(
            bp, WINDOW, N_KV_HEADS, HEAD_DIM)
        k_p.append(last_window(kp))
        v_p.append(last_window(vp))
        conv_s.append(jnp.swapaxes(tm3(xbcs)[dl - (CONV_W - 1):], 0, 1))
    ssm_s, k_s, v_s = carried
    return (hp.reshape(bp, seq, D_MODEL), jnp.swapaxes(hs.reshape(dl, bs, D_MODEL), 0, 1),
            jnp.stack(ssm_p), jnp.stack(conv_p), jnp.stack(k_p), jnp.stack(v_p),
            ssm_s.reshape(depth, bs, N_SSM_HEADS, HEAD_DIM, D_STATE), jnp.stack(conv_s),
            k_s.reshape(depth, bs, WINDOW, N_KV_HEADS, HEAD_DIM),
            v_s.reshape(depth, bs, WINDOW, N_KV_HEADS, HEAD_DIM))
```

```python
import functools

import numpy as np
import jax
import jax.numpy as jnp
from jax import lax
from jax.experimental import pallas as pl
from jax.experimental.pallas import tpu as pltpu

F32 = jnp.float32
BF16 = jnp.bfloat16

D_MODEL = 1024
HEAD_DIM = 64
D_SSM = 512
N_SSM_HEADS = 8
N_SSM_GROUPS = 2
D_STATE = 128
CONV_W = 4
CONV_DIM = D_SSM + 2 * N_SSM_GROUPS * D_STATE
SSD_CHUNK = 128
D_ATTN = 512
N_HEADS = 8
N_KV_HEADS = 2
Q_PER_KV = N_HEADS // N_KV_HEADS
KV_WIDTH = N_KV_HEADS * HEAD_DIM
WINDOW = 128
D_FF = 2752
D_PLE = 256
RMS_EPS = 1e-6

LANES = 128
SUBLANES = 8
MXU_DIM = 256
FF_CHUNK = MXU_DIM
FF_FULL_CHUNKS = D_FF // FF_CHUNK
FF_TAIL = D_FF - FF_FULL_CHUNKS * FF_CHUNK
FF_TAIL_PAD = -(-FF_TAIL // MXU_DIM) * MXU_DIM
TOKEN_TILE = 512
STAGE_C_TILE = 1024
MIXER_TILE = 1024
SAMPLE_SEQS_PER_STEP = 16
SAMPLE_UNROLL = 8
SEQ_ROWS = SUBLANES
KEY_PAD = 2 * WINDOW
VMEM_LIMIT = 56 * 1024 * 1024

OFF_Z = 0
OFF_XBC = OFF_Z + D_SSM
OFF_Q = OFF_XBC + CONV_DIM
OFF_K = OFF_Q + D_ATTN
OFF_V = OFF_K + KV_WIDTH
OFF_DT = OFF_V + KV_WIDTH
D_PROJ_PAD = OFF_DT + LANES

ALIBI_SLOPES = tuple(float(s) for s in np.power(
    np.float32(2.0), -8.0 * np.arange(1, N_HEADS + 1, dtype=np.float32) / N_HEADS))

LOG2_E = 1.4426950408889634
_NT = (((1,), (1,)), ((), ()))


def _mm(a, b):
    return jnp.dot(a, b, preferred_element_type=F32)


def _mm_nt(a, b):
    return lax.dot_general(a, b, _NT, preferred_element_type=F32)


def _sigmoid(x):
    return 1.0 / (1.0 + jnp.exp(-x))


def _silu(x):
    return x * _sigmoid(x)


def _softplus(x):
    return jnp.maximum(x, 0.0) + jnp.log(1.0 + jnp.exp(-jnp.abs(x)))


def _rmsnorm(x, g):
    return x * lax.rsqrt(jnp.mean(x * x, axis=-1, keepdims=True) + RMS_EPS) * g


def _split3(x):
    hi = x.astype(BF16)
    rest = x - hi.astype(F32)
    mid = rest.astype(BF16)
    return hi, mid, (rest - mid.astype(F32)).astype(BF16)


def _stage_ff_tail(w1_ref, w3_ref, w2_ref, tail_refs):
    w1t_ref, w3t_ref, w2t_ref = tail_refs
    lo = FF_FULL_CHUNKS * FF_CHUNK
    for dst, src in ((w1t_ref, w1_ref), (w3t_ref, w3_ref)):
        dst[...] = jnp.zeros_like(dst)
        dst[:, 0:FF_TAIL] = src[:, lo:D_FF]
    w2t_ref[...] = jnp.zeros_like(w2t_ref)
    w2t_ref[0:FF_TAIL, :] = w2_ref[lo:D_FF, :]


def _ff_tail_scratch():
    return [pltpu.VMEM((D_MODEL, FF_TAIL_PAD), BF16), pltpu.VMEM((D_MODEL, FF_TAIL_PAD), BF16),
            pltpu.VMEM((FF_TAIL_PAD, D_MODEL), BF16)]


def _swiglu(xn, w1_ref, w3_ref, w2_ref, tail_refs):
    def chunk(acc, w1c, w3c, w2c):
        a = _mm(xn, w1c)
        b = _mm(xn, w3c)
        return acc + _mm((_silu(a) * b).astype(BF16), w2c)

    acc = jnp.zeros((xn.shape[0], D_MODEL), F32)
    for c in range(FF_FULL_CHUNKS):
        sl = slice(c * FF_CHUNK, (c + 1) * FF_CHUNK)
        acc = chunk(acc, w1_ref[:, sl], w3_ref[:, sl], w2_ref[sl, :])
    if FF_TAIL:
        acc = chunk(acc, *(r[...] for r in tail_refs))
    return acc


def _segment_sumsq(x, seg_ref):
    hi, mid, _ = _split3(x * x)
    seg = seg_ref[0:x.shape[1], 0:x.shape[1]]
    return _mm(hi, seg) + _mm(mid, seg)


def _stage_a_kernel(x_ref, g1_ref, w1_ref, w3_ref, w2_ref, gm_ref, win_ref, qg_ref, kg_ref, seg_ref,
                    h_ref, z_ref, xbc_ref, dt_ref, q_ref, k_ref, v_ref, *tail_refs):
    @pl.when(pl.program_id(0) == 0)
    def _():
        _stage_ff_tail(w1_ref, w3_ref, w2_ref, tail_refs)

    x = x_ref[...]
    h = x + 0.5 * _swiglu(_rmsnorm(x, g1_ref[...]).astype(BF16), w1_ref, w3_ref, w2_ref, tail_refs)
    h_ref[...] = h
    proj = _mm(_rmsnorm(h, gm_ref[...]).astype(BF16), win_ref[...])
    z_ref[...] = proj[:, OFF_Z:OFF_XBC]
    xbc_ref[...] = proj[:, OFF_XBC:OFF_Q]
    dt_ref[...] = proj[:, OFF_DT:D_PROJ_PAD]
    q = proj[:, OFF_Q:OFF_K]
    k = proj[:, OFF_K:OFF_V]
    q_ref[...] = q * lax.rsqrt(_segment_sumsq(q, seg_ref) * (1.0 / HEAD_DIM) + RMS_EPS) * qg_ref[...]
    k_ref[...] = k * lax.rsqrt(_segment_sumsq(k, seg_ref) * (1.0 / HEAD_DIM) + RMS_EPS) * kg_ref[...]
    v_ref[...] = proj[:, OFF_V:OFF_DT]


def _layer_block(arr, layer, **kwargs):
    return pl.BlockSpec((None,) + arr.shape[1:], lambda *_: (layer, 0, 0), **kwargs)


def _layer_resident(arr, layer):
    return _layer_block(arr, layer, pipeline_mode=pl.Buffered(1))


def _stage_a(x, pw, layer):
    n = x.shape[0]
    tm = min(TOKEN_TILE, n)
    row = lambda width: pl.BlockSpec((tm, width), lambda i: (i, 0))
    widths = (D_MODEL, D_SSM, CONV_DIM, LANES, D_ATTN, KV_WIDTH, KV_WIDTH)
    consts = (pw['g_ffn1'], pw['w1_a'], pw['w3_a'], pw['w2_a'], pw['g_mix'], pw['w_in'],
              pw['q_gain'], pw['k_gain'], pw['seg'])
    return pl.pallas_call(
        _stage_a_kernel,
        grid=(n // tm,),
        in_specs=[row(D_MODEL)] + [_layer_resident(c, layer) for c in consts],
        out_specs=[row(w) for w in widths],
        out_shape=[jax.ShapeDtypeStruct((n, w), F32) for w in widths],
        scratch_shapes=_ff_tail_scratch(),
        compiler_params=pltpu.CompilerParams(dimension_semantics=("arbitrary",),
                                             vmem_limit_bytes=VMEM_LIMIT),
        name="stage_a",
    )(x, *consts)


def _stage_c_kernel(h_ref, mix_ref, pe_ref, wout_ref, g2_ref, w1_ref, w3_ref, w2_ref, gp_ref,
                    wgate_ref, wproj_ref, o_ref, *tail_refs):
    @pl.when(pl.program_id(0) == 0)
    def _():
        _stage_ff_tail(w1_ref, w3_ref, w2_ref, tail_refs)

    o_ref[...] = _stage_c_compute(h_ref[...], mix_ref[...].astype(BF16), pe_ref, wout_ref, g2_ref, w1_ref,
                                  w3_ref, w2_ref, gp_ref, wgate_ref, wproj_ref, tail_refs)


def _stage_c_compute(h, mix, pe_ref, wout_ref, g2_ref, w1_ref, w3_ref, w2_ref, gp_ref, wgate_ref, wproj_ref,
                     tail_refs):
    h = h + _mm(mix, wout_ref[...])
    h = h + 0.5 * _swiglu(_rmsnorm(h, g2_ref[...]).astype(BF16), w1_ref, w3_ref, w2_ref, tail_refs)
    gate = _sigmoid(_mm(_rmsnorm(h, gp_ref[...]).astype(BF16), wgate_ref[...]))
    return h + gate * _mm(pe_ref[...].astype(BF16), wproj_ref[...])


def _stage_c(h, mix, pe, pw, layer):
    n = h.shape[0]
    tm = min(STAGE_C_TILE, n)
    row = lambda width: pl.BlockSpec((tm, width), lambda i: (i, 0))
    consts = (pw['w_out'], pw['g_ffn2'], pw['w1_b'], pw['w3_b'], pw['w2_b'], pw['g_ple'],
              pw['w_ple_gate'], pw['w_ple_proj'])
    return pl.pallas_call(
        _stage_c_kernel,
        grid=(n // tm,),
        in_specs=[row(D_MODEL), row(D_MODEL), pl.BlockSpec((None, tm, D_PLE), lambda i: (layer, i, 0))]
                 + [_layer_resident(c, layer) for c in consts],
        out_specs=row(D_MODEL),
        out_shape=jax.ShapeDtypeStruct((n, D_MODEL), F32),
        scratch_shapes=_ff_tail_scratch(),
        compiler_params=pltpu.CompilerParams(dimension_semantics=("arbitrary",),
                                             vmem_limit_bytes=VMEM_LIMIT),
        name="stage_c",
    )(h, mix, pe, *consts)


def _lane_low_half(shape):
    return lax.broadcasted_iota(jnp.int32, shape, len(shape) - 1) < HEAD_DIM


def _split_pair(x):
    rolled = pltpu.roll(x, HEAD_DIM, 1)
    low = _lane_low_half(x.shape)
    return jnp.where(low, x, rolled), jnp.where(low, rolled, x)


def _head_query_rows(q, g):
    low = _lane_low_half((q.shape[0], LANES))
    blocks = []
    for jj in range(Q_PER_KV // 2):
        qp = q[:, (g * 2 + jj) * LANES:(g * 2 + jj + 1) * LANES]
        blocks += [jnp.where(low, qp, 0.0), jnp.where(low, 0.0, qp)]
    return jnp.concatenate(blocks, axis=0).astype(BF16)


def _attention_rows_scores(q, kk, vv):
    low_k = _lane_low_half((KEY_PAD, LANES))
    k_dup = _split_pair(kk)
    v_roll = pltpu.roll(vv, HEAD_DIM, 1)
    v_even = (jnp.where(low_k, vv, 1.0).astype(BF16), jnp.where(low_k, v_roll, 1.0).astype(BF16))
    v_odd = (jnp.where(low_k, 1.0, v_roll).astype(BF16), jnp.where(low_k, 1.0, vv).astype(BF16))
    scores = [_mm_nt(_head_query_rows(q, g), k_dup[g].astype(BF16)) for g in range(N_KV_HEADS)]
    return scores, (v_even, v_odd)


def _attention_rows_probs(scores, sinks_ref):
    r = scores[0].shape[0] // Q_PER_KV
    t_i = lax.broadcasted_iota(jnp.int32, (r, KEY_PAD), 0)
    s_i = lax.broadcasted_iota(jnp.int32, (r, KEY_PAD), 1)
    rel = t_i + WINDOW - s_i
    valid = (rel >= 0) & (rel < WINDOW)
    relf = rel.astype(F32)
    probs, sink_terms = [], []
    for head in range(N_HEADS):
        g, hh = divmod(head, Q_PER_KV)
        sh = jnp.where(valid, scores[g][hh * r:(hh + 1) * r] - ALIBI_SLOPES[head] * relf, -jnp.inf)
        sink = sinks_ref[head:head + 1, 0:1]
        m = jnp.maximum(jnp.max(sh, axis=-1, keepdims=True), sink)
        probs.append(jnp.exp(sh - m))
        sink_terms.append(jnp.broadcast_to(jnp.exp(sink - m), (r, LANES)))
    return probs, sink_terms


def _attention_rows_output(probs, sink_terms, v_operands):
    v_even, v_odd = v_operands
    r = probs[0].shape[0]
    low = _lane_low_half((r, LANES))
    outs = []
    for g in range(N_KV_HEADS):
        mine = probs[g * Q_PER_KV:(g + 1) * Q_PER_KV]
        o_even = _mm(jnp.concatenate(mine[0::2], axis=0).astype(BF16), v_even[g])
        o_odd = _mm(jnp.concatenate(mine[1::2], axis=0).astype(BF16), v_odd[g])
        for jj in range(Q_PER_KV // 2):
            oe, oo = o_even[jj * r:(jj + 1) * r], o_odd[jj * r:(jj + 1) * r]
            head = g * Q_PER_KV + 2 * jj
            den = (pltpu.roll(jnp.where(low, oo, oe), HEAD_DIM, 1)
                   + jnp.where(low, sink_terms[head], sink_terms[head + 1]))
            outs.append(jnp.where(low, oe, oo) / den)
    return jnp.concatenate(outs, axis=1)


def _fill_attention_bias(bias_scr):
    s_i = lax.broadcasted_iota(jnp.int32, (KEY_PAD, WINDOW), 0)
    t_i = lax.broadcasted_iota(jnp.int32, (KEY_PAD, WINDOW), 1)
    rel = t_i + WINDOW - s_i
    valid = (rel >= 0) & (rel < WINDOW)
    relf = rel.astype(F32)
    for head in range(N_HEADS):
        bias_scr[head * KEY_PAD:(head + 1) * KEY_PAD, :] = jnp.where(
            valid, (-ALIBI_SLOPES[head] * LOG2_E) * relf, -jnp.inf)


def _attention_cols_scores(q, kk):
    k_dup = _split_pair(kk)
    qs = q * (HEAD_DIM ** -0.5 * LOG2_E)
    return [_mm_nt(k_dup[g].astype(BF16), _head_query_rows(qs, g)) for g in range(N_KV_HEADS)]


def _attention_cols_probs(scores, bias_scr, sinks_ref, first_block):
    r = scores[0].shape[1] // Q_PER_KV
    mask_prev = None if first_block is None else jnp.where(first_block, -jnp.inf, 0.0)
    out = []
    for g in range(N_KV_HEADS):
        probs, sink_terms = [], []
        for hh in range(Q_PER_KV):
            head = g * Q_PER_KV + hh
            sh = scores[g][:, hh * r:(hh + 1) * r] + bias_scr[head * KEY_PAD:(head + 1) * KEY_PAD, :]
            prev = sh[0:WINDOW] if mask_prev is None else sh[0:WINDOW] + mask_prev
            cur = sh[WINDOW:KEY_PAD]
            sink = sinks_ref[head:head + 1, :] * LOG2_E
            m = jnp.maximum(jnp.maximum(jnp.max(prev, axis=0, keepdims=True),
                                        jnp.max(cur, axis=0, keepdims=True)), sink)
            probs.append(jnp.concatenate([jnp.exp2(prev - m), jnp.exp2(cur - m)], axis=0))
            sink_terms.append(jnp.exp2(sink - m))
        out.append((jnp.concatenate(probs, axis=1).astype(BF16), jnp.concatenate(sink_terms, axis=1)))
    return out


def _attention_cols_pv(probs, vv):
    vt = vv.T
    ones = jnp.ones((HEAD_DIM, KEY_PAD), F32)
    return [_mm(jnp.concatenate([vt[g * HEAD_DIM:(g + 1) * HEAD_DIM], ones], axis=0).astype(BF16), probs[g][0])
            for g in range(N_KV_HEADS)]


def _attention_cols_output(pv, probs):
    r = pv[0].shape[1] // Q_PER_KV
    outs = []
    for g in range(N_KV_HEADS):
        den = pv[g][HEAD_DIM:HEAD_DIM + 1, :] + probs[g][1]
        on = pv[g][0:HEAD_DIM, :] * (1.0 / den)
        for jj in range(Q_PER_KV // 2):
            pair = jnp.concatenate([on[:, (2 * jj) * r:(2 * jj + 1) * r],
                                    on[:, (2 * jj + 1) * r:(2 * jj + 2) * r]], axis=0)
            outs.append(pair.T)
    return jnp.concatenate(outs, axis=1)


def _gated_group_norm(y, z, gain):
    y = y * _silu(z)
    gw = D_SSM // N_SSM_GROUPS
    parts = [_rmsnorm(y[:, g * gw:(g + 1) * gw], gain[:, g * gw:(g + 1) * gw]) for g in range(N_SSM_GROUPS)]
    return jnp.concatenate(parts, axis=1)


def _prompt_tile_mixer(first, z_ref, xbc_ref, xprev_ref, dt_ref, q_ref, k_ref, kprev_ref, v_ref, vprev_ref,
                       cw_ref, cb_ref, dtb_ref, alog_ref, dskip_ref, snorm_ref, sinks_ref, expand_ref,
                       st_scr, xe_scr, bias_scr, store_mix, state_after):
    t = SSD_CHUNK
    rows = xbc_ref.shape[0]
    hw = D_SSM // N_SSM_GROUPS
    nbc = N_SSM_GROUPS * D_STATE

    dt_raw = _mm(jnp.concatenate(_split3(dt_ref[...] + dtb_ref[...]), axis=1), expand_ref[...])

    xe_scr[0:SUBLANES, :] = jnp.where(first, 0.0, xprev_ref[...])
    xe_scr[SUBLANES:SUBLANES + t, :] = xbc_ref[0:t, :]
    cw = cw_ref[...]
    a_rep = -jnp.exp(alog_ref[...]) * LOG2_E
    causal = lax.broadcasted_iota(jnp.int32, (t, t), 0) >= lax.broadcasted_iota(jnp.int32, (t, t), 1)
    ltri = causal.astype(BF16)
    low_b = _lane_low_half((t, LANES))
    chunks = []
    for c in range(rows // t):
        sl = slice(c * t, (c + 1) * t)
        conv = cb_ref[...] + cw[CONV_W - 1:CONV_W] * xbc_ref[sl, :]
        for back in range(1, CONV_W):
            earlier = (xe_scr[SUBLANES - back:SUBLANES - back + t, :] if c == 0
                       else xbc_ref[c * t - back:(c + 1) * t - back, :])
            conv = conv + cw[CONV_W - 1 - back:CONV_W - back] * earlier
        xc = _silu(conv)
        dt = _softplus(dt_raw[sl])
        chunks.append(dict(
            sl=sl, dt=dt, xs=xc[:, 0:D_SSM], bm=xc[:, D_SSM:D_SSM + nbc], cm=xc[:, D_SSM + nbc:CONV_DIM],
            cum3=_mm(ltri, jnp.concatenate(_split3(dt * a_rep), axis=1))))
        yield

    for c, ch in enumerate(chunks):
        sl, xs, bm, cm = ch['sl'], ch['xs'], ch['bm'], ch['cm']
        cum3 = ch['cum3']
        cum = cum3[:, 0:D_SSM] + cum3[:, D_SSM:2 * D_SSM] + cum3[:, 2 * D_SSM:3 * D_SSM]
        xdt = xs * ch['dt']
        xw_b = (xdt * jnp.exp2(cum[t - 1:t, :] - cum)).astype(BF16)
        cg_b = [cm[:, g * D_STATE:(g + 1) * D_STATE].astype(BF16) for g in range(N_SSM_GROUPS)]
        bg = [bm[:, g * D_STATE:(g + 1) * D_STATE] for g in range(N_SSM_GROUPS)]
        k_before = kprev_ref[...] if c == 0 else k_ref[(c - 1) * t:c * t, :]
        v_before = vprev_ref[...] if c == 0 else v_ref[(c - 1) * t:c * t, :]
        ch.update(
            cum=cum, ecum=jnp.exp2(cum), xdt_b=xdt.astype(BF16), cg_b=cg_b,
            cb_t=[_mm_nt(cg_b[g], bg[g].astype(BF16)) for g in range(N_SSM_GROUPS)],
            st_new=[_mm(bg[g].T.astype(BF16), xw_b[:, g * hw:(g + 1) * hw]) for g in range(N_SSM_GROUPS)],
            att_scores=_attention_cols_scores(q_ref[sl, :], jnp.concatenate([k_before, k_ref[sl, :]], axis=0)),
            vv=jnp.concatenate([v_before, v_ref[sl, :]], axis=0))
        yield

    st = jnp.where(first, 0.0, st_scr[...])
    for c, ch in enumerate(chunks):
        y_in = []
        for j in range(D_SSM // LANES):
            cb_t = ch['cb_t'][j // (hw // LANES)]
            scores = []
            for col in _split_pair(ch['cum'][:, j * LANES:(j + 1) * LANES]):
                decay = jnp.exp2(jnp.where(causal, col - col.T, -jnp.inf))
                scores.append((cb_t * decay).astype(BF16))
            xp = ch['xdt_b'][:, j * LANES:(j + 1) * LANES]
            zero = jnp.zeros_like(xp)
            rhs = jnp.concatenate([jnp.where(low_b, xp, zero), jnp.where(low_b, zero, xp)], axis=0)
            y_in.append(_mm(jnp.concatenate(scores, axis=1), rhs))
        yield

        probs = _attention_cols_probs(ch['att_scores'], bias_scr, sinks_ref, first if c == 0 else None)
        pv = _attention_cols_pv(probs, ch['vv'])
        st_b = st.astype(BF16)
        y_off = [_mm(ch['cg_b'][g], st_b[:, g * hw:(g + 1) * hw]) for g in range(N_SSM_GROUPS)]
        yield

        y = (jnp.concatenate(y_in, axis=1) + jnp.concatenate(y_off, axis=1) * ch['ecum']
             + dskip_ref[...] * ch['xs'])
        st = st * ch['ecum'][t - 1:t, :] + jnp.concatenate(ch['st_new'], axis=1)
        ssd_out = _gated_group_norm(y, z_ref[ch['sl'], :], snorm_ref[...])
        yield

        store_mix(ch['sl'], ssd_out, _attention_cols_output(pv, probs))
        yield
    st_scr[...] = st
    state_after.append(st)


def _mixer_prompt_kernel(*refs, tiles_per_seq):
    mix_ref, state_ref, st_scr, xe_scr, bias_scr = refs[-5:]
    j = pl.program_id(0)

    @pl.when(j == 0)
    def _():
        _fill_attention_bias(bias_scr)

    def store_mix(rows, ssd_out, attn_out):
        mix_ref[rows, 0:D_SSM] = ssd_out.astype(mix_ref.dtype)
        mix_ref[rows, D_SSM:D_SSM + D_ATTN] = attn_out.astype(mix_ref.dtype)

    state_after = []
    for _ in _prompt_tile_mixer(lax.rem(j, tiles_per_seq) == 0, *refs[:-5], st_scr, xe_scr, bias_scr,
                                store_mix, state_after):
        pass
    st = state_after[0]

    @pl.when(lax.rem(j, tiles_per_seq) == tiles_per_seq - 1)
    def _():
        for jb in range(D_SSM // LANES):
            state_ref[jb * LANES:(jb + 1) * LANES, :] = st[:, jb * LANES:(jb + 1) * LANES].T


def _mixer_consts(pw):
    return (pw['conv_w'], pw['conv_b'], pw['dt_bias'], pw['a_log_rep'], pw['d_skip'], pw['ssm_norm'], pw['sinks'],
            pw['head_expand'])


def _mixer_prompt(z, xbc, dtr, q, k, v, pw, layer, batch, seq):
    t = SSD_CHUNK
    tile = MIXER_TILE
    tiles_per_seq = seq // tile
    cur = lambda width: pl.BlockSpec((tile, width), lambda i: (i, 0))
    before = lambda rows, width: pl.BlockSpec((rows, width), lambda i: (jnp.maximum(i * (tile // rows) - 1, 0), 0))
    consts = _mixer_consts(pw)
    return pl.pallas_call(
        functools.partial(_mixer_prompt_kernel, tiles_per_seq=tiles_per_seq),
        grid=(batch * tiles_per_seq,),
        in_specs=[cur(D_SSM), cur(CONV_DIM), before(SUBLANES, CONV_DIM), cur(LANES), cur(D_ATTN),
                  cur(KV_WIDTH), before(t, KV_WIDTH), cur(KV_WIDTH), before(t, KV_WIDTH)]
                 + [_layer_block(cst, layer) for cst in consts],
        out_specs=[pl.BlockSpec((tile, D_SSM + D_ATTN), lambda i: (i, 0)),
                   pl.BlockSpec((None, D_SSM, D_STATE), lambda i: (i // tiles_per_seq, 0, 0))],
        out_shape=[jax.ShapeDtypeStruct((batch * seq, D_SSM + D_ATTN), BF16),
                   jax.ShapeDtypeStruct((batch, D_SSM, D_STATE), F32)],
        scratch_shapes=[pltpu.VMEM((D_STATE, D_SSM), F32),
                        pltpu.VMEM((SUBLANES + t, CONV_DIM), F32),
                        pltpu.VMEM((N_HEADS * KEY_PAD, WINDOW), F32)],
        compiler_params=pltpu.CompilerParams(dimension_semantics=("arbitrary",),
                                             vmem_limit_bytes=VMEM_LIMIT),
        name="mixer_prompt",
    )(z, xbc, xbc, dtr, q, k, k, v, v, *consts)


N_SAMPLE_IN = 18


def _put_lane_blocks(scr, col0, rows, val):
    for j in range(val.shape[1] // LANES):
        scr[col0 // LANES + j, rows, :] = val[:, j * LANES:(j + 1) * LANES]


def _get_lane_blocks(scr, col0, width, rows):
    return jnp.concatenate([scr[col0 // LANES + j, rows, :] for j in range(width // LANES)], axis=1)


def _mixer_sample_kernel(*refs):
    (z_ref, xbc_ref, dt_ref, q_ref, k_ref, v_ref, sst_ref, sconv_ref, ck_ref, cv_ref,
     cw_ref, cb_ref, dtb_ref, alog_ref, dskip_ref, snorm_ref, sinks_ref, expand_ref) = refs[:N_SAMPLE_IN]
    (mix_ref, sst_out_ref, ck_out_ref, cv_out_ref,
     ypart_scr, ecum_scr, cdec_scr, xw_scr, xt_scr, bmat_scr, seq_scr, res_scr, kk_scr, vv_scr) = refs[-14:]
    dl, bb = xbc_ref.shape[0], xbc_ref.shape[1]
    hw = D_SSM // N_SSM_GROUPS
    nbc = N_SSM_GROUPS * D_STATE
    c_q, c_c, c_k, c_v = 0, D_ATTN, D_ATTN + nbc, D_ATTN + nbc + KV_WIDTH
    xw_scr[...] = jnp.zeros_like(xw_scr)
    bmat_scr[...] = jnp.zeros_like(bmat_scr)
    seq_scr[...] = jnp.zeros_like(seq_scr)
    kk_scr[...] = jnp.zeros_like(kk_scr)
    vv_scr[...] = jnp.zeros_like(vv_scr)

    cw = cw_ref[...]
    a_rep = -jnp.exp(alog_ref[...])
    xs, bm, cm, xdt, cum = [], [], [], [], []
    for t in range(dl):
        conv = cb_ref[...] + cw[CONV_W - 1:CONV_W] * xbc_ref[t]
        for back in range(1, CONV_W):
            src = xbc_ref[t - back] if t >= back else sconv_ref[CONV_W - 1 + t - back]
            conv = conv + cw[CONV_W - 1 - back:CONV_W - back] * src
        xc = _silu(conv)
        xs.append(xc[:, 0:D_SSM])
        bm.append(xc[:, D_SSM:D_SSM + nbc])
        cm.append(xc[:, D_SSM + nbc:CONV_DIM])
        dt_c = _softplus(dt_ref[t] + dtb_ref[...])
        dt = _mm(jnp.concatenate(_split3(dt_c), axis=1), expand_ref[...])
        cum.append(dt * a_rep if t == 0 else cum[-1] + dt * a_rep)
        xdt.append(xs[t] * dt)
    cdec_scr[...] = jnp.exp(cum[dl - 1])
    for t in range(dl):
        y = dskip_ref[...] * xs[t]
        for s in range(t + 1):
            prod = cm[t] * bm[s]
            dots = jnp.concatenate(
                [jnp.broadcast_to(jnp.sum(prod[:, g * D_STATE:(g + 1) * D_STATE], axis=-1, keepdims=True), (bb, hw))
                 for g in range(N_SSM_GROUPS)], axis=1)
            y = y + (dots * xdt[s] if s == t else dots * jnp.exp(cum[t] - cum[s]) * xdt[s])
        rows_t = slice(t * bb, (t + 1) * bb)
        ypart_scr[rows_t, :] = y
        ecum_scr[rows_t, :] = jnp.exp(cum[t])
        xw_scr[rows_t, :] = xdt[t] * jnp.exp(cum[dl - 1] - cum[t])
        bmat_scr[rows_t, :] = bm[t]
        own_rows = pl.ds(t, bb, stride=SEQ_ROWS)
        _put_lane_blocks(seq_scr, c_q, own_rows, q_ref[t] * (HEAD_DIM ** -0.5))
        _put_lane_blocks(seq_scr, c_c, own_rows, cm[t])
        _put_lane_blocks(seq_scr, c_k, own_rows, k_ref[t])
        _put_lane_blocks(seq_scr, c_v, own_rows, v_ref[t])
    xw = xw_scr[...]
    for j in range(D_SSM // LANES):
        xt_scr[j * LANES:(j + 1) * LANES, :] = xw[:, j * LANES:(j + 1) * LANES].T.astype(BF16)

    block_row = lax.broadcasted_iota(jnp.int32, (SSD_CHUNK, D_STATE), 0)

    def state_matmuls(b):
        my_rows = pl.ds(pl.multiple_of(b * SEQ_ROWS, SEQ_ROWS), SEQ_ROWS)
        state = sst_ref[b]
        state_b = state.astype(BF16)
        c_rows = jnp.concatenate([_get_lane_blocks(seq_scr, c_c, nbc, my_rows),
                                  jnp.zeros((SEQ_ROWS, nbc), F32)], axis=0).astype(BF16)
        y_off = [_mm_nt(c_rows[:, g * D_STATE:(g + 1) * D_STATE], state_b[g * hw:(g + 1) * hw, :])[0:SEQ_ROWS]
                 for g in range(N_SSM_GROUPS)]
        is_mine = (block_row & (bb - 1)) == b
        upd = [_mm(xt_scr[g * hw:(g + 1) * hw, :],
                   jnp.where(is_mine, bmat_scr[:, g * D_STATE:(g + 1) * D_STATE], 0.0).astype(BF16))
               for g in range(N_SSM_GROUPS)]
        return my_rows, state, y_off, upd

    def attention_scores(b, slot, my_rows):
        kk_scr[slot, 0:WINDOW, :] = ck_ref[b]
        kk_scr[slot, WINDOW:WINDOW + SEQ_ROWS, :] = _get_lane_blocks(seq_scr, c_k, KV_WIDTH, my_rows)
        vv_scr[slot, 0:WINDOW, :] = cv_ref[b]
        vv_scr[slot, WINDOW:WINDOW + SEQ_ROWS, :] = _get_lane_blocks(seq_scr, c_v, KV_WIDTH, my_rows)
        ck_out_ref[b] = kk_scr[slot, dl:dl + WINDOW, :]
        cv_out_ref[b] = vv_scr[slot, dl:dl + WINDOW, :]
        return _attention_rows_scores(_get_lane_blocks(seq_scr, c_q, D_ATTN, my_rows), kk_scr[slot], vv_scr[slot])

    def store_state(b, my_rows, state, y_off, upd):
        _put_lane_blocks(res_scr, 0, my_rows, jnp.concatenate(y_off, axis=1))
        decay = jnp.broadcast_to(cdec_scr[pl.ds(b, 1), :], (SUBLANES, D_SSM))
        for j in range(D_SSM // LANES):
            g, jj = divmod(j, hw // LANES)
            for half, dec in enumerate(_split_pair(decay[:, j * LANES:(j + 1) * LANES])):
                h0 = j * LANES + half * HEAD_DIM
                u0 = jj * LANES + half * HEAD_DIM
                sst_out_ref[b, h0:h0 + HEAD_DIM, :] = (
                    state[h0:h0 + HEAD_DIM] * jnp.broadcast_to(dec[0:1], (HEAD_DIM, D_STATE))
                    + upd[g][u0:u0 + HEAD_DIM])

    def per_group(i, carry):
        seqs = [i * SAMPLE_UNROLL + slot for slot in range(SAMPLE_UNROLL)]
        parts = [state_matmuls(b) for b in seqs]
        scored = [attention_scores(b, slot, parts[slot][0]) for slot, b in enumerate(seqs)]
        for slot, b in enumerate(seqs):
            store_state(b, *parts[slot])
        weights = [_attention_rows_probs(scores, sinks_ref) for scores, _ in scored]
        for slot in range(SAMPLE_UNROLL):
            att = _attention_rows_output(*weights[slot], scored[slot][1])
            _put_lane_blocks(res_scr, D_SSM, parts[slot][0], att)
        return carry

    lax.fori_loop(0, bb // SAMPLE_UNROLL, per_group, 0)

    for t in range(dl):
        own_rows = pl.ds(t, bb, stride=SEQ_ROWS)
        rows_t = slice(t * bb, (t + 1) * bb)
        y = ypart_scr[rows_t, :] + _get_lane_blocks(res_scr, 0, D_SSM, own_rows) * ecum_scr[rows_t, :]
        mix_ref[t, :, 0:D_SSM] = _gated_group_norm(y, z_ref[t], snorm_ref[...])
        mix_ref[t, :, D_SSM:D_SSM + D_ATTN] = _get_lane_blocks(res_scr, D_SSM, D_ATTN, own_rows)


def _mixer_sample(z, xbc, dtr, q, k, v, state_ssm, state_conv, cache_k, cache_v, pw, layer, carried):
    depth = state_ssm.shape[0]
    dl, nb = xbc.shape[0], xbc.shape[1]
    bb = min(SAMPLE_SEQS_PER_STEP, nb)
    assert bb & (bb - 1) == 0 and dl * bb <= SSD_CHUNK and dl <= SEQ_ROWS and bb % SAMPLE_UNROLL == 0
    tok = lambda width: pl.BlockSpec((dl, bb, width), lambda i: (0, i, 0))
    lblk = lambda *dims: pl.BlockSpec((None, bb) + dims, lambda i: (layer, i) + (0,) * len(dims))
    consts = _mixer_consts(pw)
    operands = (z, xbc, dtr, q, k, v, state_ssm, state_conv, cache_k, cache_v) + consts
    assert len(operands) == N_SAMPLE_IN
    seq_cols = D_ATTN + N_SSM_GROUPS * D_STATE + 2 * KV_WIDTH
    return pl.pallas_call(
        _mixer_sample_kernel,
        grid=(nb // bb,),
        in_specs=[tok(D_SSM), tok(CONV_DIM), tok(LANES), tok(D_ATTN), tok(KV_WIDTH), tok(KV_WIDTH),
                  lblk(D_SSM, D_STATE),
                  pl.BlockSpec((None, CONV_W - 1, bb, CONV_DIM), lambda i: (layer, 0, i, 0)),
                  lblk(WINDOW, KV_WIDTH), lblk(WINDOW, KV_WIDTH)]
                 + [_layer_block(cst, layer) for cst in consts]
                 + [pl.BlockSpec(memory_space=pl.ANY)] * len(carried),
        out_specs=[tok(D_SSM + D_ATTN), lblk(D_SSM, D_STATE), lblk(WINDOW, KV_WIDTH), lblk(WINDOW, KV_WIDTH)],
        out_shape=[jax.ShapeDtypeStruct((dl, nb, D_SSM + D_ATTN), F32),
                   jax.ShapeDtypeStruct((depth, nb, D_SSM, D_STATE), F32),
                   jax.ShapeDtypeStruct((depth, nb, WINDOW, KV_WIDTH), F32),
                   jax.ShapeDtypeStruct((depth, nb, WINDOW, KV_WIDTH), F32)],
        input_output_aliases={N_SAMPLE_IN + n: 1 + n for n in range(len(carried))},
        scratch_shapes=[pltpu.VMEM((dl * bb, D_SSM), F32),
                        pltpu.VMEM((dl * bb, D_SSM), F32),
                        pltpu.VMEM((bb, D_SSM), F32),
                        pltpu.VMEM((SSD_CHUNK, D_SSM), F32),
                        pltpu.VMEM((D_SSM, SSD_CHUNK), BF16),
                        pltpu.VMEM((SSD_CHUNK, N_SSM_GROUPS * D_STATE), F32),
                        pltpu.VMEM((seq_cols // LANES, bb * SEQ_ROWS, LANES), F32),
                        pltpu.VMEM(((D_SSM + D_ATTN) // LANES, bb * SEQ_ROWS, LANES), F32),
                        pltpu.VMEM((SAMPLE_UNROLL, KEY_PAD, KV_WIDTH), F32),
                        pltpu.VMEM((SAMPLE_UNROLL, KEY_PAD, KV_WIDTH), F32)],
        compiler_params=pltpu.CompilerParams(dimension_semantics=("arbitrary",),
                                             vmem_limit_bytes=VMEM_LIMIT),
        name="mixer_sample",
    )(*operands, *carried)


def _prepare_weights(p):
    bf = lambda w: w.astype(BF16)
    vec = lambda v: v[:, None, :]

    rep = lambda v: jnp.repeat(v, HEAD_DIM, axis=-1)[:, None, :]
    lane_pad = lambda v: jnp.pad(v, ((0, 0), (0, LANES - v.shape[-1])))[:, None, :]
    cuts = np.cumsum([D_SSM, CONV_DIM, N_SSM_HEADS, D_ATTN, KV_WIDTH])
    wz, wxbc, wdt, wq, wk, wv = jnp.split(bf(p['w_in']), cuts, axis=2)
    depth = p['w_in'].shape[0]
    seg = np.kron(np.eye(D_ATTN // HEAD_DIM, dtype=np.float32), np.ones((HEAD_DIM, HEAD_DIM), np.float32))
    heads = np.eye(LANES, dtype=np.float32)[:, :N_SSM_HEADS]
    expand = np.tile(np.kron(heads, np.ones((1, HEAD_DIM), np.float32)), (3, 1))
    return {
        'g_ffn1': vec(p['g_ffn1']), 'g_mix': vec(p['g_mix']), 'g_ffn2': vec(p['g_ffn2']), 'g_ple': vec(p['g_ple']),
        'w1_a': bf(p['w1_a']), 'w3_a': bf(p['w3_a']), 'w2_a': bf(p['w2_a']),
        'w1_b': bf(p['w1_b']), 'w3_b': bf(p['w3_b']), 'w2_b': bf(p['w2_b']),
        'w_in': jnp.concatenate(
            [wz, wxbc, wq, wk, wv, jnp.pad(wdt, ((0, 0), (0, 0), (0, LANES - N_SSM_HEADS)))], axis=2),
        'w_out': bf(p['w_out']), 'w_ple_gate': bf(p['w_ple_gate']), 'w_ple_proj': bf(p['w_ple_proj']),
        'q_gain': vec(jnp.tile(p['q_norm'], (1, N_HEADS))),
        'k_gain': vec(jnp.tile(p['k_norm'], (1, N_KV_HEADS))),
        'seg': jnp.broadcast_to(jnp.asarray(seg, BF16), (depth,) + seg.shape),
        'conv_w': p['conv_w'], 'conv_b': vec(p['conv_b']),
        'dt_bias': lane_pad(p['dt_bias']), 'a_log_rep': rep(p['a_log']),
        'd_skip': rep(p['d_skip']),
        'head_expand': jnp.broadcast_to(jnp.asarray(expand, BF16), (depth,) + expand.shape),
        'ssm_norm': vec(p['ssm_norm']),
        'sinks': jnp.broadcast_to(p['sinks'][:, :, None], (depth, N_HEADS, LANES)),
    }


def kernel(x_prompt, x_sample, state_ssm, state_conv, cache_k_win, cache_v_win, p_prompt, p_sample, g_ffn1, w1_a, w3_a, w2_a, g_mix, w_in, conv_w, conv_b, dt_bias, a_log, d_skip, ssm_norm, q_norm, k_norm, sinks, w_out, g_ffn2, w1_b, w3_b, w2_b, g_ple, w_ple_gate, w_ple_proj):
    params = dict(g_ffn1=g_ffn1, w1_a=w1_a, w3_a=w3_a, w2_a=w2_a, g_mix=g_mix, w_in=w_in, conv_w=conv_w,
                  conv_b=conv_b, dt_bias=dt_bias, a_log=a_log, d_skip=d_skip, ssm_norm=ssm_norm, q_norm=q_norm,
                  k_norm=k_norm, sinks=sinks, w_out=w_out, g_ffn2=g_ffn2, w1_b=w1_b, w3_b=w3_b, w2_b=w2_b,
                  g_ple=g_ple, w_ple_gate=w_ple_gate, w_ple_proj=w_ple_proj)
    depth = w_in.shape[0]
    bp, seq, _ = x_prompt.shape
    bs, dl, _ = x_sample.shape
    assert seq % MIXER_TILE == 0 and MIXER_TILE % SSD_CHUNK == 0 and seq >= WINDOW
    assert CONV_W - 1 <= dl <= SEQ_ROWS and bs % SAMPLE_SEQS_PER_STEP == 0
    assert (bp * seq) % TOKEN_TILE == 0 and (bp * seq) % STAGE_C_TILE == 0 and bs * dl <= TOKEN_TILE

    pw = _prepare_weights(params)
    sst_in = state_ssm.reshape(depth, bs, D_SSM, D_STATE)
    ck_in = cache_k_win.reshape(depth, bs, WINDOW, KV_WIDTH)
    cv_in = cache_v_win.reshape(depth, bs, WINDOW, KV_WIDTH)
    sconv_in = jnp.swapaxes(state_conv, 1, 2)
    pe_p = p_prompt.reshape(depth, bp * seq, D_PLE)
    pe_s = jnp.swapaxes(p_sample, 1, 2).reshape(depth, dl * bs, D_PLE)
    hp = x_prompt.reshape(bp * seq, D_MODEL)
    hs = jnp.swapaxes(x_sample, 0, 1).reshape(dl * bs, D_MODEL)
    ssm_p, conv_p, k_p, v_p, conv_s = [], [], [], [], []
    carried = ()
    for l in range(depth):
        hp, zp, xbcp, dtp, qp, kp, vp = _stage_a(hp, pw, l)
        hs, zs, xbcs, dts, qs, ks, vs = _stage_a(hs, pw, l)
        mixp, sp = _mixer_prompt(zp, xbcp, dtp, qp, kp, vp, pw, l, bp, seq)
        hp = _stage_c(hp, mixp, pe_p, pw, l)
        tm3 = lambda a: a.reshape(dl, bs, a.shape[-1])
        mixs, *carried = _mixer_sample(tm3(zs), tm3(xbcs), tm3(dts), tm3(qs), tm3(ks), tm3(vs),
                                       sst_in, sconv_in, ck_in, cv_in, pw, l, tuple(carried))
        hs = _stage_c(hs, mixs.reshape(dl * bs, D_MODEL), pe_s, pw, l)
        ssm_p.append(sp.reshape(bp, N_SSM_HEADS, HEAD_DIM, D_STATE))
        conv_p.append(xbcp.reshape(bp, seq, CONV_DIM)[:, seq - (CONV_W - 1):])
        last_window = lambda a: a.reshape(bp, seq, KV_WIDTH)[:, seq - WINDOW:].reshape(
            bp, WINDOW, N_KV_HEADS, HEAD_DIM)
        k_p.append(last_window(kp))
        v_p.append(last_window(vp))
        conv_s.append(jnp.swapaxes(tm3(xbcs)[dl - (CONV_W - 1):], 0, 1))
    ssm_s, k_s, v_s = carried
    return (hp.reshape(bp, seq, D_MODEL), jnp.swapaxes(hs.reshape(dl, bs, D_MODEL), 0, 1),
            jnp.stack(ssm_p), jnp.stack(conv_p), jnp.stack(k_p), jnp.stack(v_p),
            ssm_s.reshape(depth, bs, N_SSM_HEADS, HEAD_DIM, D_STATE), jnp.stack(conv_s),
            k_s.reshape(depth, bs, WINDOW, N_KV_HEADS, HEAD_DIM),
            v_s.reshape(depth, bs, WINDOW, N_KV_HEADS, HEAD_DIM))
```

```python
import functools

import numpy as np
import jax
import jax.numpy as jnp
from jax import lax
from jax.experimental import pallas as pl
from jax.experimental.pallas import tpu as pltpu

F32 = jnp.float32
BF16 = jnp.bfloat16

D_MODEL = 1024
HEAD_DIM = 64
D_SSM = 512
N_SSM_HEADS = 8
N_SSM_GROUPS = 2
D_STATE = 128
CONV_W = 4
CONV_DIM = D_SSM + 2 * N_SSM_GROUPS * D_STATE
SSD_CHUNK = 128
D_ATTN = 512
N_HEADS = 8
N_KV_HEADS = 2
Q_PER_KV = N_HEADS // N_KV_HEADS
KV_WIDTH = N_KV_HEADS * HEAD_DIM
WINDOW = 128
D_FF = 2752
D_PLE = 256
RMS_EPS = 1e-6

LANES = 128
SUBLANES = 8
MXU_DIM = 256
FF_CHUNK = MXU_DIM
FF_FULL_CHUNKS = D_FF // FF_CHUNK
FF_TAIL = D_FF - FF_FULL_CHUNKS * FF_CHUNK
FF_TAIL_PAD = -(-FF_TAIL // MXU_DIM) * MXU_DIM
TOKEN_TILE = 512
STAGE_C_TILE = 1024
MIXER_TILE = 1024
SAMPLE_SEQS_PER_STEP = 16
SAMPLE_UNROLL = 8
SEQ_ROWS = SUBLANES
KEY_PAD = 2 * WINDOW
VMEM_LIMIT = 56 * 1024 * 1024

OFF_Z = 0
OFF_XBC = OFF_Z + D_SSM
OFF_Q = OFF_XBC + CONV_DIM
OFF_K = OFF_Q + D_ATTN
OFF_V = OFF_K + KV_WIDTH
OFF_DT = OFF_V + KV_WIDTH
D_PROJ_PAD = OFF_DT + LANES

ALIBI_SLOPES = tuple(float(s) for s in np.power(
    np.float32(2.0), -8.0 * np.arange(1, N_HEADS + 1, dtype=np.float32) / N_HEADS))

LOG2_E = 1.4426950408889634
_NT = (((1,), (1,)), ((), ()))


def _mm(a, b):
    return jnp.dot(a, b, preferred_element_type=F32)


def _mm_nt(a, b):
    return lax.dot_general(a, b, _NT, preferred_element_type=F32)


def _sigmoid(x):
    return 1.0 / (1.0 + jnp.exp(-x))


def _silu(x):
    return x * _sigmoid(x)


def _softplus(x):
    return jnp.maximum(x, 0.0) + jnp.log(1.0 + jnp.exp(-jnp.abs(x)))


def _rmsnorm(x, g):
    return x * lax.rsqrt(jnp.mean(x * x, axis=-1, keepdims=True) + RMS_EPS) * g


def _split3(x):
    hi = x.astype(BF16)
    rest = x - hi.astype(F32)
    mid = rest.astype(BF16)
    return hi, mid, (rest - mid.astype(F32)).astype(BF16)


def _stage_ff_tail(w1_ref, w3_ref, w2_ref, tail_refs):
    w1t_ref, w3t_ref, w2t_ref = tail_refs
    lo = FF_FULL_CHUNKS * FF_CHUNK
    for dst, src in ((w1t_ref, w1_ref), (w3t_ref, w3_ref)):
        dst[...] = jnp.zeros_like(dst)
        dst[:, 0:FF_TAIL] = src[:, lo:D_FF]
    w2t_ref[...] = jnp.zeros_like(w2t_ref)
    w2t_ref[0:FF_TAIL, :] = w2_ref[lo:D_FF, :]


def _ff_tail_scratch():
    return [pltpu.VMEM((D_MODEL, FF_TAIL_PAD), BF16), pltpu.VMEM((D_MODEL, FF_TAIL_PAD), BF16),
            pltpu.VMEM((FF_TAIL_PAD, D_MODEL), BF16)]


def _swiglu(xn, w1_ref, w3_ref, w2_ref, tail_refs):
    def chunk(acc, w1c, w3c, w2c):
        a = _mm(xn, w1c)
        b = _mm(xn, w3c)
        return acc + _mm((_silu(a) * b).astype(BF16), w2c)

    acc = jnp.zeros((xn.shape[0], D_MODEL), F32)
    for c in range(FF_FULL_CHUNKS):
        sl = slice(c * FF_CHUNK, (c + 1) * FF_CHUNK)
        acc = chunk(acc, w1_ref[:, sl], w3_ref[:, sl], w2_ref[sl, :])
    if FF_TAIL:
        acc = chunk(acc, *(r[...] for r in tail_refs))
    return acc


def _segment_sumsq(x, seg_ref):
    hi, mid, _ = _split3(x * x)
    seg = seg_ref[0:x.shape[1], 0:x.shape[1]]
    return _mm(hi, seg) + _mm(mid, seg)


def _stage_a_kernel(x_ref, g1_ref, w1_ref, w3_ref, w2_ref, gm_ref, win_ref, qg_ref, kg_ref, seg_ref,
                    h_ref, z_ref, xbc_ref, dt_ref, q_ref, k_ref, v_ref, *tail_refs):
    @pl.when(pl.program_id(0) == 0)
    def _():
        _stage_ff_tail(w1_ref, w3_ref, w2_ref, tail_refs)

    x = x_ref[...]
    h = x + 0.5 * _swiglu(_rmsnorm(x, g1_ref[...]).astype(BF16), w1_ref, w3_ref, w2_ref, tail_refs)
    h_ref[...] = h
    proj = _mm(_rmsnorm(h, gm_ref[...]).astype(BF16), win_ref[...])
    z_ref[...] = proj[:, OFF_Z:OFF_XBC].astype(z_ref.dtype)
    xbc_ref[...] = proj[:, OFF_XBC:OFF_Q]
    dt_ref[...] = proj[:, OFF_DT:D_PROJ_PAD]
    q = proj[:, OFF_Q:OFF_K]
    k = proj[:, OFF_K:OFF_V]
    q_ref[...] = q * lax.rsqrt(_segment_sumsq(q, seg_ref) * (1.0 / HEAD_DIM) + RMS_EPS) * qg_ref[...]
    k_ref[...] = k * lax.rsqrt(_segment_sumsq(k, seg_ref) * (1.0 / HEAD_DIM) + RMS_EPS) * kg_ref[...]
    v_ref[...] = proj[:, OFF_V:OFF_DT]


def _layer_block(arr, layer, **kwargs):
    return pl.BlockSpec((None,) + arr.shape[1:], lambda *_: (layer, 0, 0), **kwargs)


def _layer_resident(arr, layer):
    return _layer_block(arr, layer, pipeline_mode=pl.Buffered(1))


def _stage_a(x, pw, layer):
    n = x.shape[0]
    tm = min(TOKEN_TILE, n)
    row = lambda width: pl.BlockSpec((tm, width), lambda i: (i, 0))
    widths = (D_MODEL, D_SSM, CONV_DIM, LANES, D_ATTN, KV_WIDTH, KV_WIDTH)
    consts = (pw['g_ffn1'], pw['w1_a'], pw['w3_a'], pw['w2_a'], pw['g_mix'], pw['w_in'],
              pw['q_gain'], pw['k_gain'], pw['seg'])
    return pl.pallas_call(
        _stage_a_kernel,
        grid=(n // tm,),
        in_specs=[row(D_MODEL)] + [_layer_resident(c, layer) for c in consts],
        out_specs=[row(w) for w in widths],
        out_shape=[jax.ShapeDtypeStruct((n, w), BF16 if i == 1 else F32) for i, w in enumerate(widths)],
        scratch_shapes=_ff_tail_scratch(),
        compiler_params=pltpu.CompilerParams(dimension_semantics=("arbitrary",),
                                             vmem_limit_bytes=VMEM_LIMIT),
        name="stage_a",
    )(x, *consts)


def _stage_c_kernel(h_ref, mix_ref, pe_ref, wout_ref, g2_ref, w1_ref, w3_ref, w2_ref, gp_ref,
                    wgate_ref, wproj_ref, o_ref, *tail_refs):
    @pl.when(pl.program_id(0) == 0)
    def _():
        _stage_ff_tail(w1_ref, w3_ref, w2_ref, tail_refs)

    o_ref[...] = _stage_c_compute(h_ref[...], mix_ref[...].astype(BF16), pe_ref, wout_ref, g2_ref, w1_ref,
                                  w3_ref, w2_ref, gp_ref, wgate_ref, wproj_ref, tail_refs)


def _stage_c_compute(h, mix, pe_ref, wout_ref, g2_ref, w1_ref, w3_ref, w2_ref, gp_ref, wgate_ref, wproj_ref,
                     tail_refs):
    h = h + _mm(mix, wout_ref[...])
    h = h + 0.5 * _swiglu(_rmsnorm(h, g2_ref[...]).astype(BF16), w1_ref, w3_ref, w2_ref, tail_refs)
    gate = _sigmoid(_mm(_rmsnorm(h, gp_ref[...]).astype(BF16), wgate_ref[...]))
    return h + gate * _mm(pe_ref[...].astype(BF16), wproj_ref[...])


def _stage_c(h, mix, pe, pw, layer):
    n = h.shape[0]
    tm = min(STAGE_C_TILE, n)
    row = lambda width: pl.BlockSpec((tm, width), lambda i: (i, 0))
    consts = (pw['w_out'], pw['g_ffn2'], pw['w1_b'], pw['w3_b'], pw['w2_b'], pw['g_ple'],
              pw['w_ple_gate'], pw['w_ple_proj'])
    return pl.pallas_call(
        _stage_c_kernel,
        grid=(n // tm,),
        in_specs=[row(D_MODEL), row(D_MODEL), pl.BlockSpec((None, tm, D_PLE), lambda i: (layer, i, 0))]
                 + [_layer_resident(c, layer) for c in consts],
        out_specs=row(D_MODEL),
        out_shape=jax.ShapeDtypeStruct((n, D_MODEL), F32),
        scratch_shapes=_ff_tail_scratch(),
        compiler_params=pltpu.CompilerParams(dimension_semantics=("arbitrary",),
                                             vmem_limit_bytes=VMEM_LIMIT),
        name="stage_c",
    )(h, mix, pe, *consts)


def _lane_low_half(shape):
    return lax.broadcasted_iota(jnp.int32, shape, len(shape) - 1) < HEAD_DIM


def _split_pair(x):
    rolled = pltpu.roll(x, HEAD_DIM, 1)
    low = _lane_low_half(x.shape)
    return jnp.where(low, x, rolled), jnp.where(low, rolled, x)


def _head_query_rows(q, g):
    low = _lane_low_half((q.shape[0], LANES))
    blocks = []
    for jj in range(Q_PER_KV // 2):
        qp = q[:, (g * 2 + jj) * LANES:(g * 2 + jj + 1) * LANES]
        blocks += [jnp.where(low, qp, 0.0), jnp.where(low, 0.0, qp)]
    return jnp.concatenate(blocks, axis=0).astype(BF16)


def _attention_rows_scores(q, kk, vv):
    low_k = _lane_low_half((KEY_PAD, LANES))
    k_dup = _split_pair(kk)
    v_roll = pltpu.roll(vv, HEAD_DIM, 1)
    v_even = (jnp.where(low_k, vv, 1.0).astype(BF16), jnp.where(low_k, v_roll, 1.0).astype(BF16))
    v_odd = (jnp.where(low_k, 1.0, v_roll).astype(BF16), jnp.where(low_k, 1.0, vv).astype(BF16))
    scores = [_mm_nt(_head_query_rows(q, g), k_dup[g].astype(BF16)) for g in range(N_KV_HEADS)]
    return scores, (v_even, v_odd)


def _attention_rows_probs(scores, sinks_ref):
    r = scores[0].shape[0] // Q_PER_KV
    t_i = lax.broadcasted_iota(jnp.int32, (r, KEY_PAD), 0)
    s_i = lax.broadcasted_iota(jnp.int32, (r, KEY_PAD), 1)
    rel = t_i + WINDOW - s_i
    valid = (rel >= 0) & (rel < WINDOW)
    relf = rel.astype(F32)
    probs, sink_terms = [], []
    for head in range(N_HEADS):
        g, hh = divmod(head, Q_PER_KV)
        sh = jnp.where(valid, scores[g][hh * r:(hh + 1) * r] - ALIBI_SLOPES[head] * relf, -jnp.inf)
        sink = sinks_ref[head:head + 1, 0:1]
        m = jnp.maximum(jnp.max(sh, axis=-1, keepdims=True), sink)
        probs.append(jnp.exp(sh - m))
        sink_terms.append(jnp.broadcast_to(jnp.exp(sink - m), (r, LANES)))
    return probs, sink_terms


def _attention_rows_output(probs, sink_terms, v_operands):
    v_even, v_odd = v_operands
    r = probs[0].shape[0]
    low = _lane_low_half((r, LANES))
    outs = []
    for g in range(N_KV_HEADS):
        mine = probs[g * Q_PER_KV:(g + 1) * Q_PER_KV]
        o_even = _mm(jnp.concatenate(mine[0::2], axis=0).astype(BF16), v_even[g])
        o_odd = _mm(jnp.concatenate(mine[1::2], axis=0).astype(BF16), v_odd[g])
        for jj in range(Q_PER_KV // 2):
            oe, oo = o_even[jj * r:(jj + 1) * r], o_odd[jj * r:(jj + 1) * r]
            head = g * Q_PER_KV + 2 * jj
            den = (pltpu.roll(jnp.where(low, oo, oe), HEAD_DIM, 1)
                   + jnp.where(low, sink_terms[head], sink_terms[head + 1]))
            outs.append(jnp.where(low, oe, oo) / den)
    return jnp.concatenate(outs, axis=1)


def _fill_attention_bias(bias_scr):
    s_i = lax.broadcasted_iota(jnp.int32, (KEY_PAD, WINDOW), 0)
    t_i = lax.broadcasted_iota(jnp.int32, (KEY_PAD, WINDOW), 1)
    rel = t_i + WINDOW - s_i
    valid = (rel >= 0) & (rel < WINDOW)
    relf = rel.astype(F32)
    for head in range(N_HEADS):
        bias_scr[head * KEY_PAD:(head + 1) * KEY_PAD, :] = jnp.where(
            valid, (-ALIBI_SLOPES[head] * LOG2_E) * relf, -jnp.inf)


def _attention_cols_scores(q, kk):
    k_dup = _split_pair(kk)
    qs = q * (HEAD_DIM ** -0.5 * LOG2_E)
    return [_mm_nt(k_dup[g].astype(BF16), _head_query_rows(qs, g)) for g in range(N_KV_HEADS)]


def _attention_cols_probs(scores, bias_scr, sinks_ref, first_block):
    r = scores[0].shape[1] // Q_PER_KV
    mask_prev = None if first_block is None else jnp.where(first_block, -jnp.inf, 0.0)
    out = []
    for g in range(N_KV_HEADS):
        probs, sink_terms = [], []
        for hh in range(Q_PER_KV):
            head = g * Q_PER_KV + hh
            sh = scores[g][:, hh * r:(hh + 1) * r] + bias_scr[head * KEY_PAD:(head + 1) * KEY_PAD, :]
            prev = sh[0:WINDOW] if mask_prev is None else sh[0:WINDOW] + mask_prev
            cur = sh[WINDOW:KEY_PAD]
            sink = sinks_ref[head:head + 1, :] * LOG2_E
            m = jnp.maximum(jnp.maximum(jnp.max(prev, axis=0, keepdims=True),
                                        jnp.max(cur, axis=0, keepdims=True)), sink)
            probs.append(jnp.concatenate([jnp.exp2(prev - m), jnp.exp2(cur - m)], axis=0))
            sink_terms.append(jnp.exp2(sink - m))
        out.append((jnp.concatenate(probs, axis=1).astype(BF16), jnp.concatenate(sink_terms, axis=1)))
    return out


def _attention_cols_pv(probs, vv):
    vt = vv.T
    ones = jnp.ones((HEAD_DIM, KEY_PAD), F32)
    return [_mm(jnp.concatenate([vt[g * HEAD_DIM:(g + 1) * HEAD_DIM], ones], axis=0).astype(BF16), probs[g][0])
            for g in range(N_KV_HEADS)]


def _attention_cols_output(pv, probs):
    r = pv[0].shape[1] // Q_PER_KV
    outs = []
    for g in range(N_KV_HEADS):
        den = pv[g][HEAD_DIM:HEAD_DIM + 1, :] + probs[g][1]
        on = pv[g][0:HEAD_DIM, :] * (1.0 / den)
        for jj in range(Q_PER_KV // 2):
            pair = jnp.concatenate([on[:, (2 * jj) * r:(2 * jj + 1) * r],
                                    on[:, (2 * jj + 1) * r:(2 * jj + 2) * r]], axis=0)
            outs.append(pair.T)
    return jnp.concatenate(outs, axis=1)


def _gated_group_norm(y, z, gain):
    y = y * _silu(z.astype(F32))
    gw = D_SSM // N_SSM_GROUPS
    parts = [_rmsnorm(y[:, g * gw:(g + 1) * gw], gain[:, g * gw:(g + 1) * gw]) for g in range(N_SSM_GROUPS)]
    return jnp.concatenate(parts, axis=1)


def _prompt_tile_mixer(first, z_ref, xbc_ref, xprev_ref, dt_ref, q_ref, k_ref, kprev_ref, v_ref, vprev_ref,
                       cw_ref, cb_ref, dtb_ref, alog_ref, dskip_ref, snorm_ref, sinks_ref, expand_ref,
                       st_scr, xe_scr, bias_scr, store_mix, state_after):
    t = SSD_CHUNK
    rows = xbc_ref.shape[0]
    hw = D_SSM // N_SSM_GROUPS
    nbc = N_SSM_GROUPS * D_STATE

    dt_raw = _mm(jnp.concatenate(_split3(dt_ref[...] + dtb_ref[...]), axis=1), expand_ref[...])

    xe_scr[0:SUBLANES, :] = jnp.where(first, 0.0, xprev_ref[...])
    xe_scr[SUBLANES:SUBLANES + rows, :] = xbc_ref[...]
    cw = cw_ref[...]
    a_rep = -jnp.exp(alog_ref[...]) * LOG2_E
    causal = lax.broadcasted_iota(jnp.int32, (t, t), 0) >= lax.broadcasted_iota(jnp.int32, (t, t), 1)
    ltri = causal.astype(BF16)
    low_b = _lane_low_half((t, LANES))
    chunks = []
    for c in range(rows // t):
        sl = slice(c * t, (c + 1) * t)
        conv = cb_ref[...] + cw[CONV_W - 1:CONV_W] * xbc_ref[sl, :]
        for back in range(1, CONV_W):
            r0 = SUBLANES - back + c * t
            conv = conv + cw[CONV_W - 1 - back:CONV_W - back] * xe_scr[r0:r0 + t, :]
        xc = _silu(conv)
        dt = _softplus(dt_raw[sl])
        chunks.append(dict(
            sl=sl, dt=dt, xs=xc[:, 0:D_SSM], bm=xc[:, D_SSM:D_SSM + nbc], cm=xc[:, D_SSM + nbc:CONV_DIM],
            cum3=_mm(ltri, jnp.concatenate(_split3(dt * a_rep), axis=1))))
        yield

    for c, ch in enumerate(chunks):
        sl, xs, bm, cm = ch['sl'], ch['xs'], ch['bm'], ch['cm']
        cum3 = ch['cum3']
        cum = cum3[:, 0:D_SSM] + cum3[:, D_SSM:2 * D_SSM] + cum3[:, 2 * D_SSM:3 * D_SSM]
        xdt = xs * ch['dt']
        xw_b = (xdt * jnp.exp2(cum[t - 1:t, :] - cum)).astype(BF16)
        cg_b = [cm[:, g * D_STATE:(g + 1) * D_STATE].astype(BF16) for g in range(N_SSM_GROUPS)]
        bg = [bm[:, g * D_STATE:(g + 1) * D_STATE] for g in range(N_SSM_GROUPS)]
        k_before = kprev_ref[...] if c == 0 else k_ref[(c - 1) * t:c * t, :]
        v_before = vprev_ref[...] if c == 0 else v_ref[(c - 1) * t:c * t, :]
        ch.update(
            cum=cum, ecum=jnp.exp2(cum), xdt_b=xdt.astype(BF16), cg_b=cg_b,
            cb_t=[_mm_nt(cg_b[g], bg[g].astype(BF16)) for g in range(N_SSM_GROUPS)],
            st_new=[_mm(bg[g].T.astype(BF16), xw_b[:, g * hw:(g + 1) * hw]) for g in range(N_SSM_GROUPS)],
            att_scores=_attention_cols_scores(q_ref[sl, :], jnp.concatenate([k_before, k_ref[sl, :]], axis=0)),
            vv=jnp.concatenate([v_before, v_ref[sl, :]], axis=0))
        yield

    st = jnp.where(first, 0.0, st_scr[...])
    for c, ch in enumerate(chunks):
        y_in = []
        for j in range(D_SSM // LANES):
            cb_t = ch['cb_t'][j // (hw // LANES)]
            scores = []
            for col in _split_pair(ch['cum'][:, j * LANES:(j + 1) * LANES]):
                decay = jnp.exp2(jnp.where(causal, col - col.T, -jnp.inf))
                scores.append((cb_t * decay).astype(BF16))
            xp = ch['xdt_b'][:, j * LANES:(j + 1) * LANES]
            zero = jnp.zeros_like(xp)
            rhs = jnp.concatenate([jnp.where(low_b, xp, zero), jnp.where(low_b, zero, xp)], axis=0)
            y_in.append(_mm(jnp.concatenate(scores, axis=1), rhs))
        yield

        probs = _attention_cols_probs(ch['att_scores'], bias_scr, sinks_ref, first if c == 0 else None)
        pv = _attention_cols_pv(probs, ch['vv'])
        st_b = st.astype(BF16)
        y_off = [_mm(ch['cg_b'][g], st_b[:, g * hw:(g + 1) * hw]) for g in range(N_SSM_GROUPS)]
        yield

        y = (jnp.concatenate(y_in, axis=1) + jnp.concatenate(y_off, axis=1) * ch['ecum']
             + dskip_ref[...] * ch['xs'])
        st = st * ch['ecum'][t - 1:t, :] + jnp.concatenate(ch['st_new'], axis=1)
        ssd_out = _gated_group_norm(y, z_ref[ch['sl'], :], snorm_ref[...])
        yield

        store_mix(ch['sl'], ssd_out, _attention_cols_output(pv, probs))
        yield
    st_scr[...] = st
    state_after.append(st)


def _mixer_prompt_kernel(*refs, tiles_per_seq):
    mix_ref, state_ref, st_scr, xe_scr, bias_scr = refs[-5:]
    j = pl.program_id(0)

    @pl.when(j == 0)
    def _():
        _fill_attention_bias(bias_scr)

    def store_mix(rows, ssd_out, attn_out):
        mix_ref[rows, 0:D_SSM] = ssd_out.astype(mix_ref.dtype)
        mix_ref[rows, D_SSM:D_SSM + D_ATTN] = attn_out.astype(mix_ref.dtype)

    state_after = []
    for _ in _prompt_tile_mixer(lax.rem(j, tiles_per_seq) == 0, *refs[:-5], st_scr, xe_scr, bias_scr,
                                store_mix, state_after):
        pass
    st = state_after[0]

    @pl.when(lax.rem(j, tiles_per_seq) == tiles_per_seq - 1)
    def _():
        for jb in range(D_SSM // LANES):
            state_ref[jb * LANES:(jb + 1) * LANES, :] = st[:, jb * LANES:(jb + 1) * LANES].T


def _mixer_consts(pw):
    return (pw['conv_w'], pw['conv_b'], pw['dt_bias'], pw['a_log_rep'], pw['d_skip'], pw['ssm_norm'], pw['sinks'],
            pw['head_expand'])


def _mixer_prompt(z, xbc, dtr, q, k, v, pw, layer, batch, seq):
    t = SSD_CHUNK
    tile = MIXER_TILE
    tiles_per_seq = seq // tile
    cur = lambda width: pl.BlockSpec((tile, width), lambda i: (i, 0))
    before = lambda rows, width: pl.BlockSpec((rows, width), lambda i: (jnp.maximum(i * (tile // rows) - 1, 0), 0))
    consts = _mixer_consts(pw)
    return pl.pallas_call(
        functools.partial(_mixer_prompt_kernel, tiles_per_seq=tiles_per_seq),
        grid=(batch * tiles_per_seq,),
        in_specs=[cur(D_SSM), cur(CONV_DIM), before(SUBLANES, CONV_DIM), cur(LANES), cur(D_ATTN),
                  cur(KV_WIDTH), before(t, KV_WIDTH), cur(KV_WIDTH), before(t, KV_WIDTH)]
                 + [_layer_block(cst, layer) for cst in consts],
        out_specs=[pl.BlockSpec((tile, D_SSM + D_ATTN), lambda i: (i, 0)),
                   pl.BlockSpec((None, D_SSM, D_STATE), lambda i: (i // tiles_per_seq, 0, 0))],
        out_shape=[jax.ShapeDtypeStruct((batch * seq, D_SSM + D_ATTN), BF16),
                   jax.ShapeDtypeStruct((batch, D_SSM, D_STATE), F32)],
        scratch_shapes=[pltpu.VMEM((D_STATE, D_SSM), F32),
                        pltpu.VMEM((SUBLANES + tile, CONV_DIM), F32),
                        pltpu.VMEM((N_HEADS * KEY_PAD, WINDOW), F32)],
        compiler_params=pltpu.CompilerParams(dimension_semantics=("arbitrary",),
                                             vmem_limit_bytes=VMEM_LIMIT),
        name="mixer_prompt",
    )(z, xbc, xbc, dtr, q, k, k, v, v, *consts)


N_SAMPLE_IN = 18


def _put_lane_blocks(scr, col0, rows, val):
    for j in range(val.shape[1] // LANES):
        scr[col0 // LANES + j, rows, :] = val[:, j * LANES:(j + 1) * LANES]


def _get_lane_blocks(scr, col0, width, rows):
    return jnp.concatenate([scr[col0 // LANES + j, rows, :] for j in range(width // LANES)], axis=1)


def _mixer_sample_kernel(*refs):
    (z_ref, xbc_ref, dt_ref, q_ref, k_ref, v_ref, sst_ref, sconv_ref, ck_ref, cv_ref,
     cw_ref, cb_ref, dtb_ref, alog_ref, dskip_ref, snorm_ref, sinks_ref, expand_ref) = refs[:N_SAMPLE_IN]
    (mix_ref, sst_out_ref, ck_out_ref, cv_out_ref,
     ypart_scr, ecum_scr, cdec_scr, xw_scr, xt_scr, bmat_scr, seq_scr, res_scr, kk_scr, vv_scr) = refs[-14:]
    dl, bb = xbc_ref.shape[0], xbc_ref.shape[1]
    hw = D_SSM // N_SSM_GROUPS
    nbc = N_SSM_GROUPS * D_STATE
    c_q, c_c, c_k, c_v = 0, D_ATTN, D_ATTN + nbc, D_ATTN + nbc + KV_WIDTH
    xw_scr[...] = jnp.zeros_like(xw_scr)
    bmat_scr[...] = jnp.zeros_like(bmat_scr)
    seq_scr[...] = jnp.zeros_like(seq_scr)
    kk_scr[...] = jnp.zeros_like(kk_scr)
    vv_scr[...] = jnp.zeros_like(vv_scr)

    cw = cw_ref[...]
    a_rep = -jnp.exp(alog_ref[...])
    xs, bm, cm, xdt, cum = [], [], [], [], []
    for t in range(dl):
        conv = cb_ref[...] + cw[CONV_W - 1:CONV_W] * xbc_ref[t]
        for back in range(1, CONV_W):
            src = xbc_ref[t - back] if t >= back else sconv_ref[CONV_W - 1 + t - back]
            conv = conv + cw[CONV_W - 1 - back:CONV_W - back] * src
        xc = _silu(conv)
        xs.append(xc[:, 0:D_SSM])
        bm.append(xc[:, D_SSM:D_SSM + nbc])
        cm.append(xc[:, D_SSM + nbc:CONV_DIM])
        dt_c = _softplus(dt_ref[t] + dtb_ref[...])
        dt = _mm(jnp.concatenate(_split3(dt_c), axis=1), expand_ref[...])
        cum.append(dt * a_rep if t == 0 else cum[-1] + dt * a_rep)
        xdt.append(xs[t] * dt)
    cdec_scr[...] = jnp.exp(cum[dl - 1])
    for t in range(dl):
        y = dskip_ref[...] * xs[t]
        for s in range(t + 1):
            prod = cm[t] * bm[s]
            dots = jnp.concatenate(
                [jnp.broadcast_to(jnp.sum(prod[:, g * D_STATE:(g + 1) * D_STATE], axis=-1, keepdims=True), (bb, hw))
                 for g in range(N_SSM_GROUPS)], axis=1)
            y = y + (dots * xdt[s] if s == t else dots * jnp.exp(cum[t] - cum[s]) * xdt[s])
        rows_t = slice(t * bb, (t + 1) * bb)
        ypart_scr[rows_t, :] = y
        ecum_scr[rows_t, :] = jnp.exp(cum[t])
        xw_scr[rows_t, :] = xdt[t] * jnp.exp(cum[dl - 1] - cum[t])
        bmat_scr[rows_t, :] = bm[t]
        own_rows = pl.ds(t, bb, stride=SEQ_ROWS)
        _put_lane_blocks(seq_scr, c_q, own_rows, q_ref[t] * (HEAD_DIM ** -0.5))
        _put_lane_blocks(seq_scr, c_c, own_rows, cm[t])
        _put_lane_blocks(seq_scr, c_k, own_rows, k_ref[t])
        _put_lane_blocks(seq_scr, c_v, own_rows, v_ref[t])
    xw = xw_scr[...]
    for j in range(D_SSM // LANES):
        xt_scr[j * LANES:(j + 1) * LANES, :] = xw[:, j * LANES:(j + 1) * LANES].T.astype(BF16)

    block_row = lax.broadcasted_iota(jnp.int32, (SSD_CHUNK, D_STATE), 0)

    def state_matmuls(b):
        my_rows = pl.ds(pl.multiple_of(b * SEQ_ROWS, SEQ_ROWS), SEQ_ROWS)
        state = sst_ref[b]
        state_b = state.astype(BF16)
        c_rows = jnp.concatenate([_get_lane_blocks(seq_scr, c_c, nbc, my_rows),
                                  jnp.zeros((SEQ_ROWS, nbc), F32)], axis=0).astype(BF16)
        y_off = [_mm_nt(c_rows[:, g * D_STATE:(g + 1) * D_STATE], state_b[g * hw:(g + 1) * hw, :])[0:SEQ_ROWS]
                 for g in range(N_SSM_GROUPS)]
        is_mine = (block_row & (bb - 1)) == b
        upd = [_mm(xt_scr[g * hw:(g + 1) * hw, :],
                   jnp.where(is_mine, bmat_scr[:, g * D_STATE:(g + 1) * D_STATE], 0.0).astype(BF16))
               for g in range(N_SSM_GROUPS)]
        return my_rows, state, y_off, upd

    def attention_scores(b, slot, my_rows):
        kk_scr[slot, 0:WINDOW, :] = ck_ref[b]
        kk_scr[slot, WINDOW:WINDOW + SEQ_ROWS, :] = _get_lane_blocks(seq_scr, c_k, KV_WIDTH, my_rows)
        vv_scr[slot, 0:WINDOW, :] = cv_ref[b]
        vv_scr[slot, WINDOW:WINDOW + SEQ_ROWS, :] = _get_lane_blocks(seq_scr, c_v, KV_WIDTH, my_rows)
        ck_out_ref[b] = kk_scr[slot, dl:dl + WINDOW, :]
        cv_out_ref[b] = vv_scr[slot, dl:dl + WINDOW, :]
        return _attention_rows_scores(_get_lane_blocks(seq_scr, c_q, D_ATTN, my_rows), kk_scr[slot], vv_scr[slot])

    def store_state(b, my_rows, state, y_off, upd):
        _put_lane_blocks(res_scr, 0, my_rows, jnp.concatenate(y_off, axis=1))
        decay = jnp.broadcast_to(cdec_scr[pl.ds(b, 1), :], (SUBLANES, D_SSM))
        for j in range(D_SSM // LANES):
            g, jj = divmod(j, hw // LANES)
            for half, dec in enumerate(_split_pair(decay[:, j * LANES:(j + 1) * LANES])):
                h0 = j * LANES + half * HEAD_DIM
                u0 = jj * LANES + half * HEAD_DIM
                sst_out_ref[b, h0:h0 + HEAD_DIM, :] = (
                    state[h0:h0 + HEAD_DIM] * jnp.broadcast_to(dec[0:1], (HEAD_DIM, D_STATE))
                    + upd[g][u0:u0 + HEAD_DIM])

    def per_group(i, carry):
        seqs = [i * SAMPLE_UNROLL + slot for slot in range(SAMPLE_UNROLL)]
        parts = [state_matmuls(b) for b in seqs]
        scored = [attention_scores(b, slot, parts[slot][0]) for slot, b in enumerate(seqs)]
        for slot, b in enumerate(seqs):
            store_state(b, *parts[slot])
        weights = [_attention_rows_probs(scores, sinks_ref) for scores, _ in scored]
        for slot in range(SAMPLE_UNROLL):
            att = _attention_rows_output(*weights[slot], scored[slot][1])
            _put_lane_blocks(res_scr, D_SSM, parts[slot][0], att)
        return carry

    lax.fori_loop(0, bb // SAMPLE_UNROLL, per_group, 0)

    for t in range(dl):
        own_rows = pl.ds(t, bb, stride=SEQ_ROWS)
        rows_t = slice(t * bb, (t + 1) * bb)
        y = ypart_scr[rows_t, :] + _get_lane_blocks(res_scr, 0, D_SSM, own_rows) * ecum_scr[rows_t, :]
        mix_ref[t, :, 0:D_SSM] = _gated_group_norm(y, z_ref[t], snorm_ref[...])
        mix_ref[t, :, D_SSM:D_SSM + D_ATTN] = _get_lane_blocks(res_scr, D_SSM, D_ATTN, own_rows)


def _mixer_sample(z, xbc, dtr, q, k, v, state_ssm, state_conv, cache_k, cache_v, pw, layer, carried):
    depth = state_ssm.shape[0]
    dl, nb = xbc.shape[0], xbc.shape[1]
    bb = min(SAMPLE_SEQS_PER_STEP, nb)
    assert bb & (bb - 1) == 0 and dl * bb <= SSD_CHUNK and dl <= SEQ_ROWS and bb % SAMPLE_UNROLL == 0
    tok = lambda width: pl.BlockSpec((dl, bb, width), lambda i: (0, i, 0))
    lblk = lambda *dims: pl.BlockSpec((None, bb) + dims, lambda i: (layer, i) + (0,) * len(dims))
    consts = _mixer_consts(pw)
    operands = (z, xbc, dtr, q, k, v, state_ssm, state_conv, cache_k, cache_v) + consts
    assert len(operands) == N_SAMPLE_IN
    seq_cols = D_ATTN + N_SSM_GROUPS * D_STATE + 2 * KV_WIDTH
    return pl.pallas_call(
        _mixer_sample_kernel,
        grid=(nb // bb,),
        in_specs=[tok(D_SSM), tok(CONV_DIM), tok(LANES), tok(D_ATTN), tok(KV_WIDTH), tok(KV_WIDTH),
                  lblk(D_SSM, D_STATE),
                  pl.BlockSpec((None, CONV_W - 1, bb, CONV_DIM), lambda i: (layer, 0, i, 0)),
                  lblk(WINDOW, KV_WIDTH), lblk(WINDOW, KV_WIDTH)]
                 + [_layer_block(cst, layer) for cst in consts]
                 + [pl.BlockSpec(memory_space=pl.ANY)] * len(carried),
        out_specs=[tok(D_SSM + D_ATTN), lblk(D_SSM, D_STATE), lblk(WINDOW, KV_WIDTH), lblk(WINDOW, KV_WIDTH)],
        out_shape=[jax.ShapeDtypeStruct((dl, nb, D_SSM + D_ATTN), F32),
                   jax.ShapeDtypeStruct((depth, nb, D_SSM, D_STATE), F32),
                   jax.ShapeDtypeStruct((depth, nb, WINDOW, KV_WIDTH), F32),
                   jax.ShapeDtypeStruct((depth, nb, WINDOW, KV_WIDTH), F32)],
        input_output_aliases={N_SAMPLE_IN + n: 1 + n for n in range(len(carried))},
        scratch_shapes=[pltpu.VMEM((dl * bb, D_SSM), F32),
                        pltpu.VMEM((dl * bb, D_SSM), F32),
                        pltpu.VMEM((bb, D_SSM), F32),
                        pltpu.VMEM((SSD_CHUNK, D_SSM), F32),
                        pltpu.VMEM((D_SSM, SSD_CHUNK), BF16),
                        pltpu.VMEM((SSD_CHUNK, N_SSM_GROUPS * D_STATE), F32),
                        pltpu.VMEM((seq_cols // LANES, bb * SEQ_ROWS, LANES), F32),
                        pltpu.VMEM(((D_SSM + D_ATTN) // LANES, bb * SEQ_ROWS, LANES), F32),
                        pltpu.VMEM((SAMPLE_UNROLL, KEY_PAD, KV_WIDTH), F32),
                        pltpu.VMEM((SAMPLE_UNROLL, KEY_PAD, KV_WIDTH), F32)],
        compiler_params=pltpu.CompilerParams(dimension_semantics=("arbitrary",),
                                             vmem_limit_bytes=VMEM_LIMIT),
        name="mixer_sample",
    )(*operands, *carried)


def _prepare_weights(p):
    bf = lambda w: w.astype(BF16)
    vec = lambda v: v[:, None, :]

    rep = lambda v: jnp.repeat(v, HEAD_DIM, axis=-1)[:, None, :]
    lane_pad = lambda v: jnp.pad(v, ((0, 0), (0, LANES - v.shape[-1])))[:, None, :]
    cuts = np.cumsum([D_SSM, CONV_DIM, N_SSM_HEADS, D_ATTN, KV_WIDTH])
    wz, wxbc, wdt, wq, wk, wv = jnp.split(bf(p['w_in']), cuts, axis=2)
    depth = p['w_in'].shape[0]
    seg = np.kron(np.eye(D_ATTN // HEAD_DIM, dtype=np.float32), np.ones((HEAD_DIM, HEAD_DIM), np.float32))
    heads = np.eye(LANES, dtype=np.float32)[:, :N_SSM_HEADS]
    expand = np.tile(np.kron(heads, np.ones((1, HEAD_DIM), np.float32)), (3, 1))
    return {
        'g_ffn1': vec(p['g_ffn1']), 'g_mix': vec(p['g_mix']), 'g_ffn2': vec(p['g_ffn2']), 'g_ple': vec(p['g_ple']),
        'w1_a': bf(p['w1_a']), 'w3_a': bf(p['w3_a']), 'w2_a': bf(p['w2_a']),
        'w1_b': bf(p['w1_b']), 'w3_b': bf(p['w3_b']), 'w2_b': bf(p['w2_b']),
        'w_in': jnp.concatenate(
            [wz, wxbc, wq, wk, wv, jnp.pad(wdt, ((0, 0), (0, 0), (0, LANES - N_SSM_HEADS)))], axis=2),
        'w_out': bf(p['w_out']), 'w_ple_gate': bf(p['w_ple_gate']), 'w_ple_proj': bf(p['w_ple_proj']),
        'q_gain': vec(jnp.tile(p['q_norm'], (1, N_HEADS))),
        'k_gain': vec(jnp.tile(p['k_norm'], (1, N_KV_HEADS))),
        'seg': jnp.broadcast_to(jnp.asarray(seg, BF16), (depth,) + seg.shape),
        'conv_w': p['conv_w'], 'conv_b': vec(p['conv_b']),
        'dt_bias': lane_pad(p['dt_bias']), 'a_log_rep': rep(p['a_log']),
        'd_skip': rep(p['d_skip']),
        'head_expand': jnp.broadcast_to(jnp.asarray(expand, BF16), (depth,) + expand.shape),
        'ssm_norm': vec(p['ssm_norm']),
        'sinks': jnp.broadcast_to(p['sinks'][:, :, None], (depth, N_HEADS, LANES)),
    }


def kernel(x_prompt, x_sample, state_ssm, state_conv, cache_k_win, cache_v_win, p_prompt, p_sample, g_ffn1, w1_a, w3_a, w2_a, g_mix, w_in, conv_w, conv_b, dt_bias, a_log, d_skip, ssm_norm, q_norm, k_norm, sinks, w_out, g_ffn2, w1_b, w3_b, w2_b, g_ple, w_ple_gate, w_ple_proj):
    params = dict(g_ffn1=g_ffn1, w1_a=w1_a, w3_a=w3_a, w2_a=w2_a, g_mix=g_mix, w_in=w_in, conv_w=conv_w,
                  conv_b=conv_b, dt_bias=dt_bias, a_log=a_log, d_skip=d_skip, ssm_norm=ssm_norm, q_norm=q_norm,
                  k_norm=k_norm, sinks=sinks, w_out=w_out, g_ffn2=g_ffn2, w1_b=w1_b, w3_b=w3_b, w2_b=w2_b,
                  g_ple=g_ple, w_ple_gate=w_ple_gate, w_ple_proj=w_ple_proj)
    depth = w_in.shape[0]
    bp, seq, _ = x_prompt.shape
    bs, dl, _ = x_sample.shape
    assert seq % MIXER_TILE == 0 and MIXER_TILE % SSD_CHUNK == 0 and seq >= WINDOW
    assert CONV_W - 1 <= dl <= SEQ_ROWS and bs % SAMPLE_SEQS_PER_STEP == 0
    assert (bp * seq) % TOKEN_TILE == 0 and (bp * seq) % STAGE_C_TILE == 0 and bs * dl <= TOKEN_TILE

    pw = _prepare_weights(params)
    sst_in = state_ssm.reshape(depth, bs, D_SSM, D_STATE)
    ck_in = cache_k_win.reshape(depth, bs, WINDOW, KV_WIDTH)
    cv_in = cache_v_win.reshape(depth, bs, WINDOW, KV_WIDTH)
    sconv_in = jnp.swapaxes(state_conv, 1, 2)
    pe_p = p_prompt.reshape(depth, bp * seq, D_PLE)
    pe_s = jnp.swapaxes(p_sample, 1, 2).reshape(depth, dl * bs, D_PLE)
    hp = x_prompt.reshape(bp * seq, D_MODEL)
    hs = jnp.swapaxes(x_sample, 0, 1).reshape(dl * bs, D_MODEL)
    ssm_p, conv_p, k_p, v_p, conv_s = [], [], [], [], []
    carried = ()
    for l in range(depth):
        hp, zp, xbcp, dtp, qp, kp, vp = _stage_a(hp, pw, l)
        hs, zs, xbcs, dts, qs, ks, vs = _stage_a(hs, pw, l)
        mixp, sp = _mixer_prompt(zp, xbcp, dtp, qp, kp, vp, pw, l, bp, seq)
        hp = _stage_c(hp, mixp, pe_p, pw, l)
        tm3 = lambda a: a.reshape(dl, bs, a.shape[-1])
        mixs, *carried = _mixer_sample(tm3(zs), tm3(xbcs), tm3(dts), tm3(qs), tm3(ks), tm3(vs),
                                       sst_in, sconv_in, ck_in, cv_in, pw, l, tuple(carried))
        hs = _stage_c(hs, mixs.reshape(dl * bs, D_MODEL), pe_s, pw, l)
        ssm_p.append(sp.reshape(bp, N_SSM_HEADS, HEAD_DIM, D_STATE))
        conv_p.append(xbcp.reshape(bp, seq, CONV_DIM)[:, seq - (CONV_W - 1):])
        last_window = lambda a: a.reshape(bp, seq, KV_WIDTH)[:, seq - WINDOW:].reshape(
            bp, WINDOW, N_KV_HEADS, HEAD_DIM)
        k_p.append(last_window(kp))
        v_p.append(last_window(vp))
        conv_s.append(jnp.swapaxes(tm3(xbcs)[dl - (CONV_W - 1):], 0, 1))
    ssm_s, k_s, v_s = carried
    return (hp.reshape(bp, seq, D_MODEL), jnp.swapaxes(hs.reshape(dl, bs, D_MODEL), 0, 1),
            jnp.stack(ssm_p), jnp.stack(conv_p), jnp.stack(k_p), jnp.stack(v_p),
            ssm_s.reshape(depth, bs, N_SSM_HEADS, HEAD_DIM, D_STATE), jnp.stack(conv_s),
            k_s.reshape(depth, bs, WINDOW, N_KV_HEADS, HEAD_DIM),
            v_s.reshape(depth, bs, WINDOW, N_KV_HEADS, HEAD_DIM))
```
